```python
import jax, jax.numpy as jnp
from jax import lax
import numpy as np

D_MODEL = 1024
BATCH = 4
SEQ = 4096
DEPTH = 2
DEC_BATCH = 2
DEC_SEQ = 16384
PAST_LEN = 128

N_META = 16
GRID_W = 64
N_HEADS_A = 8
N_KV_HEADS = 2
HEAD_DIM = 64
ROPE_AXIS_DIM = HEAD_DIM // 2
ROPE_THETA = 10000.0
Q_BLOCK = 128
N_HEADS_D = 8
DK = 64
DV = 64
CONV_K = 3
CHUNK = 64
N_EXPERTS = 16
EC_CAPACITY = 2
D_EXPERT = 1024
EPS = 1e-6

ATT_Q_W = N_HEADS_A * HEAD_DIM
ATT_KV_W = N_KV_HEADS * HEAD_DIM
DN_QK_W = N_HEADS_D * DK
DN_V_W = N_HEADS_D * DV
SPLITS = (ATT_Q_W, ATT_KV_W, ATT_KV_W, DN_QK_W, DN_QK_W, DN_V_W, DN_V_W,
          N_HEADS_D, N_HEADS_D, N_HEADS_D, N_HEADS_D, D_MODEL, D_MODEL)
D_IN = ATT_Q_W + 2 * ATT_KV_W + 2 * DN_QK_W + 2 * DN_V_W + 4 * N_HEADS_D + 2 * D_MODEL

kernel_name = "hybrid_gqa_gdn_ec_encoder"


def rmsnorm(x, g):
    xf = x.astype(jnp.float32)
    y = xf * lax.rsqrt(jnp.mean(xf * xf, axis=-1, keepdims=True) + EPS)
    return (y * g.astype(jnp.float32)).astype(x.dtype)


def l2norm(x):
    xf = x.astype(jnp.float32)
    return xf * lax.rsqrt(jnp.sum(xf * xf, axis=-1, keepdims=True) + EPS)


def axial_rope_tables(n_tokens):
    rows = n_tokens // GRID_W
    r = jnp.repeat(jnp.arange(rows), GRID_W)
    c = jnp.tile(jnp.arange(GRID_W), rows)
    pos = jnp.stack([r, c], axis=-1).astype(jnp.float32)
    pos = jnp.concatenate([jnp.zeros((N_META, 2), jnp.float32), pos], axis=0)
    n_freq = ROPE_AXIS_DIM // 2
    inv_freq = ROPE_THETA ** (-jnp.arange(n_freq, dtype=jnp.float32) / n_freq)
    ang = pos[:, :, None] * inv_freq
    return jnp.cos(ang), jnp.sin(ang)


def apply_axial_rope(x, cos, sin):
    B, L, H, _ = x.shape
    n_freq = ROPE_AXIS_DIM // 2
    xs = x.astype(jnp.float32).reshape(B, L, H, 2, 2, n_freq)
    x1, x2 = xs[..., 0, :], xs[..., 1, :]
    c = cos[None, :, None]
    s = sin[None, :, None]
    out = jnp.stack([x1 * c - x2 * s, x1 * s + x2 * c], axis=-2)
    return out.reshape(x.shape).astype(x.dtype)


def gqa_attention(q, k, v):
    B, L = q.shape[0], q.shape[1]
    G = N_HEADS_A // N_KV_HEADS
    scale = HEAD_DIM ** -0.5

    def attend(qb):
        Q = qb.shape[1]
        qg = qb.reshape(B, Q, N_KV_HEADS, G, HEAD_DIM)
        s = jnp.einsum('bqkgd,blkd->bkgql', qg, k, preferred_element_type=jnp.float32) * scale
        p = jax.nn.softmax(s, axis=-1).astype(v.dtype)
        o = jnp.einsum('bkgql,blkd->bqkgd', p, v)
        return o.reshape(B, Q, N_HEADS_A * HEAD_DIM)

    o_meta = attend(q[:, :N_META])
    n_blocks = (L - N_META) // Q_BLOCK
    qr = q[:, N_META:].reshape(B, n_blocks, Q_BLOCK, N_HEADS_A, HEAD_DIM).transpose(1, 0, 2, 3, 4)
    o_real = lax.map(attend, qr)
    o_real = o_real.transpose(1, 0, 2, 3).reshape(B, L - N_META, ATT_Q_W)
    return jnp.concatenate([o_meta, o_real], axis=1)


def centred_depthwise_conv(x, w):
    C = x.shape[-1]
    return lax.conv_general_dilated(
        x, w[:, None, :].astype(x.dtype), window_strides=(1,),
        padding=[(CONV_K // 2, CONV_K // 2)],
        dimension_numbers=('NWC', 'WIO', 'NWC'), feature_group_count=C)


def gated_delta_chunked(q, k, v, g, beta):
    B, T, H, _ = k.shape
    n_chunks = T // CHUNK

    def chunks(x):
        return x.reshape(B, n_chunks, CHUNK, H, -1).transpose(1, 0, 3, 2, 4)

    qc = chunks(q * (DK ** -0.5))
    kc = chunks(k)
    vc = chunks(v)
    gc = chunks(g[..., None])[..., 0]
    bc = chunks(beta[..., None])[..., 0]
    gcum = jnp.cumsum(gc, axis=-1)
    causal = jnp.tril(jnp.ones((CHUNK, CHUNK), bool))
    strict = jnp.tril(jnp.ones((CHUNK, CHUNK), bool), -1)
    diff = gcum[..., :, None] - gcum[..., None, :]
    decay = jnp.where(causal, jnp.exp(jnp.where(causal, diff, 0.0)), 0.0)
    kb = kc * bc[..., None]
    vb = vc * bc[..., None]
    lmat = jnp.where(strict, jnp.einsum('nbhcd,nbhsd->nbhcs', kb, kc) * decay, 0.0)
    eye = jnp.eye(CHUNK, dtype=jnp.float32)
    tmat = lax.linalg.triangular_solve(eye + lmat, jnp.broadcast_to(eye, lmat.shape),
                                       left_side=True, lower=True, unit_diagonal=True)
    u = tmat @ vb
    w = tmat @ (kb * jnp.exp(gcum)[..., None])
    a_intra = jnp.einsum('nbhcd,nbhsd->nbhcs', qc, kc) * decay
    q_dec = qc * jnp.exp(gcum)[..., None]
    k_dec = kc * jnp.exp(gcum[..., -1:] - gcum)[..., None]
    g_last = jnp.exp(gcum[..., -1])

    def step(state, xs):
        u_n, w_n, qd_n, kd_n, a_n, gl_n = xs
        v_new = u_n - w_n @ state
        o_n = qd_n @ state + a_n @ v_new
        state = state * gl_n[..., None, None] + jnp.swapaxes(kd_n, -1, -2) @ v_new
        return state, o_n

    state0 = jnp.zeros((B, H, DK, DV), jnp.float32)
    _, o = lax.scan(step, state0, (u, w, q_dec, k_dec, a_intra, g_last))
    return o.transpose(1, 0, 3, 2, 4).reshape(B, T, H, DV)


def hybrid_mixer(u, w_in, q_norm, k_norm, conv_w, a_log, dt_bias, o_norm,
                 w_attn_proj, w_delta_proj, w_out, cos, sin):
    B, L, _ = u.shape
    proj = u @ w_in
    (q_a, k_a, v_a, q_d, k_d, v_d, z_d, a_f, a_b, b_f, b_b, gate_a, gate_d) = jnp.split(
        proj, np.cumsum(SPLITS)[:-1].tolist(), axis=-1)

    q_a = apply_axial_rope(rmsnorm(q_a.reshape(B, L, N_HEADS_A, HEAD_DIM), q_norm), cos, sin)
    k_a = apply_axial_rope(rmsnorm(k_a.reshape(B, L, N_KV_HEADS, HEAD_DIM), k_norm), cos, sin)
    v_a = v_a.reshape(B, L, N_KV_HEADS, HEAD_DIM)
    y_a = gqa_attention(q_a, k_a, v_a) @ w_attn_proj

    qkv = jax.nn.silu(centred_depthwise_conv(jnp.concatenate([q_d, k_d, v_d], axis=-1), conv_w))
    q_d, k_d, v_d = jnp.split(qkv, [DN_QK_W, 2 * DN_QK_W], axis=-1)
    q_d = l2norm(q_d.reshape(B, L, N_HEADS_D, DK))
    k_d = l2norm(k_d.reshape(B, L, N_HEADS_D, DK))
    v_d = v_d.reshape(B, L, N_HEADS_D, DV).astype(jnp.float32)
    rate = jnp.exp(a_log.astype(jnp.float32))
    dtb = dt_bias.astype(jnp.float32)
    g_f = -rate[0] * jax.nn.softplus(a_f.astype(jnp.float32) + dtb[0])
    g_b = -rate[1] * jax.nn.softplus(a_b.astype(jnp.float32) + dtb[1])
    beta_f = jax.nn.sigmoid(b_f.astype(jnp.float32))
    beta_b = jax.nn.sigmoid(b_b.astype(jnp.float32))
    n_pad = CHUNK - N_META

    def padt(x):
        return jnp.pad(x, ((0, 0), (n_pad, 0)) + ((0, 0),) * (x.ndim - 2))

    def flip(x):
        return jnp.flip(x, axis=1)

    qp, kp, vp = padt(q_d), padt(k_d), padt(v_d)
    o_f = gated_delta_chunked(qp, kp, vp, padt(g_f), padt(beta_f))
    o_b = flip(gated_delta_chunked(flip(qp), flip(kp), flip(vp), flip(padt(g_b)), flip(padt(beta_b))))
    o_d = (o_f + o_b)[:, n_pad:]
    o_d = rmsnorm(o_d, o_norm) * jax.nn.silu(z_d.reshape(B, L, N_HEADS_D, DV).astype(jnp.float32))
    y_d = o_d.reshape(B, L, DN_V_W).astype(u.dtype) @ w_delta_proj

    merged = jax.nn.sigmoid(gate_a) * y_a + jax.nn.sigmoid(gate_d) * y_d
    return merged @ w_out


def expert_choice_ffn(u, w_router, w_gate, w_up, w_down):
    B, L, D = u.shape
    n_tok = B * L
    cap = EC_CAPACITY * n_tok // N_EXPERTS
    xf = u.reshape(n_tok, D)
    aff = jax.nn.softmax(jnp.einsum('nd,de->ne', xf, w_router, preferred_element_type=jnp.float32), axis=-1)
    gates, idx = lax.top_k(aff.T, cap)
    xe = xf[idx]
    h = jax.nn.silu(jnp.einsum('ecd,edf->ecf', xe, w_gate)) * jnp.einsum('ecd,edf->ecf', xe, w_up)
    ye = jnp.einsum('ecf,efd->ecd', h, w_down) * gates[..., None].astype(u.dtype)
    out = jnp.zeros_like(xf).at[idx.reshape(-1)].add(ye.reshape(-1, D))
    return out.reshape(B, L, D)


def encoder_trunk(x, meta_tokens, norm_mix, w_in, q_norm, k_norm, conv_w, a_log, dt_bias,
                  o_norm, w_attn_proj, w_delta_proj, w_out, norm_ffn, w_router, w_gate,
                  w_up, w_down, norm_final):
    B, S, D = x.shape
    cos, sin = axial_rope_tables(S)
    h = jnp.concatenate([jnp.broadcast_to(meta_tokens[None].astype(x.dtype), (B, N_META, D)), x], axis=1)
    for l in range(DEPTH):
        h = h + hybrid_mixer(rmsnorm(h, norm_mix[l]), w_in[l], q_norm[l], k_norm[l], conv_w[l],
                             a_log[l], dt_bias[l], o_norm[l], w_attn_proj[l], w_delta_proj[l],
                             w_out[l], cos, sin)
        h = h + expert_choice_ffn(rmsnorm(h, norm_ffn[l]), w_router[l], w_gate[l], w_up[l], w_down[l])
    return rmsnorm(h, norm_final)[:, N_META:]


def setup_inputs(seed: int = 0) -> dict:
    key = jax.random.key(seed)
    ks = jax.random.split(key, 24)
    f32 = jnp.float32
    nrm = lambda k, shape, scale: jax.random.normal(k, shape, f32) * scale
    gain = lambda k, shape: 1.0 + 0.01 * jax.random.normal(k, shape, f32)
    dt = jnp.exp(jax.random.uniform(ks[9], (DEPTH, 2, N_HEADS_D), f32,
                                    minval=float(np.log(1e-3)), maxval=float(np.log(0.1))))
    return {
        "x_prompt": nrm(ks[0], (BATCH, SEQ, D_MODEL), 1.0),
        "x_sample": nrm(ks[1], (DEC_BATCH, DEC_SEQ, D_MODEL), 1.0),
        "meta_tokens": nrm(ks[2], (N_META, D_MODEL), 1.0),
        "norm_mix": gain(ks[3], (DEPTH, D_MODEL)),
        "w_in": nrm(ks[4], (DEPTH, D_MODEL, D_IN), D_MODEL ** -0.5),
        "q_norm": gain(ks[5], (DEPTH, HEAD_DIM)),
        "k_norm": gain(ks[6], (DEPTH, HEAD_DIM)),
        "conv_w": nrm(ks[7], (DEPTH, CONV_K, 2 * DN_QK_W + DN_V_W), CONV_K ** -0.5),
        "a_log": jnp.log(jax.random.uniform(ks[8], (DEPTH, 2, N_HEADS_D), f32, minval=1.0, maxval=16.0)),
        "dt_bias": dt + jnp.log(-jnp.expm1(-dt)),
        "o_norm": gain(ks[10], (DEPTH, DV)),
        "w_attn_proj": nrm(ks[11], (DEPTH, ATT_Q_W, D_MODEL), ATT_Q_W ** -0.5),
        "w_delta_proj": nrm(ks[12], (DEPTH, DN_V_W, D_MODEL), DN_V_W ** -0.5),
        "w_out": nrm(ks[13], (DEPTH, D_MODEL, D_MODEL), D_MODEL ** -0.5),
        "norm_ffn": gain(ks[14], (DEPTH, D_MODEL)),
        "w_router": nrm(ks[15], (DEPTH, D_MODEL, N_EXPERTS), D_MODEL ** -0.5),
        "w_gate": nrm(ks[16], (DEPTH, N_EXPERTS, D_MODEL, D_EXPERT), D_MODEL ** -0.5),
        "w_up": nrm(ks[17], (DEPTH, N_EXPERTS, D_MODEL, D_EXPERT), D_MODEL ** -0.5),
        "w_down": nrm(ks[18], (DEPTH, N_EXPERTS, D_EXPERT, D_MODEL), D_EXPERT ** -0.5),
        "norm_final": gain(ks[19], (D_MODEL,)),
    }


def reference(x_prompt, x_sample, meta_tokens, norm_mix, w_in, q_norm, k_norm, conv_w, a_log,
              dt_bias, o_norm, w_attn_proj, w_delta_proj, w_out, norm_ffn, w_router, w_gate,
              w_up, w_down, norm_final):
    y_prompt = encoder_trunk(x_prompt, meta_tokens, norm_mix, w_in, q_norm, k_norm, conv_w, a_log,
                             dt_bias, o_norm, w_attn_proj, w_delta_proj, w_out, norm_ffn, w_router,
                             w_gate, w_up, w_down, norm_final)
    y_sample = encoder_trunk(x_sample, meta_tokens, norm_mix, w_in, q_norm, k_norm, conv_w, a_log,
                             dt_bias, o_norm, w_attn_proj, w_delta_proj, w_out, norm_ffn, w_router,
                             w_gate, w_up, w_down, norm_final)
    return (y_prompt, y_sample)
```

```python
import functools

import numpy as np
import jax
import jax.numpy as jnp
from jax import lax
from jax.experimental import pallas as pl
from jax.experimental.pallas import tpu as pltpu

F32 = jnp.float32
BF16 = jnp.bfloat16

D_MODEL = 1024
N_META = 16
GRID_W = 64
N_HEADS_A = 8
N_KV_HEADS = 2
HEAD_DIM = 64
ROPE_THETA = 10000.0
N_HEADS_D = 8
DK = 64
DV = 64
CHUNK = 64
N_EXPERTS = 16
EC_CAPACITY = 2
D_EXPERT = 1024
EPS = 1e-6

FRONT = 128
N_NULL = FRONT - N_META
LANES = 128
ROW_TILE = 384
HEAD_GROUP = 4
GROUP_W = HEAD_GROUP * DK
VMEM_LIMIT = 48 * 1024 * 1024

C_Q = 0
C_K = C_Q + N_HEADS_A * LANES
C_V = C_K + LANES
C_QKVD = C_V + LANES
C_Z = C_QKVD + 3 * N_HEADS_D * DK
C_SM = C_Z + N_HEADS_D * DV
C_GA = C_SM + LANES
C_GD = C_GA + D_MODEL
C_END = C_GD + D_MODEL


def _cparams(sem):
    return pltpu.CompilerParams(dimension_semantics=sem, vmem_limit_bytes=VMEM_LIMIT)


def _dot(a, b):
    return jnp.dot(a, b, preferred_element_type=F32)


def _dot_nt(a, b):
    return lax.dot_general(a, b, (((1,), (1,)), ((), ())), preferred_element_type=F32)


def _split_dot(t, w_bf16):
    hi = t.astype(BF16)
    lo = (t - hi.astype(F32)).astype(BF16)
    return _dot(hi, w_bf16) + _dot(lo, w_bf16)


def _inproj_kernel(x_ref, g_ref, w_ref, cos_ref, sin_ref, qn_ref, kn_ref,
                   q_ref, k_ref, v_ref, qkvd_ref, z_ref, sm_ref, ga_ref, gd_ref):
    x = x_ref[...]
    ms = jnp.mean(x * x, axis=-1, keepdims=True)
    u = (x * lax.rsqrt(ms + EPS) * g_ref[...]).astype(BF16)
    cos = cos_ref[...]
    sin = sin_ref[...]
    lane = lax.broadcasted_iota(jnp.int32, (1, LANES), 1)
    first = (lane % 32) < 16
    low = lane < HEAD_DIM

    def rope(t):
        rot = jnp.where(first, pltpu.roll(t, LANES - 16, 1), pltpu.roll(t, 16, 1))
        return t * cos + rot * sin

    def mm(c0, n):
        return _dot(u, w_ref[:, c0:c0 + n])

    for h in range(N_HEADS_A):
        t = mm(C_Q + LANES * h, LANES)
        msq = jnp.sum(t * t, axis=-1, keepdims=True) * (1.0 / HEAD_DIM)
        t = t * lax.rsqrt(msq + EPS) * qn_ref[...]
        q_ref[:, LANES * h:LANES * (h + 1)] = (rope(t) * (HEAD_DIM ** -0.5)).astype(BF16)

    t = mm(C_K, LANES)
    t2 = t * t
    s_lo = jnp.sum(jnp.where(low, t2, 0.0), axis=-1, keepdims=True)
    s_hi = jnp.sum(jnp.where(low, 0.0, t2), axis=-1, keepdims=True)
    msk = jnp.where(low, s_lo, s_hi) * (1.0 / HEAD_DIM)
    t = t * lax.rsqrt(msk + EPS) * kn_ref[...]
    k_ref[...] = rope(t).astype(BF16)
    v_ref[...] = mm(C_V, LANES).astype(BF16)
    qkvd_ref[...] = mm(C_QKVD, C_Z - C_QKVD)
    z_ref[...] = mm(C_Z, C_SM - C_Z).astype(BF16)
    sm_ref[...] = mm(C_SM, LANES)
    ga_ref[...] = mm(C_GA, D_MODEL).astype(BF16)
    gd_ref[...] = mm(C_GD, D_MODEL).astype(BF16)


def _inproj(h2, g, w_all, cos, sin, qn, kn, lp):
    n = h2.shape[0]
    tm = ROW_TILE
    nt = lp // tm
    row = lambda i: (i, 0)
    const = lambda i: (0, 0)
    tab = lambda i: (i % nt, 0)
    out_shapes = (
        jax.ShapeDtypeStruct((n, N_HEADS_A * LANES), BF16),
        jax.ShapeDtypeStruct((n, LANES), BF16),
        jax.ShapeDtypeStruct((n, LANES), BF16),
        jax.ShapeDtypeStruct((n, C_Z - C_QKVD), F32),
        jax.ShapeDtypeStruct((n, C_SM - C_Z), BF16),
        jax.ShapeDtypeStruct((n, LANES), F32),
        jax.ShapeDtypeStruct((n, D_MODEL), BF16),
        jax.ShapeDtypeStruct((n, D_MODEL), BF16),
    )
    return pl.pallas_call(
        _inproj_kernel,
        grid=(n // tm,),
        in_specs=[
            pl.BlockSpec((tm, D_MODEL), row),
            pl.BlockSpec((1, D_MODEL), const),
            pl.BlockSpec((D_MODEL, C_END), const, pipeline_mode=pl.Buffered(1)),
            pl.BlockSpec((tm, LANES), tab),
            pl.BlockSpec((tm, LANES), tab),
            pl.BlockSpec((1, LANES), const),
            pl.BlockSpec((1, LANES), const),
        ],
        out_specs=tuple(pl.BlockSpec((tm, s.shape[1]), row) for s in out_shapes),
        out_shape=out_shapes,
        compiler_params=_cparams(("parallel",)),
    )(h2, g, w_all, cos, sin, qn, kn)


def _flash_kernel(q_ref, k_ref, v_ref, o_ref, qs_ref, m_ref, l_ref, acc_ref, *, tq, tk):
    j = pl.program_id(1)
    ki = pl.program_id(3)
    nk = pl.num_programs(3)
    g = N_HEADS_A // N_KV_HEADS

    @pl.when(ki == 0)
    def _():
        for i in range(g):
            qs_ref[i * tq:(i + 1) * tq, :] = q_ref[0, :, LANES * i:LANES * (i + 1)]
        m_ref[...] = jnp.full(m_ref.shape, -jnp.inf, F32)
        l_ref[...] = jnp.zeros(l_ref.shape, F32)
        acc_ref[...] = jnp.zeros(acc_ref.shape, F32)

    def step(mask_null_keys):
        s = _dot_nt(qs_ref[...], k_ref[0])
        if mask_null_keys:
            col = lax.broadcasted_iota(jnp.int32, (1, tk), 1)
            s = jnp.where(col >= N_NULL, s, -jnp.inf)
        m_prev = m_ref[...]
        m_new = jnp.maximum(m_prev, jnp.max(s, axis=-1, keepdims=True))
        alpha = jnp.exp(m_prev - m_new)
        p = jnp.exp(s - m_new)
        l_ref[...] = alpha * l_ref[...] + jnp.sum(p, axis=-1, keepdims=True)
        acc_ref[...] = alpha * acc_ref[...] + _dot(p.astype(BF16), v_ref[0])
        m_ref[...] = m_new

    @pl.when(ki == 0)
    def _():
        step(True)

    @pl.when(ki != 0)
    def _():
        step(False)

    @pl.when(ki == nk - 1)
    def _():
        out = acc_ref[...] / l_ref[...]
        lane = lax.broadcasted_iota(jnp.int32, (1, LANES), 1)
        for p in range(g // 2):
            a = out[(2 * p) * tq:(2 * p + 1) * tq]
            b = out[(2 * p + 1) * tq:(2 * p + 2) * tq]
            a_lo = jnp.where(j == 0, a, pltpu.roll(a, HEAD_DIM, 1))
            b_hi = jnp.where(j == 1, b, pltpu.roll(b, HEAD_DIM, 1))
            o_ref[0, :, LANES * p:LANES * (p + 1)] = jnp.where(lane < HEAD_DIM, a_lo, b_hi).astype(BF16)


def _attention(q, k, v):
    b, lp, _ = q.shape
    tq = ROW_TILE
    tk = ROW_TILE
    g = N_HEADS_A // N_KV_HEADS
    return pl.pallas_call(
        functools.partial(_flash_kernel, tq=tq, tk=tk),
        grid=(b, N_KV_HEADS, lp // tq, lp // tk),
        in_specs=[
            pl.BlockSpec((1, tq, g * LANES), lambda bi, j, qi, ki: (bi, qi, j)),
            pl.BlockSpec((1, tk, LANES), lambda bi, j, qi, ki: (bi, ki, 0)),
            pl.BlockSpec((1, tk, LANES), lambda bi, j, qi, ki: (bi, ki, 0)),
        ],
        out_specs=pl.BlockSpec((1, tq, g * HEAD_DIM), lambda bi, j, qi, ki: (bi, qi, j)),
        out_shape=jax.ShapeDtypeStruct((b, lp, N_HEADS_A * HEAD_DIM), BF16),
        scratch_shapes=[
            pltpu.VMEM((g * tq, LANES), BF16),
            pltpu.VMEM((g * tq, 1), F32),
            pltpu.VMEM((g * tq, 1), F32),
            pltpu.VMEM((g * tq, LANES), F32),
        ],
        compiler_params=_cparams(("parallel", "parallel", "parallel", "arbitrary")),
    )(q, k, v)


def _dprep_kernel(prev_ref, cur_ref, next_ref, sm_ref, cw_ref, rate_ref, dtb_ref, seg_ref,
                  trif_ref, trib_ref, q_ref, k_ref, v_ref, gcf_ref, gcb_ref, bf_ref, bb_ref,
                  *, tm, n_tiles):
    i = pl.program_id(1)
    row = lax.broadcasted_iota(jnp.int32, (tm, 1), 0)
    valid = (row + i * tm) >= N_NULL
    x = jnp.where(valid, cur_ref[0], 0.0)
    prow = jnp.where(i > 0, prev_ref[0, 7:8, :], 0.0)
    nrow = jnp.where(i < n_tiles - 1, next_ref[0, 0:1, :], 0.0)
    xp = jnp.where(row == 0, prow, pltpu.roll(x, 1, 0))
    xn = jnp.where(row == tm - 1, nrow, pltpu.roll(x, tm - 1, 0))
    cw = cw_ref[...]
    y = cw[0:1] * xp + cw[1:2] * x + cw[2:3] * xn
    y = y * jax.nn.sigmoid(y)

    w = N_HEADS_D * DK
    seg = seg_ref[...]
    q = y[:, 0:w]
    q = q * lax.rsqrt(_split_dot(q * q, seg) + EPS) * (DK ** -0.5)
    k = y[:, w:2 * w]
    k = k * lax.rsqrt(_split_dot(k * k, seg) + EPS)
    q_ref[0] = jnp.where(valid, q, 0.0)
    k_ref[0] = jnp.where(valid, k, 0.0)
    v_ref[0] = jnp.where(valid, y[:, 2 * w:3 * w], 0.0)

    sm = sm_ref[0]
    t = sm + dtb_ref[...]
    softplus = jnp.maximum(t, 0.0) + jnp.log1p(jnp.exp(-jnp.abs(t)))
    g_all = jnp.where(valid, -rate_ref[...] * softplus, 0.0)
    beta_all = jnp.where(valid, jax.nn.sigmoid(sm), 0.0)
    gc_f = jnp.dot(trif_ref[...], g_all, preferred_element_type=F32, precision=lax.Precision.HIGHEST)
    gc_b = jnp.dot(trib_ref[...], g_all, preferred_element_type=F32, precision=lax.Precision.HIGHEST)

    lane = lax.broadcasted_iota(jnp.int32, (1, LANES), 1)
    low = lane < DK

    def expand(a, c0, out_ref):
        for p in range(N_HEADS_D // 2):
            e0 = jnp.broadcast_to(a[:, c0 + 2 * p:c0 + 2 * p + 1], (tm, LANES))
            e1 = jnp.broadcast_to(a[:, c0 + 2 * p + 1:c0 + 2 * p + 2], (tm, LANES))
            out_ref[0, :, LANES * p:LANES * (p + 1)] = jnp.where(low, e0, e1)

    expand(gc_f, 0, gcf_ref)
    expand(gc_b, N_HEADS_D, gcb_ref)
    expand(beta_all, 2 * N_HEADS_D, bf_ref)
    expand(beta_all, 3 * N_HEADS_D, bb_ref)


def _delta_prep(qkvd, sm, cw, rate, dtb, seg, trif, trib):
    b, lp, c = qkvd.shape
    tm = ROW_TILE
    nt = lp // tm
    w = N_HEADS_D * DK
    halo = 8
    nb8 = lp // halo
    cur = lambda bi, i: (bi, i, 0)
    const = lambda bi, i: (0, 0)
    out = jax.ShapeDtypeStruct((b, lp, w), F32)
    return pl.pallas_call(
        functools.partial(_dprep_kernel, tm=tm, n_tiles=nt),
        grid=(b, nt),
        in_specs=[
            pl.BlockSpec((1, halo, c), lambda bi, i: (bi, jnp.maximum(i * (tm // halo) - 1, 0), 0)),
            pl.BlockSpec((1, tm, c), cur),
            pl.BlockSpec((1, halo, c), lambda bi, i: (bi, jnp.minimum((i + 1) * (tm // halo), nb8 - 1), 0)),
            pl.BlockSpec((1, tm, LANES), cur),
            pl.BlockSpec((3, c), const),
            pl.BlockSpec((1, LANES), const),
            pl.BlockSpec((1, LANES), const),
            pl.BlockSpec((w, w), const),
            pl.BlockSpec((tm, tm), const),
            pl.BlockSpec((tm, tm), const),
        ],
        out_specs=tuple(pl.BlockSpec((1, tm, w), cur) for _ in range(7)),
        out_shape=(out,) * 7,
        compiler_params=_cparams(("parallel", "parallel")),
    )(qkvd, qkvd, qkvd, sm, cw, rate, dtb, seg, trif, trib)


def _dscan_kernel(qf_ref, kf_ref, vf_ref, gf_ref, bf_ref, qb_ref, kb_ref, vb_ref, gb_ref, bb_ref,
                  of_ref, ob_ref, s_ref):
    i = pl.program_id(1)

    @pl.when(i == 0)
    def _():
        s_ref[...] = jnp.zeros(s_ref.shape, F32)

    c = CHUNK
    r = lax.broadcasted_iota(jnp.int32, (c, GROUP_W), 0)
    cj = lax.broadcasted_iota(jnp.int32, (c, GROUP_W), 1) % c
    eye = r == cj
    eyef = eye.astype(F32)
    bdmask = (lax.broadcasted_iota(jnp.int32, (GROUP_W, GROUP_W), 0) // c ==
              lax.broadcasted_iota(jnp.int32, (GROUP_W, GROUP_W), 1) // c)

    def bd(t):
        return jnp.where(bdmask, jnp.concatenate([t] * HEAD_GROUP, axis=0), 0.0).astype(BF16)

    def one(q, k, v, gc, beta, sidx, reverse):
        incl = (r <= cj) if reverse else (r >= cj)
        strict = (r < cj) if reverse else (r > cj)
        gcol = jnp.sum(jnp.where(eye, gc, 0.0), axis=0, keepdims=True)
        decay = jnp.where(incl, jnp.exp(jnp.where(incl, gc - gcol, 0.0)), 0.0)
        eg = jnp.exp(gc)
        glast = gc[0:1] if reverse else gc[c - 1:c]
        kbeta = k * beta
        gq = _dot_nt(jnp.concatenate([kbeta, q], axis=0).astype(BF16), bd(k))
        lmat = jnp.where(strict, gq[:c] * decay, 0.0)
        a_intra = gq[c:] * decay
        m = -lmat
        p = eyef + m
        m = _dot(m.astype(BF16), bd(m))
        for _ in range(4):
            rr = _dot(jnp.concatenate([m, p], axis=0).astype(BF16), bd(m))
            p = p + rr[c:]
            m = rr[:c]
        tmat = (p + _dot(p.astype(BF16), bd(m))).astype(BF16)
        u = _dot(tmat, bd(v * beta))
        w = _dot(tmat, bd(kbeta * eg))
        state = s_ref[sidx]
        wq = _dot(jnp.concatenate([w, q * eg], axis=0).astype(BF16), state.astype(BF16))
        v_new = u - wq[:c]
        o = wq[c:] + _dot(a_intra.astype(BF16), bd(v_new))
        kdec = k * jnp.exp(glast - gc)
        upd = _dot(kdec.T.astype(BF16), v_new.astype(BF16))
        s_ref[sidx] = state * jnp.exp(glast) + jnp.where(bdmask, upd, 0.0)
        return o

    for grp in range(N_HEADS_D // HEAD_GROUP):
        sl = slice(GROUP_W * grp, GROUP_W * (grp + 1))
        of_ref[0, :, sl] = one(qf_ref[0, :, sl], kf_ref[0, :, sl], vf_ref[0, :, sl],
                               gf_ref[0, :, sl], bf_ref[0, :, sl], grp, False)
        ob_ref[0, :, sl] = one(qb_ref[0, :, sl], kb_ref[0, :, sl], vb_ref[0, :, sl],
                               gb_ref[0, :, sl], bb_ref[0, :, sl], N_HEADS_D // HEAD_GROUP + grp, True)


def _delta_scan(q, k, v, gcf, gcb, bf, bb):
    b, lp, w = q.shape
    n = lp // CHUNK
    fwd = lambda bi, i: (bi, i, 0)
    bwd = lambda bi, i: (bi, n - 1 - i, 0)
    blk = (1, CHUNK, w)
    out = jax.ShapeDtypeStruct((b, lp, w), F32)
    return pl.pallas_call(
        _dscan_kernel,
        grid=(b, n),
        in_specs=[pl.BlockSpec(blk, fwd)] * 5 + [pl.BlockSpec(blk, bwd)] * 5,
        out_specs=(pl.BlockSpec(blk, fwd), pl.BlockSpec(blk, bwd)),
        out_shape=(out, out),
        scratch_shapes=[pltpu.VMEM((2 * N_HEADS_D // HEAD_GROUP, GROUP_W, GROUP_W), F32)],
        compiler_params=_cparams(("parallel", "arbitrary")),
    )(q, k, v, gcf, bf, q, k, v, gcb, bb)


def _post_kernel(h_ref, at_ref, of_ref, ob_ref, z_ref, ga_ref, gd_ref, on_ref, seg_ref,
                 wa_ref, wd_ref, wo_ref, gf_ref, wrh_ref, wrl_ref,
                 h_out, xn_out, aff_out, *, tm, n_tiles):
    od = of_ref[...] + ob_ref[...]
    msd = _split_dot(od * od, seg_ref[...]) * (1.0 / DV)
    z = z_ref[...].astype(F32)
    od = od * lax.rsqrt(msd + EPS) * on_ref[...] * (z * jax.nn.sigmoid(z))
    y_d = _dot(od.astype(BF16), wd_ref[...])
    y_a = _dot(at_ref[...], wa_ref[...])
    merged = (jax.nn.sigmoid(ga_ref[...].astype(F32)) * y_a +
              jax.nn.sigmoid(gd_ref[...].astype(F32)) * y_d)
    h = h_ref[...] + _dot(merged.astype(BF16), wo_ref[...])
    h_out[...] = h

    ms = jnp.mean(h * h, axis=-1, keepdims=True)
    xn = h * lax.rsqrt(ms + EPS) * gf_ref[...]
    xh = xn.astype(BF16)
    xn_out[...] = xh
    xl = (xn - xh.astype(F32)).astype(BF16)
    logits = _dot(xh, wrh_ref[...]) + _dot(xl, wrh_ref[...]) + _dot(xh, wrl_ref[...])
    lane = lax.broadcasted_iota(jnp.int32, (1, LANES), 1)
    logits = jnp.where(lane < N_EXPERTS, logits, -jnp.inf)
    e = jnp.exp(logits - jnp.max(logits, axis=-1, keepdims=True))
    aff = e / jnp.sum(e, axis=-1, keepdims=True)
    row = lax.broadcasted_iota(jnp.int32, (tm, 1), 0) + (pl.program_id(0) % n_tiles) * tm
    aff_out[...] = jnp.where(row >= N_NULL, aff, -1.0)


def _post(h2, at, of, ob, z, ga, gd, on, seg, wa, wd, wo, gf, wrh, wrl, lp):
    n = h2.shape[0]
    tm = ROW_TILE
    nt = lp // tm
    row = lambda i: (i, 0)
    const = lambda i: (0, 0)
    hw = N_HEADS_D * DV

    def wspec(shape):
        return pl.BlockSpec(shape, const, pipeline_mode=pl.Buffered(1))

    return pl.pallas_call(
        functools.partial(_post_kernel, tm=tm, n_tiles=nt),
        grid=(n // tm,),
        in_specs=[
            pl.BlockSpec((tm, D_MODEL), row),
            pl.BlockSpec((tm, hw), row),
            pl.BlockSpec((tm, hw), row),
            pl.BlockSpec((tm, hw), row),
            pl.BlockSpec((tm, hw), row),
            pl.BlockSpec((tm, D_MODEL), row),
            pl.BlockSpec((tm, D_MODEL), row),
            pl.BlockSpec((1, hw), const),
            wspec((hw, hw)),
            wspec((hw, D_MODEL)),
            wspec((hw, D_MODEL)),
            wspec((D_MODEL, D_MODEL)),
            pl.BlockSpec((1, D_MODEL), const),
            wspec((D_MODEL, LANES)),
            wspec((D_MODEL, LANES)),
        ],
        out_specs=(pl.BlockSpec((tm, D_MODEL), row), pl.BlockSpec((tm, D_MODEL), row),
                   pl.BlockSpec((tm, LANES), row)),
        out_shape=(jax.ShapeDtypeStruct((n, D_MODEL), F32), jax.ShapeDtypeStruct((n, D_MODEL), BF16),
                   jax.ShapeDtypeStruct((n, LANES), F32)),
        compiler_params=_cparams(("parallel",)),
    )(h2, at, of, ob, z, ga, gd, on, seg, wa, wd, wo, gf, wrh, wrl)


def _expert_kernel(x_ref, gate_ref, wg_ref, wu_ref, wd_ref, y_ref):
    x = x_ref[0]
    a = _dot(x, wg_ref[0])
    b = _dot(x, wu_ref[0])
    hh = (a * jax.nn.sigmoid(a) * b).astype(BF16)
    y_ref[0] = _dot(hh, wd_ref[0]) * gate_ref[0]


def _experts(xe, gates, wg, wu, wd, tm):
    e, cp, d = xe.shape
    tile = lambda ei, i: (ei, i, 0)
    wsp = lambda ei, i: (ei, 0, 0)
    return pl.pallas_call(
        _expert_kernel,
        grid=(e, cp // tm),
        in_specs=[
            pl.BlockSpec((1, tm, d), tile),
            pl.BlockSpec((1, tm, 1), tile),
            pl.BlockSpec((1, d, D_EXPERT), wsp),
            pl.BlockSpec((1, d, D_EXPERT), wsp),
            pl.BlockSpec((1, D_EXPERT, d), wsp),
        ],
        out_specs=pl.BlockSpec((1, tm, d), tile),
        out_shape=jax.ShapeDtypeStruct((e, cp, d), F32),
        compiler_params=_cparams(("parallel", "arbitrary")),
    )(xe, gates, wg, wu, wd)


def _final_kernel(x_ref, g_ref, o_ref):
    x = x_ref[0]
    ms = jnp.mean(x * x, axis=-1, keepdims=True)
    o_ref[0] = x * lax.rsqrt(ms + EPS) * g_ref[...]


def _final_norm(h3, g, s):
    b = h3.shape[0]
    tm = FRONT
    return pl.pallas_call(
        _final_kernel,
        grid=(b, s // tm),
        in_specs=[pl.BlockSpec((1, tm, D_MODEL), lambda bi, i: (bi, i + FRONT // tm, 0)),
                  pl.BlockSpec((1, D_MODEL), lambda bi, i: (0, 0))],
        out_specs=pl.BlockSpec((1, tm, D_MODEL), lambda bi, i: (bi, i, 0)),
        out_shape=jax.ShapeDtypeStruct((b, s, D_MODEL), F32),
        compiler_params=_cparams(("parallel", "parallel")),
    )(h3, g)


def _rope_tables(s):
    lp = FRONT + s
    t = np.arange(lp) - FRONT
    real = t >= 0
    pos = np.stack([np.where(real, t // GRID_W, 0), np.where(real, t % GRID_W, 0)], axis=-1)
    n_freq = HEAD_DIM // 4
    inv_freq = jnp.asarray(ROPE_THETA, F32) ** (-jnp.arange(n_freq, dtype=F32) / n_freq)
    lane = np.arange(LANES) % HEAD_DIM
    axis = lane // (HEAD_DIM // 2)
    freq = lane % n_freq
    ang = jnp.asarray(pos, F32)[:, axis] * inv_freq[freq][None, :]
    sign = np.where((lane % (HEAD_DIM // 2)) < n_freq, -1.0, 1.0).astype(np.float32)
    return jnp.cos(ang), jnp.sin(ang) * sign[None, :]


def _pack_w_in(w):
    o = 0

    def take(n):
        nonlocal o
        part = w[:, o:o + n]
        o += n
        return part

    q_a = take(N_HEADS_A * HEAD_DIM).reshape(D_MODEL, N_HEADS_A, HEAD_DIM)
    k_a = take(N_KV_HEADS * HEAD_DIM)
    v_a = take(N_KV_HEADS * HEAD_DIM)
    qkv_d = take(3 * N_HEADS_D * DK)
    z_d = take(N_HEADS_D * DV)
    small = take(4 * N_HEADS_D)
    gate_a = take(D_MODEL)
    gate_d = take(D_MODEL)
    zq = jnp.zeros_like(q_a)
    kv_of_head = (np.arange(N_HEADS_A) // (N_HEADS_A // N_KV_HEADS))[None, :, None]
    q_spread = jnp.where(kv_of_head == 0, jnp.concatenate([q_a, zq], -1), jnp.concatenate([zq, q_a], -1))
    q_spread = q_spread.reshape(D_MODEL, N_HEADS_A * LANES)
    small = jnp.pad(small, ((0, 0), (0, LANES - small.shape[1])))
    return jnp.concatenate([q_spread, k_a, v_a, qkv_d, z_d, small, gate_a, gate_d], axis=1).astype(BF16)


def _chunk_tri(tm, reverse):
    i = np.arange(tm)
    same = (i[:, None] // CHUNK) == (i[None, :] // CHUNK)
    tri = (i[None, :] >= i[:, None]) if reverse else (i[None, :] <= i[:, None])
    return jnp.asarray((same & tri).astype(np.float32))


def _expert_tile(cap):
    n_tiles = -(-cap // 512)
    tm = -(-cap // n_tiles)
    tm = -(-tm // 16) * 16
    return tm, n_tiles


def _trunk(x, meta_tokens, layers, norm_final):
    b, s, d = x.shape
    lp = FRONT + s
    n = b * lp
    n_tok = b * (N_META + s)
    cap = EC_CAPACITY * n_tok // N_EXPERTS
    etm, ent = _expert_tile(cap)
    cap_pad = etm * ent
    cos, sin = _rope_tables(s)
    seg = jnp.asarray(np.kron(np.eye(N_HEADS_D), np.ones((DK, DK))), BF16)
    trif = _chunk_tri(ROW_TILE, False)
    trib = _chunk_tri(ROW_TILE, True)

    front = jnp.concatenate([jnp.zeros((N_NULL, d), F32), meta_tokens.astype(F32)], axis=0)
    h = jnp.concatenate([jnp.broadcast_to(front[None], (b, FRONT, d)), x], axis=1).reshape(n, d)

    for lw in layers:
        q, k, v, qkvd, z, sm, ga, gd = _inproj(h, lw["norm_mix"], lw["w_all"], cos, sin,
                                               lw["qn"], lw["kn"], lp)
        at = _attention(q.reshape(b, lp, -1), k.reshape(b, lp, -1), v.reshape(b, lp, -1))
        qd, kd, vd, gcf, gcb, bf, bb = _delta_prep(qkvd.reshape(b, lp, -1), sm.reshape(b, lp, -1),
                                                   lw["conv_w"], lw["rate"], lw["dtb"], seg, trif, trib)
        of, ob = _delta_scan(qd, kd, vd, gcf, gcb, bf, bb)
        hw = N_HEADS_D * DV
        h, xn, aff = _post(h, at.reshape(n, hw), of.reshape(n, hw), ob.reshape(n, hw), z, ga, gd,
                           lw["on"], seg, lw["w_attn_proj"], lw["w_delta_proj"], lw["w_out"],
                           lw["norm_ffn"], lw["wr_hi"], lw["wr_lo"], lp)
        gates, idx = lax.top_k(aff[:, :N_EXPERTS].T, cap)
        idx = jnp.pad(idx, ((0, 0), (0, cap_pad - cap)))
        gates = jnp.pad(gates, ((0, 0), (0, cap_pad - cap)))
        xe = jnp.take(xn, idx.reshape(-1), axis=0).reshape(N_EXPERTS, cap_pad, d)
        ye = _experts(xe, gates[..., None], lw["w_gate"], lw["w_up"], lw["w_down"], etm)
        h = h.at[idx.reshape(-1)].add(ye.reshape(-1, d))

    return _final_norm(h.reshape(b, lp, d), norm_final, s)


def kernel(x_prompt, x_sample, meta_tokens, norm_mix, w_in, q_norm, k_norm, conv_w, a_log, dt_bias,
           o_norm, w_attn_proj, w_delta_proj, w_out, norm_ffn, w_router, w_gate, w_up, w_down,
           norm_final):
    depth = w_in.shape[0]
    layers = []
    for l in range(depth):
        rate = jnp.exp(a_log[l].astype(F32)).reshape(1, -1)
        dtb = dt_bias[l].astype(F32).reshape(1, -1)
        pad = LANES - rate.shape[1]
        wr = jnp.pad(w_router[l].astype(F32), ((0, 0), (0, LANES - N_EXPERTS)))
        wr_hi = wr.astype(BF16)
        layers.append(dict(
            norm_mix=norm_mix[l].astype(F32).reshape(1, -1),
            w_all=_pack_w_in(w_in[l]),
            qn=jnp.tile(q_norm[l].astype(F32), LANES // HEAD_DIM).reshape(1, -1),
            kn=jnp.tile(k_norm[l].astype(F32), LANES // HEAD_DIM).reshape(1, -1),
            conv_w=conv_w[l].astype(F32),
            rate=jnp.pad(rate, ((0, 0), (0, pad))),
            dtb=jnp.pad(dtb, ((0, 0), (0, pad))),
            on=jnp.tile(o_norm[l].astype(F32), N_HEADS_D).reshape(1, -1),
            w_attn_proj=w_attn_proj[l].astype(BF16),
            w_delta_proj=w_delta_proj[l].astype(BF16),
            w_out=w_out[l].astype(BF16),
            norm_ffn=norm_ffn[l].astype(F32).reshape(1, -1),
            wr_hi=wr_hi,
            wr_lo=(wr - wr_hi.astype(F32)).astype(BF16),
            w_gate=w_gate[l].astype(BF16),
            w_up=w_up[l].astype(BF16),
            w_down=w_down[l].astype(BF16),
        ))
    nf = norm_final.astype(F32).reshape(1, -1)
    y_prompt = _trunk(x_prompt, meta_tokens, layers, nf)
    y_sample = _trunk(x_sample, meta_tokens, layers, nf)
    return (y_prompt, y_sample)
```

```python
import functools

import numpy as np
import jax
import jax.numpy as jnp
from jax import lax
from jax.experimental import pallas as pl
from jax.experimental.pallas import tpu as pltpu

F32 = jnp.float32
BF16 = jnp.bfloat16

D_MODEL = 1024
N_META = 16
GRID_W = 64
N_HEADS_A = 8
N_KV_HEADS = 2
HEAD_DIM = 64
ROPE_THETA = 10000.0
N_HEADS_D = 8
DK = 64
DV = 64
CHUNK = 64
N_EXPERTS = 16
EC_CAPACITY = 2
D_EXPERT = 1024
EPS = 1e-6

FRONT = 128
N_NULL = FRONT - N_META
LANES = 128
ROW_TILE = 384
HEAD_GROUP = 4
GROUP_W = HEAD_GROUP * DK
VMEM_LIMIT = 48 * 1024 * 1024
Q_SCALE = HEAD_DIM ** -0.5 * float(np.log2(np.e))

C_Q = 0
C_K = C_Q + N_HEADS_A * LANES
C_QKVD = C_K + N_KV_HEADS * LANES
C_Z = C_QKVD + 3 * N_HEADS_D * DK
C_SM = C_Z + N_HEADS_D * DV
C_GA = C_SM + LANES
C_GD = C_GA + D_MODEL
C_END = C_GD + D_MODEL


def _cparams(sem):
    return pltpu.CompilerParams(dimension_semantics=sem, vmem_limit_bytes=VMEM_LIMIT)


def _dot(a, b):
    return jnp.dot(a, b, preferred_element_type=F32)


def _dot_nt(a, b):
    return lax.dot_general(a, b, (((1,), (1,)), ((), ())), preferred_element_type=F32)


def _split_dot(t, w_bf16):
    hi = t.astype(BF16)
    lo = (t - hi.astype(F32)).astype(BF16)
    return _dot(hi, w_bf16) + _dot(lo, w_bf16)


def _inproj_kernel(x_ref, g_ref, w_ref, wvt_ref, cos_ref, sin_ref, qn_ref, kn_ref,
                   q_ref, k_ref, vt_ref, qkvd_ref, z_ref, sm_ref, ga_ref, gd_ref):
    x = x_ref[...]
    ms = jnp.mean(x * x, axis=-1, keepdims=True)
    u = (x * lax.rsqrt(ms + EPS) * g_ref[...]).astype(BF16)
    cos = cos_ref[...]
    sin = sin_ref[...]
    lane = lax.broadcasted_iota(jnp.int32, (1, LANES), 1)
    first = (lane % 32) < 16

    def rope(t):
        rot = jnp.where(first, pltpu.roll(t, LANES - 16, 1), pltpu.roll(t, 16, 1))
        return t * cos + rot * sin

    def mm(c0, n):
        return _dot(u, w_ref[:, c0:c0 + n])

    def head(c0, gain):
        t = mm(c0, LANES)
        msq = jnp.sum(t * t, axis=-1, keepdims=True) * (1.0 / HEAD_DIM)
        return rope(t * lax.rsqrt(msq + EPS) * gain)

    for h in range(N_HEADS_A):
        q_ref[:, LANES * h:LANES * (h + 1)] = (head(C_Q + LANES * h, qn_ref[...]) * Q_SCALE).astype(BF16)
    for j in range(N_KV_HEADS):
        k_ref[:, LANES * j:LANES * (j + 1)] = head(C_K + LANES * j, kn_ref[...]).astype(BF16)

    vt = _dot_nt(wvt_ref[...], u)
    vrow = lax.broadcasted_iota(jnp.int32, (N_KV_HEADS * LANES, 1), 0) % LANES
    vt_ref[...] = jnp.where(vrow < HEAD_DIM, vt, 1.0).astype(BF16)
    qkvd_ref[...] = mm(C_QKVD, C_Z - C_QKVD)
    z_ref[...] = mm(C_Z, C_SM - C_Z).astype(BF16)
    sm_ref[...] = mm(C_SM, LANES)
    ga_ref[...] = mm(C_GA, D_MODEL).astype(BF16)
    gd_ref[...] = mm(C_GD, D_MODEL).astype(BF16)


def _inproj(h2, g, w_all, wvt, cos, sin, qn, kn, lp):
    n = h2.shape[0]
    tm = ROW_TILE
    nt = lp // tm
    row = lambda i: (i, 0)
    const = lambda i: (0, 0)
    tab = lambda i: (i % nt, 0)
    out_shapes = (
        jax.ShapeDtypeStruct((n, N_HEADS_A * LANES), BF16),
        jax.ShapeDtypeStruct((n, N_KV_HEADS * LANES), BF16),
        jax.ShapeDtypeStruct((N_KV_HEADS * LANES, n), BF16),
        jax.ShapeDtypeStruct((n, C_Z - C_QKVD), F32),
        jax.ShapeDtypeStruct((n, C_SM - C_Z), BF16),
        jax.ShapeDtypeStruct((n, LANES), F32),
        jax.ShapeDtypeStruct((n, D_MODEL), BF16),
        jax.ShapeDtypeStruct((n, D_MODEL), BF16),
    )
    out_specs = [pl.BlockSpec((tm, s.shape[1]), row) for s in out_shapes]
    out_specs[2] = pl.BlockSpec((N_KV_HEADS * LANES, tm), lambda i: (0, i))
    return pl.pallas_call(
        _inproj_kernel,
        grid=(n // tm,),
        in_specs=[
            pl.BlockSpec((tm, D_MODEL), row),
            pl.BlockSpec((1, D_MODEL), const),
            pl.BlockSpec((D_MODEL, C_END), const, pipeline_mode=pl.Buffered(1)),
            pl.BlockSpec((N_KV_HEADS * LANES, D_MODEL), const, pipeline_mode=pl.Buffered(1)),
            pl.BlockSpec((tm, LANES), tab),
            pl.BlockSpec((tm, LANES), tab),
            pl.BlockSpec((1, LANES), const),
            pl.BlockSpec((1, LANES), const),
        ],
        out_specs=tuple(out_specs),
        out_shape=out_shapes,
        compiler_params=_cparams(("parallel",)),
        name="inproj",
    )(h2, g, w_all, wvt, cos, sin, qn, kn)


def _flash_kernel(q_ref, k_ref, vt_ref, o_ref, qs_ref, m_ref, acc_ref, *, tq, tk, s_real):
    g = N_HEADS_A // N_KV_HEADS
    r = g * tq
    for i in range(g):
        qs_ref[i * tq:(i + 1) * tq, :] = q_ref[0, :, LANES * i:LANES * (i + 1)]
    m_ref[...] = jnp.full(m_ref.shape, -jnp.inf, F32)
    acc_ref[...] = jnp.zeros(acc_ref.shape, F32)

    def chunk(koff, size, mask_null_keys):
        s = _dot_nt(k_ref[0, pl.ds(koff, size), :], qs_ref[...])
        if mask_null_keys:
            row = lax.broadcasted_iota(jnp.int32, (size, 1), 0)
            s = jnp.where(row >= N_NULL, s, -jnp.inf)
        m_prev = m_ref[...]
        m_new = jnp.maximum(m_prev, jnp.max(s, axis=0, keepdims=True))
        alpha = jnp.exp2(m_prev - m_new)
        p = jnp.exp2(s - m_new).astype(BF16)
        acc_ref[...] = alpha * acc_ref[...] + _dot(vt_ref[:, pl.ds(koff, size)], p)
        m_ref[...] = m_new

    chunk(0, FRONT, True)
    n_main = s_real // tk

    def body(c, carry):
        chunk(pl.multiple_of(FRONT + c * tk, LANES), tk, False)
        return carry

    lax.fori_loop(0, n_main, body, 0)
    if s_real % tk:
        chunk(FRONT + n_main * tk, s_real % tk, False)

    acc = acc_ref[...]
    out_t = acc[:HEAD_DIM] / acc[HEAD_DIM:HEAD_DIM + 1]
    for p in range(g // 2):
        pair = jnp.concatenate([out_t[:, (2 * p) * tq:(2 * p + 1) * tq],
                                out_t[:, (2 * p + 1) * tq:(2 * p + 2) * tq]], axis=0)
        o_ref[0, :, 2 * HEAD_DIM * p:2 * HEAD_DIM * (p + 1)] = pair.T.astype(BF16)


def _attention(q, k, vt, s_real):
    b, lp, _ = q.shape
    tq = LANES
    tk = 512
    g = N_HEADS_A // N_KV_HEADS
    return pl.pallas_call(
        functools.partial(_flash_kernel, tq=tq, tk=tk, s_real=s_real),
        grid=(b, N_KV_HEADS, lp // tq),
        in_specs=[
            pl.BlockSpec((1, tq, g * LANES), lambda bi, j, qi: (bi, qi, j)),
            pl.BlockSpec((1, lp, LANES), lambda bi, j, qi: (bi, 0, j)),
            pl.BlockSpec((LANES, lp), lambda bi, j, qi: (j, bi)),
        ],
        out_specs=pl.BlockSpec((1, tq, g * HEAD_DIM), lambda bi, j, qi: (bi, qi, j)),
        out_shape=jax.ShapeDtypeStruct((b, lp, N_HEADS_A * HEAD_DIM), BF16),
        scratch_shapes=[
            pltpu.VMEM((g * tq, LANES), BF16),
            pltpu.VMEM((1, g * tq), F32),
            pltpu.VMEM((LANES, g * tq), F32),
        ],
        compiler_params=_cparams(("parallel", "parallel", "arbitrary")),
        name="attention",
    )(q, k, vt)


def _dprep_kernel(prev_ref, cur_ref, next_ref, sm_ref, cw_ref, rate_ref, dtb_ref, seg_ref,
                  trif_ref, trib_ref, q_ref, k_ref, v_ref, gcf_ref, gcb_ref, bf_ref, bb_ref,
                  *, tm, n_tiles):
    i = pl.program_id(1)
    row = lax.broadcasted_iota(jnp.int32, (tm, 1), 0)
    valid = (row + i * tm) >= N_NULL
    x = jnp.where(valid, cur_ref[0], 0.0)
    prow = jnp.where(i > 0, prev_ref[0, 7:8, :], 0.0)
    nrow = jnp.where(i < n_tiles - 1, next_ref[0, 0:1, :], 0.0)
    xp = jnp.where(row == 0, prow, pltpu.roll(x, 1, 0))
    xn = jnp.where(row == tm - 1, nrow, pltpu.roll(x, tm - 1, 0))
    cw = cw_ref[...]
    y = cw[0:1] * xp + cw[1:2] * x + cw[2:3] * xn
    y = y * jax.nn.sigmoid(y)

    w = N_HEADS_D * DK
    seg = seg_ref[...]
    q = y[:, 0:w]
    q = q * lax.rsqrt(_split_dot(q * q, seg) + EPS) * (DK ** -0.5)
    k = y[:, w:2 * w]
    k = k * lax.rsqrt(_split_dot(k * k, seg) + EPS)
    q_ref[0] = jnp.where(valid, q, 0.0)
    k_ref[0] = jnp.where(valid, k, 0.0)
    v_ref[0] = jnp.where(valid, y[:, 2 * w:3 * w], 0.0)

    sm = sm_ref[0]
    t = sm + dtb_ref[...]
    softplus = jnp.maximum(t, 0.0) + jnp.log1p(jnp.exp(-jnp.abs(t)))
    g_all = jnp.where(valid, -rate_ref[...] * softplus, 0.0)
    beta_all = jnp.where(valid, jax.nn.sigmoid(sm), 0.0)
    gc_f = jnp.dot(trif_ref[...], g_all, preferred_element_type=F32, precision=lax.Precision.HIGHEST)
    gc_b = jnp.dot(trib_ref[...], g_all, preferred_element_type=F32, precision=lax.Precision.HIGHEST)

    lane = lax.broadcasted_iota(jnp.int32, (1, LANES), 1)
    low = lane < DK

    def expand(a, c0, out_ref):
        for p in range(N_HEADS_D // 2):
            e0 = jnp.broadcast_to(a[:, c0 + 2 * p:c0 + 2 * p + 1], (tm, LANES))
            e1 = jnp.broadcast_to(a[:, c0 + 2 * p + 1:c0 + 2 * p + 2], (tm, LANES))
            out_ref[0, :, LANES * p:LANES * (p + 1)] = jnp.where(low, e0, e1)

    expand(gc_f, 0, gcf_ref)
    expand(gc_b, N_HEADS_D, gcb_ref)
    expand(beta_all, 2 * N_HEADS_D, bf_ref)
    expand(beta_all, 3 * N_HEADS_D, bb_ref)


def _delta_prep(qkvd, sm, cw, rate, dtb, seg, trif, trib):
    b, lp, c = qkvd.shape
    tm = ROW_TILE
    nt = lp // tm
    w = N_HEADS_D * DK
    halo = 8
    nb8 = lp // halo
    cur = lambda bi, i: (bi, i, 0)
    const = lambda bi, i: (0, 0)
    out = jax.ShapeDtypeStruct((b, lp, w), F32)
    return pl.pallas_call(
        functools.partial(_dprep_kernel, tm=tm, n_tiles=nt),
        grid=(b, nt),
        in_specs=[
            pl.BlockSpec((1, halo, c), lambda bi, i: (bi, jnp.maximum(i * (tm // halo) - 1, 0), 0)),
            pl.BlockSpec((1, tm, c), cur),
            pl.BlockSpec((1, halo, c), lambda bi, i: (bi, jnp.minimum((i + 1) * (tm // halo), nb8 - 1), 0)),
            pl.BlockSpec((1, tm, LANES), cur),
            pl.BlockSpec((3, c), const),
            pl.BlockSpec((1, LANES), const),
            pl.BlockSpec((1, LANES), const),
            pl.BlockSpec((w, w), const),
            pl.BlockSpec((tm, tm), const),
            pl.BlockSpec((tm, tm), const),
        ],
        out_specs=tuple(pl.BlockSpec((1, tm, w), cur) for _ in range(7)),
        out_shape=(out,) * 7,
        compiler_params=_cparams(("parallel", "parallel")),
        name="delta_prep",
    )(qkvd, qkvd, qkvd, sm, cw, rate, dtb, seg, trif, trib)


def _dscan_kernel(qf_ref, kf_ref, vf_ref, gf_ref, bf_ref, qb_ref, kb_ref, vb_ref, gb_ref, bb_ref,
                  of_ref, ob_ref, s_ref):
    i = pl.program_id(0)

    @pl.when(i == 0)
    def _():
        s_ref[...] = jnp.zeros(s_ref.shape, F32)

    c = CHUNK
    r = lax.broadcasted_iota(jnp.int32, (c, GROUP_W), 0)
    cj = lax.broadcasted_iota(jnp.int32, (c, GROUP_W), 1) % c
    eye = r == cj
    eyef = eye.astype(F32)
    bdmask = (lax.broadcasted_iota(jnp.int32, (GROUP_W, GROUP_W), 0) // c ==
              lax.broadcasted_iota(jnp.int32, (GROUP_W, GROUP_W), 1) // c)

    def bd(t):
        return jnp.where(bdmask, jnp.concatenate([t] * HEAD_GROUP, axis=0), 0.0).astype(BF16)

    def one(q, k, v, gc, beta, sidx, reverse):
        incl = (r <= cj) if reverse else (r >= cj)
        strict = (r < cj) if reverse else (r > cj)
        gcol = jnp.sum(jnp.where(eye, gc, 0.0), axis=0, keepdims=True)
        decay = jnp.where(incl, jnp.exp(jnp.where(incl, gc - gcol, 0.0)), 0.0)
        eg = jnp.exp(gc)
        glast = gc[0:1] if reverse else gc[c - 1:c]
        kbeta = k * beta
        gq = _dot_nt(jnp.concatenate([kbeta, q], axis=0).astype(BF16), bd(k))
        lmat = jnp.where(strict, gq[:c] * decay, 0.0)
        a_intra = gq[c:] * decay
        m = -lmat
        p = eyef + m
        m = _dot(m.astype(BF16), bd(m))
        for _ in range(4):
            rr = _dot(jnp.concatenate([m, p], axis=0).astype(BF16), bd(m))
            p = p + rr[c:]
            m = rr[:c]
        tmat = (p + _dot(p.astype(BF16), bd(m))).astype(BF16)
        u = _dot(tmat, bd(v * beta))
        w = _dot(tmat, bd(kbeta * eg))
        state = s_ref[sidx]
        wq = _dot(jnp.concatenate([w, q * eg], axis=0).astype(BF16), state.astype(BF16))
        v_new = u - wq[:c]
        o = wq[c:] + _dot(a_intra.astype(BF16), bd(v_new))
        kdec = k * jnp.exp(glast - gc)
        upd = _dot(kdec.T.astype(BF16), v_new.astype(BF16))
        s_ref[sidx] = state * jnp.exp(glast) + jnp.where(bdmask, upd, 0.0)
        return o

    n_grp = N_HEADS_D // HEAD_GROUP
    for bi in range(of_ref.shape[0]):
        for grp in range(n_grp):
            sl = slice(GROUP_W * grp, GROUP_W * (grp + 1))
            sidx = (bi * 2) * n_grp + grp
            of_ref[bi, :, sl] = one(qf_ref[bi, :, sl], kf_ref[bi, :, sl], vf_ref[bi, :, sl],
                                    gf_ref[bi, :, sl], bf_ref[bi, :, sl], sidx, False)
            ob_ref[bi, :, sl] = one(qb_ref[bi, :, sl], kb_ref[bi, :, sl], vb_ref[bi, :, sl],
                                    gb_ref[bi, :, sl], bb_ref[bi, :, sl], sidx + n_grp, True)


def _delta_scan(q, k, v, gcf, gcb, bf, bb):
    b, lp, w = q.shape
    n = lp // CHUNK
    fwd = lambda i: (0, i, 0)
    bwd = lambda i: (0, n - 1 - i, 0)
    blk = (b, CHUNK, w)
    out = jax.ShapeDtypeStruct((b, lp, w), F32)
    return pl.pallas_call(
        _dscan_kernel,
        grid=(n,),
        in_specs=[pl.BlockSpec(blk, fwd)] * 5 + [pl.BlockSpec(blk, bwd)] * 5,
        out_specs=(pl.BlockSpec(blk, fwd), pl.BlockSpec(blk, bwd)),
        out_shape=(out, out),
        scratch_shapes=[pltpu.VMEM((b * 2 * N_HEADS_D // HEAD_GROUP, GROUP_W, GROUP_W), F32)],
        compiler_params=_cparams(("arbitrary",)),
        name="delta_scan",
    )(q, k, v, gcf, bf, q, k, v, gcb, bb)


def _post_kernel(h_ref, at_ref, of_ref, ob_ref, z_ref, ga_ref, gd_ref, on_ref, seg_ref,
                 wa_ref, wd_ref, wo_ref, gf_ref, wrh_ref, wrl_ref,
                 h_out, xn_out, aff_out, *, tm, n_tiles):
    od = of_ref[...] + ob_ref[...]
    msd = _split_dot(od * od, seg_ref[...]) * (1.0 / DV)
    z = z_ref[...].astype(F32)
    od = od * lax.rsqrt(msd + EPS) * on_ref[...] * (z * jax.nn.sigmoid(z))
    y_d = _dot(od.astype(BF16), wd_ref[...])
    y_a = _dot(at_ref[...], wa_ref[...])
    merged = (jax.nn.sigmoid(ga_ref[...].astype(F32)) * y_a +
              jax.nn.sigmoid(gd_ref[...].astype(F32)) * y_d)
    h = h_ref[...] + _dot(merged.astype(BF16), wo_ref[...])
    h_out[...] = h

    ms = jnp.mean(h * h, axis=-1, keepdims=True)
    xn = h * lax.rsqrt(ms + EPS) * gf_ref[...]
    xh = xn.astype(BF16)
    xn_out[...] = xh
    xl = (xn - xh.astype(F32)).astype(BF16)
    logits = _dot(xh, wrh_ref[...]) + _dot(xl, wrh_ref[...]) + _dot(xh, wrl_ref[...])
    lane = lax.broadcasted_iota(jnp.int32, (1, LANES), 1)
    logits = jnp.where(lane < N_EXPERTS, logits, -jnp.inf)
    e = jnp.exp(logits - jnp.max(logits, axis=-1, keepdims=True))
    aff = e / jnp.sum(e, axis=-1, keepdims=True)
    row = lax.broadcasted_iota(jnp.int32, (tm, 1), 0) + (pl.program_id(0) % n_tiles) * tm
    aff_out[...] = jnp.where(row >= N_NULL, aff, -1.0)


def _post(h2, at, of, ob, z, ga, gd, on, seg, wa, wd, wo, gf, wrh, wrl, lp):
    n = h2.shape[0]
    tm = ROW_TILE
    nt = lp // tm
    row = lambda i: (i, 0)
    const = lambda i: (0, 0)
    hw = N_HEADS_D * DV

    def wspec(shape):
        return pl.BlockSpec(shape, const, pipeline_mode=pl.Buffered(1))

    return pl.pallas_call(
        functools.partial(_post_kernel, tm=tm, n_tiles=nt),
        grid=(n // tm,),
        in_specs=[
            pl.BlockSpec((tm, D_MODEL), row),
            pl.BlockSpec((tm, hw), row),
            pl.BlockSpec((tm, hw), row),
            pl.BlockSpec((tm, hw), row),
            pl.BlockSpec((tm, hw), row),
            pl.BlockSpec((tm, D_MODEL), row),
            pl.BlockSpec((tm, D_MODEL), row),
            pl.BlockSpec((1, hw), const),
            wspec((hw, hw)),
            wspec((hw, D_MODEL)),
            wspec((hw, D_MODEL)),
            wspec((D_MODEL, D_MODEL)),
            pl.BlockSpec((1, D_MODEL), const),
            wspec((D_MODEL, LANES)),
            wspec((D_MODEL, LANES)),
        ],
        out_specs=(pl.BlockSpec((tm, D_MODEL), row), pl.BlockSpec((tm, D_MODEL), row),
                   pl.BlockSpec((tm, LANES), row)),
        out_shape=(jax.ShapeDtypeStruct((n, D_MODEL), F32), jax.ShapeDtypeStruct((n, D_MODEL), BF16),
                   jax.ShapeDtypeStruct((n, LANES), F32)),
        compiler_params=_cparams(("parallel",)),
        name="merge_out",
    )(h2, at, of, ob, z, ga, gd, on, seg, wa, wd, wo, gf, wrh, wrl)


def _expert_kernel(x_ref, gate_ref, wg_ref, wu_ref, wd_ref, y_ref):
    x = x_ref[0]
    a = _dot(x, wg_ref[0])
    b = _dot(x, wu_ref[0])
    hh = (a * jax.nn.sigmoid(a) * b).astype(BF16)
    y_ref[0] = _dot(hh, wd_ref[0]) * gate_ref[0]


def _experts(xe, gates, wg, wu, wd, tm):
    e, cp, d = xe.shape
    tile = lambda ei, i: (ei, i, 0)
    wsp = lambda ei, i: (ei, 0, 0)
    return pl.pallas_call(
        _expert_kernel,
        grid=(e, cp // tm),
        in_specs=[
            pl.BlockSpec((1, tm, d), tile),
            pl.BlockSpec((1, tm, 1), tile),
            pl.BlockSpec((1, d, D_EXPERT), wsp),
            pl.BlockSpec((1, d, D_EXPERT), wsp),
            pl.BlockSpec((1, D_EXPERT, d), wsp),
        ],
        out_specs=pl.BlockSpec((1, tm, d), tile),
        out_shape=jax.ShapeDtypeStruct((e, cp, d), F32),
        compiler_params=_cparams(("parallel", "arbitrary")),
        name="experts",
    )(xe, gates, wg, wu, wd)


def _final_kernel(x_ref, g_ref, o_ref):
    x = x_ref[0]
    ms = jnp.mean(x * x, axis=-1, keepdims=True)
    o_ref[0] = x * lax.rsqrt(ms + EPS) * g_ref[...]


def _final_norm(h3, g, s):
    b = h3.shape[0]
    tm = FRONT
    return pl.pallas_call(
        _final_kernel,
        grid=(b, s // tm),
        in_specs=[pl.BlockSpec((1, tm, D_MODEL), lambda bi, i: (bi, i + FRONT // tm, 0)),
                  pl.BlockSpec((1, D_MODEL), lambda bi, i: (0, 0))],
        out_specs=pl.BlockSpec((1, tm, D_MODEL), lambda bi, i: (bi, i, 0)),
        out_shape=jax.ShapeDtypeStruct((b, s, D_MODEL), F32),
        compiler_params=_cparams(("parallel", "parallel")),
        name="final_norm",
    )(h3, g)


def _rope_tables(s):
    lp = FRONT + s
    t = np.arange(lp) - FRONT
    real = t >= 0
    pos = np.stack([np.where(real, t // GRID_W, 0), np.where(real, t % GRID_W, 0)], axis=-1)
    n_freq = HEAD_DIM // 4
    inv_freq = jnp.asarray(ROPE_THETA, F32) ** (-jnp.arange(n_freq, dtype=F32) / n_freq)
    lane = np.arange(LANES) % HEAD_DIM
    axis = lane // (HEAD_DIM // 2)
    freq = lane % n_freq
    ang = jnp.asarray(pos, F32)[:, axis] * inv_freq[freq][None, :]
    sign = np.where((lane % (HEAD_DIM // 2)) < n_freq, -1.0, 1.0).astype(np.float32)
    return jnp.cos(ang), jnp.sin(ang) * sign[None, :]


def _pack_w_in(w):
    o = 0

    def take(n):
        nonlocal o
        part = w[:, o:o + n]
        o += n
        return part

    def head_blocks(part, n_heads):
        part = part.reshape(D_MODEL, n_heads, HEAD_DIM)
        part = jnp.pad(part, ((0, 0), (0, 0), (0, LANES - HEAD_DIM)))
        return part.reshape(D_MODEL, n_heads * LANES)

    q_a = head_blocks(take(N_HEADS_A * HEAD_DIM), N_HEADS_A)
    k_a = head_blocks(take(N_KV_HEADS * HEAD_DIM), N_KV_HEADS)
    v_a = head_blocks(take(N_KV_HEADS * HEAD_DIM), N_KV_HEADS)
    qkv_d = take(3 * N_HEADS_D * DK)
    z_d = take(N_HEADS_D * DV)
    small = take(4 * N_HEADS_D)
    gate_a = take(D_MODEL)
    gate_d = take(D_MODEL)
    small = jnp.pad(small, ((0, 0), (0, LANES - small.shape[1])))
    w_all = jnp.concatenate([q_a, k_a, qkv_d, z_d, small, gate_a, gate_d], axis=1).astype(BF16)
    return w_all, v_a.T.astype(BF16)


def _chunk_tri(tm, reverse):
    i = np.arange(tm)
    same = (i[:, None] // CHUNK) == (i[None, :] // CHUNK)
    tri = (i[None, :] >= i[:, None]) if reverse else (i[None, :] <= i[:, None])
    return jnp.asarray((same & tri).astype(np.float32))


def _expert_tile(cap):
    n_tiles = -(-cap // 512)
    tm = -(-cap // n_tiles)
    tm = -(-tm // 16) * 16
    return tm, n_tiles


def _trunk(x, meta_tokens, layers, norm_final):
    b, s, d = x.shape
    lp = FRONT + s
    n = b * lp
    n_tok = b * (N_META + s)
    cap = EC_CAPACITY * n_tok // N_EXPERTS
    etm, ent = _expert_tile(cap)
    cap_pad = etm * ent
    cos, sin = _rope_tables(s)
    seg = jnp.asarray(np.kron(np.eye(N_HEADS_D), np.ones((DK, DK))), BF16)
    trif = _chunk_tri(ROW_TILE, False)
    trib = _chunk_tri(ROW_TILE, True)

    front = jnp.concatenate([jnp.zeros((N_NULL, d), F32), meta_tokens.astype(F32)], axis=0)
    h = jnp.concatenate([jnp.broadcast_to(front[None], (b, FRONT, d)), x], axis=1).reshape(n, d)

    for lw in layers:
        q, k, vt, qkvd, z, sm, ga, gd = _inproj(h, lw["norm_mix"], lw["w_all"], lw["wvt"], cos, sin,
                                                lw["qn"], lw["kn"], lp)
        at = _attention(q.reshape(b, lp, -1), k.reshape(b, lp, -1), vt, s)
        qd, kd, vd, gcf, gcb, bf, bb = _delta_prep(qkvd.reshape(b, lp, -1), sm.reshape(b, lp, -1),
                                                   lw["conv_w"], lw["rate"], lw["dtb"], seg, trif, trib)
        of, ob = _delta_scan(qd, kd, vd, gcf, gcb, bf, bb)
        hw = N_HEADS_D * DV
        h, xn, aff = _post(h, at.reshape(n, hw), of.reshape(n, hw), ob.reshape(n, hw), z, ga, gd,
                           lw["on"], seg, lw["w_attn_proj"], lw["w_delta_proj"], lw["w_out"],
                           lw["norm_ffn"], lw["wr_hi"], lw["wr_lo"], lp)
        gates, idx = lax.top_k(aff[:, :N_EXPERTS].T, cap)
        idx = jnp.pad(idx, ((0, 0), (0, cap_pad - cap)))
        gates = jnp.pad(gates, ((0, 0), (0, cap_pad - cap)))
        xe = jnp.take(xn, idx.reshape(-1), axis=0).reshape(N_EXPERTS, cap_pad, d)
        ye = _experts(xe, gates[..., None], lw["w_gate"], lw["w_up"], lw["w_down"], etm)
        h = h.at[idx.reshape(-1)].add(ye.reshape(-1, d))

    return _final_norm(h.reshape(b, lp, d), norm_final, s)


def kernel(x_prompt, x_sample, meta_tokens, norm_mix, w_in, q_norm, k_norm, conv_w, a_log, dt_bias,
           o_norm, w_attn_proj, w_delta_proj, w_out, norm_ffn, w_router, w_gate, w_up, w_down,
           norm_final):
    depth = w_in.shape[0]
    layers = []
    for l in range(depth):
        rate = jnp.exp(a_log[l].astype(F32)).reshape(1, -1)
        dtb = dt_bias[l].astype(F32).reshape(1, -1)
        pad = LANES - rate.shape[1]
        wr = jnp.pad(w_router[l].astype(F32), ((0, 0), (0, LANES - N_EXPERTS)))
        wr_hi = wr.astype(BF16)
        w_all, wvt = _pack_w_in(w_in[l])
        layers.append(dict(
            norm_mix=norm_mix[l].astype(F32).reshape(1, -1),
            w_all=w_all,
            wvt=wvt,
            qn=jnp.tile(q_norm[l].astype(F32), LANES // HEAD_DIM).reshape(1, -1),
            kn=jnp.tile(k_norm[l].astype(F32), LANES // HEAD_DIM).reshape(1, -1),
            conv_w=conv_w[l].astype(F32),
            rate=jnp.pad(rate, ((0, 0), (0, pad))),
            dtb=jnp.pad(dtb, ((0, 0), (0, pad))),
            on=jnp.tile(o_norm[l].astype(F32), N_HEADS_D).reshape(1, -1),
            w_attn_proj=w_attn_proj[l].astype(BF16),
            w_delta_proj=w_delta_proj[l].astype(BF16),
            w_out=w_out[l].astype(BF16),
            norm_ffn=norm_ffn[l].astype(F32).reshape(1, -1),
            wr_hi=wr_hi,
            wr_lo=(wr - wr_hi.astype(F32)).astype(BF16),
            w_gate=w_gate[l].astype(BF16),
            w_up=w_up[l].astype(BF16),
            w_down=w_down[l].astype(BF16),
        ))
    nf = norm_final.astype(F32).reshape(1, -1)
    y_prompt = _trunk(x_prompt, meta_tokens, layers, nf)
    y_sample = _trunk(x_sample, meta_tokens, layers, nf)
    return (y_prompt, y_sample)
```

```python
import functools

import numpy as np
import jax
import jax.numpy as jnp
from jax import lax
from jax.experimental import pallas as pl
from jax.experimental.pallas import tpu as pltpu

F32 = jnp.float32
BF16 = jnp.bfloat16

D_MODEL = 1024
N_META = 16
GRID_W = 64
N_HEADS_A = 8
N_KV_HEADS = 2
HEAD_DIM = 64
ROPE_THETA = 10000.0
N_HEADS_D = 8
DK = 64
DV = 64
CHUNK = 64
N_EXPERTS = 16
EC_CAPACITY = 2
D_EXPERT = 1024
EPS = 1e-6

FRONT = 128
N_NULL = FRONT - N_META
LANES = 128
ROW_TILE = 384
HEAD_GROUP = 4
GROUP_W = HEAD_GROUP * DK
VMEM_LIMIT = 48 * 1024 * 1024
Q_SCALE = HEAD_DIM ** -0.5 * float(np.log2(np.e))

C_Q = 0
C_K = C_Q + N_HEADS_A * LANES
C_QKVD = C_K + N_KV_HEADS * LANES
C_Z = C_QKVD + 3 * N_HEADS_D * DK
C_SM = C_Z + N_HEADS_D * DV
C_GA = C_SM + LANES
C_GD = C_GA + D_MODEL
C_END = C_GD + D_MODEL


def _cparams(sem):
    return pltpu.CompilerParams(dimension_semantics=sem, vmem_limit_bytes=VMEM_LIMIT)


def _dot(a, b):
    return jnp.dot(a, b, preferred_element_type=F32)


def _dot_nt(a, b):
    return lax.dot_general(a, b, (((1,), (1,)), ((), ())), preferred_element_type=F32)


def _split_dot(t, w_bf16):
    hi = t.astype(BF16)
    lo = (t - hi.astype(F32)).astype(BF16)
    return _dot(hi, w_bf16) + _dot(lo, w_bf16)


def _inproj_kernel(x_ref, g_ref, w_ref, wvt_ref, cos_ref, sin_ref, qn_ref, kn_ref,
                   q_ref, k_ref, vt_ref, qkvd_ref, z_ref, sm_ref, ga_ref, gd_ref):
    x = x_ref[...]
    ms = jnp.mean(x * x, axis=-1, keepdims=True)
    u = (x * lax.rsqrt(ms + EPS) * g_ref[...]).astype(BF16)
    cos = cos_ref[...]
    sin = sin_ref[...]
    lane = lax.broadcasted_iota(jnp.int32, (1, LANES), 1)
    first = (lane % 32) < 16

    def rope(t):
        rot = jnp.where(first, pltpu.roll(t, LANES - 16, 1), pltpu.roll(t, 16, 1))
        return t * cos + rot * sin

    def mm(c0, n):
        return _dot(u, w_ref[:, c0:c0 + n])

    def head(c0, gain):
        t = mm(c0, LANES)
        msq = jnp.sum(t * t, axis=-1, keepdims=True) * (1.0 / HEAD_DIM)
        return rope(t * lax.rsqrt(msq + EPS) * gain)

    for h in range(N_HEADS_A):
        q_ref[:, LANES * h:LANES * (h + 1)] = (head(C_Q + LANES * h, qn_ref[...]) * Q_SCALE).astype(BF16)
    for j in range(N_KV_HEADS):
        k_ref[:, LANES * j:LANES * (j + 1)] = head(C_K + LANES * j, kn_ref[...]).astype(BF16)

    vt = _dot_nt(wvt_ref[...], u)
    vrow = lax.broadcasted_iota(jnp.int32, (N_KV_HEADS * LANES, 1), 0) % LANES
    vt_ref[...] = jnp.where(vrow < HEAD_DIM, vt, 1.0).astype(BF16)
    qkvd_ref[...] = mm(C_QKVD, C_Z - C_QKVD)
    z_ref[...] = mm(C_Z, C_SM - C_Z).astype(BF16)
    sm_ref[...] = mm(C_SM, LANES)
    ga_ref[...] = mm(C_GA, D_MODEL).astype(BF16)
    gd_ref[...] = mm(C_GD, D_MODEL).astype(BF16)


def _inproj(h2, g, w_all, wvt, cos, sin, qn, kn, lp):
    n = h2.shape[0]
    tm = ROW_TILE
    nt = lp // tm
    row = lambda i: (i, 0)
    const = lambda i: (0, 0)
    tab = lambda i: (i % nt, 0)
    out_shapes = (
        jax.ShapeDtypeStruct((n, N_HEADS_A * LANES), BF16),
        jax.ShapeDtypeStruct((n, N_KV_HEADS * LANES), BF16),
        jax.ShapeDtypeStruct((N_KV_HEADS * LANES, n), BF16),
        jax.ShapeDtypeStruct((n, C_Z - C_QKVD), F32),
        jax.ShapeDtypeStruct((n, C_SM - C_Z), BF16),
        jax.ShapeDtypeStruct((n, LANES), F32),
        jax.ShapeDtypeStruct((n, D_MODEL), BF16),
        jax.ShapeDtypeStruct((n, D_MODEL), BF16),
    )
    out_specs = [pl.BlockSpec((tm, s.shape[1]), row) for s in out_shapes]
    out_specs[2] = pl.BlockSpec((N_KV_HEADS * LANES, tm), lambda i: (0, i))
    return pl.pallas_call(
        _inproj_kernel,
        grid=(n // tm,),
        in_specs=[
            pl.BlockSpec((tm, D_MODEL), row),
            pl.BlockSpec((1, D_MODEL), const),
            pl.BlockSpec((D_MODEL, C_END), const, pipeline_mode=pl.Buffered(1)),
            pl.BlockSpec((N_KV_HEADS * LANES, D_MODEL), const, pipeline_mode=pl.Buffered(1)),
            pl.BlockSpec((tm, LANES), tab),
            pl.BlockSpec((tm, LANES), tab),
            pl.BlockSpec((1, LANES), const),
            pl.BlockSpec((1, LANES), const),
        ],
        out_specs=tuple(out_specs),
        out_shape=out_shapes,
        compiler_params=_cparams(("parallel",)),
        name="inproj",
    )(h2, g, w_all, wvt, cos, sin, qn, kn)


def _flash_kernel(q_ref, k_ref, vt_ref, o_ref, qs_ref, m_ref, acc_ref, sa_ref, sb_ref, ca_ref, cb_ref,
                  *, tq, tk, s_real):
    g = N_HEADS_A // N_KV_HEADS
    r = g * tq
    for i in range(g):
        qs_ref[i * tq:(i + 1) * tq, :] = q_ref[0, :, LANES * i:LANES * (i + 1)]
    m_ref[...] = jnp.full(m_ref.shape, -jnp.inf, F32)
    acc_ref[...] = jnp.zeros(acc_ref.shape, F32)

    def scores(koff, size):
        return _dot_nt(k_ref[0, pl.ds(koff, size), :], qs_ref[...])

    def absorb(s, cmax, koff, size):
        m_prev = m_ref[...]
        m_new = jnp.maximum(m_prev, cmax)
        alpha = jnp.exp2(m_prev - m_new)
        p = jnp.exp2(s - m_new).astype(BF16)
        acc_ref[...] = alpha * acc_ref[...] + _dot(vt_ref[:, pl.ds(koff, size)], p)
        m_ref[...] = m_new

    def produce(koff, s_ref, c_ref):
        s = scores(koff, tk)
        s_ref[...] = s
        c_ref[...] = jnp.max(s, axis=0, keepdims=True)

    def main_off(c):
        return pl.multiple_of(FRONT + c * tk, LANES)

    s = scores(0, FRONT)
    s = jnp.where(lax.broadcasted_iota(jnp.int32, (FRONT, 1), 0) >= N_NULL, s, -jnp.inf)
    absorb(s, jnp.max(s, axis=0, keepdims=True), 0, FRONT)

    n_main = s_real // tk
    if n_main:
        produce(main_off(0), sa_ref, ca_ref)
        n_pairs = (n_main - 1) // 2

        def body(pi, carry):
            produce(main_off(2 * pi + 1), sb_ref, cb_ref)
            absorb(sa_ref[...], ca_ref[...], main_off(2 * pi), tk)
            produce(main_off(2 * pi + 2), sa_ref, ca_ref)
            absorb(sb_ref[...], cb_ref[...], main_off(2 * pi + 1), tk)
            return carry

        lax.fori_loop(0, n_pairs, body, 0)
        if n_main - 2 * n_pairs == 2:
            produce(main_off(2 * n_pairs + 1), sb_ref, cb_ref)
            absorb(sa_ref[...], ca_ref[...], main_off(2 * n_pairs), tk)
            absorb(sb_ref[...], cb_ref[...], main_off(2 * n_pairs + 1), tk)
        else:
            absorb(sa_ref[...], ca_ref[...], main_off(2 * n_pairs), tk)
    if s_real % tk:
        s = scores(FRONT + n_main * tk, s_real % tk)
        absorb(s, jnp.max(s, axis=0, keepdims=True), FRONT + n_main * tk, s_real % tk)

    acc = acc_ref[...]
    out_t = acc[:HEAD_DIM] / acc[HEAD_DIM:HEAD_DIM + 1]
    for p in range(g // 2):
        pair = jnp.concatenate([out_t[:, (2 * p) * tq:(2 * p + 1) * tq],
                                out_t[:, (2 * p + 1) * tq:(2 * p + 2) * tq]], axis=0)
        o_ref[0, :, 2 * HEAD_DIM * p:2 * HEAD_DIM * (p + 1)] = pair.T.astype(BF16)


def _attention(q, k, vt, s_real):
    b, lp, _ = q.shape
    tq = LANES
    tk = 512
    g = N_HEADS_A // N_KV_HEADS
    return pl.pallas_call(
        functools.partial(_flash_kernel, tq=tq, tk=tk, s_real=s_real),
        grid=(b, N_KV_HEADS, lp // tq),
        in_specs=[
            pl.BlockSpec((1, tq, g * LANES), lambda bi, j, qi: (bi, qi, j)),
            pl.BlockSpec((1, lp, LANES), lambda bi, j, qi: (bi, 0, j)),
            pl.BlockSpec((LANES, lp), lambda bi, j, qi: (j, bi)),
        ],
        out_specs=pl.BlockSpec((1, tq, g * HEAD_DIM), lambda bi, j, qi: (bi, qi, j)),
        out_shape=jax.ShapeDtypeStruct((b, lp, N_HEADS_A * HEAD_DIM), BF16),
        scratch_shapes=[
            pltpu.VMEM((g * tq, LANES), BF16),
            pltpu.VMEM((1, g * tq), F32),
            pltpu.VMEM((LANES, g * tq), F32),
            pltpu.VMEM((tk, g * tq), F32),
            pltpu.VMEM((tk, g * tq), F32),
            pltpu.VMEM((1, g * tq), F32),
            pltpu.VMEM((1, g * tq), F32),
        ],
        compiler_params=_cparams(("parallel", "parallel", "arbitrary")),
        name="attention",
    )(q, k, vt)


def _dprep_kernel(prev_ref, cur_ref, next_ref, sm_ref, cw_ref, rate_ref, dtb_ref, seg_ref,
                  trif_ref, trib_ref, q_ref, k_ref, v_ref, gcf_ref, gcb_ref, bf_ref, bb_ref,
                  *, tm, n_tiles):
    i = pl.program_id(1)
    row = lax.broadcasted_iota(jnp.int32, (tm, 1), 0)
    valid = (row + i * tm) >= N_NULL
    x = jnp.where(valid, cur_ref[0], 0.0)
    prow = jnp.where(i > 0, prev_ref[0, 7:8, :], 0.0)
    nrow = jnp.where(i < n_tiles - 1, next_ref[0, 0:1, :], 0.0)
    xp = jnp.where(row == 0, prow, pltpu.roll(x, 1, 0))
    xn = jnp.where(row == tm - 1, nrow, pltpu.roll(x, tm - 1, 0))
    cw = cw_ref[...]
    y = cw[0:1] * xp + cw[1:2] * x + cw[2:3] * xn
    y = y * jax.nn.sigmoid(y)

    w = N_HEADS_D * DK
    seg = seg_ref[...]
    q = y[:, 0:w]
    q = q * lax.rsqrt(_split_dot(q * q, seg) + EPS) * (DK ** -0.5)
    k = y[:, w:2 * w]
    k = k * lax.rsqrt(_split_dot(k * k, seg) + EPS)
    q_ref[0] = jnp.where(valid, q, 0.0)
    k_ref[0] = jnp.where(valid, k, 0.0)
    v_ref[0] = jnp.where(valid, y[:, 2 * w:3 * w], 0.0)

    sm = sm_ref[0]
    t = sm + dtb_ref[...]
    softplus = jnp.maximum(t, 0.0) + jnp.log1p(jnp.exp(-jnp.abs(t)))
    g_all = jnp.where(valid, -rate_ref[...] * softplus, 0.0)
    beta_all = jnp.where(valid, jax.nn.sigmoid(sm), 0.0)
    gc_f = jnp.dot(trif_ref[...], g_all, preferred_element_type=F32, precision=lax.Precision.HIGHEST)
    gc_b = jnp.dot(trib_ref[...], g_all, preferred_element_type=F32, precision=lax.Precision.HIGHEST)

    lane = lax.broadcasted_iota(jnp.int32, (1, LANES), 1)
    low = lane < DK

    def expand(a, c0, out_ref):
        for p in range(N_HEADS_D // 2):
            e0 = jnp.broadcast_to(a[:, c0 + 2 * p:c0 + 2 * p + 1], (tm, LANES))
            e1 = jnp.broadcast_to(a[:, c0 + 2 * p + 1:c0 + 2 * p + 2], (tm, LANES))
            out_ref[0, :, LANES * p:LANES * (p + 1)] = jnp.where(low, e0, e1)

    expand(gc_f, 0, gcf_ref)
    expand(gc_b, N_HEADS_D, gcb_ref)
    expand(beta_all, 2 * N_HEADS_D, bf_ref)
    expand(beta_all, 3 * N_HEADS_D, bb_ref)


def _delta_prep(qkvd, sm, cw, rate, dtb, seg, trif, trib):
    b, lp, c = qkvd.shape
    tm = ROW_TILE
    nt = lp // tm
    w = N_HEADS_D * DK
    halo = 8
    nb8 = lp // halo
    cur = lambda bi, i: (bi, i, 0)
    const = lambda bi, i: (0, 0)
    out = jax.ShapeDtypeStruct((b, lp, w), F32)
    return pl.pallas_call(
        functools.partial(_dprep_kernel, tm=tm, n_tiles=nt),
        grid=(b, nt),
        in_specs=[
            pl.BlockSpec((1, halo, c), lambda bi, i: (bi, jnp.maximum(i * (tm // halo) - 1, 0), 0)),
            pl.BlockSpec((1, tm, c), cur),
            pl.BlockSpec((1, halo, c), lambda bi, i: (bi, jnp.minimum((i + 1) * (tm // halo), nb8 - 1), 0)),
            pl.BlockSpec((1, tm, LANES), cur),
            pl.BlockSpec((3, c), const),
            pl.BlockSpec((1, LANES), const),
            pl.BlockSpec((1, LANES), const),
            pl.BlockSpec((w, w), const),
            pl.BlockSpec((tm, tm), const),
            pl.BlockSpec((tm, tm), const),
        ],
        out_specs=tuple(pl.BlockSpec((1, tm, w), cur) for _ in range(7)),
        out_shape=(out,) * 7,
        compiler_params=_cparams(("parallel", "parallel")),
        name="delta_prep",
    )(qkvd, qkvd, qkvd, sm, cw, rate, dtb, seg, trif, trib)


def _dscan_kernel(qf_ref, kf_ref, vf_ref, gf_ref, bf_ref, qb_ref, kb_ref, vb_ref, gb_ref, bb_ref,
                  of_ref, ob_ref, s_ref):
    i = pl.program_id(0)

    @pl.when(i == 0)
    def _():
        s_ref[...] = jnp.zeros(s_ref.shape, F32)

    c = CHUNK
    r = lax.broadcasted_iota(jnp.int32, (c, GROUP_W), 0)
    cj = lax.broadcasted_iota(jnp.int32, (c, GROUP_W), 1) % c
    eye = r == cj
    eyef = eye.astype(F32)
    bdmask = (lax.broadcasted_iota(jnp.int32, (GROUP_W, GROUP_W), 0) // c ==
              lax.broadcasted_iota(jnp.int32, (GROUP_W, GROUP_W), 1) // c)

    def bd(t):
        return jnp.where(bdmask, jnp.concatenate([t] * HEAD_GROUP, axis=0), 0.0).astype(BF16)

    n_grp = N_HEADS_D // HEAD_GROUP
    fwd_refs = (qf_ref, kf_ref, vf_ref, gf_ref, bf_ref, of_ref)
    bwd_refs = (qb_ref, kb_ref, vb_ref, gb_ref, bb_ref, ob_ref)
    chains = []
    for bi in range(of_ref.shape[0]):
        for grp in range(n_grp):
            for reverse in (False, True):
                sl = slice(GROUP_W * grp, GROUP_W * (grp + 1))
                chains.append((fwd_refs if not reverse else bwd_refs, bi, sl, reverse, len(chains)))

    def each(fn, *cols):
        return [fn(*args) for args in zip(*cols)]

    rev = [ch[3] for ch in chains]
    q = [ch[0][0][ch[1], :, ch[2]] for ch in chains]
    k = [ch[0][1][ch[1], :, ch[2]] for ch in chains]
    v = [ch[0][2][ch[1], :, ch[2]] for ch in chains]
    gc = [ch[0][3][ch[1], :, ch[2]] for ch in chains]
    beta = [ch[0][4][ch[1], :, ch[2]] for ch in chains]
    state = [s_ref[ch[4]] for ch in chains]

    def decay_of(g, reverse):
        incl = (r <= cj) if reverse else (r >= cj)
        gcol = jnp.sum(jnp.where(eye, g, 0.0), axis=0, keepdims=True)
        return jnp.where(incl, jnp.exp(jnp.where(incl, g - gcol, 0.0)), 0.0)

    decay = each(decay_of, gc, rev)
    eg = each(jnp.exp, gc)
    glast = each(lambda g, reverse: g[0:1] if reverse else g[c - 1:c], gc, rev)
    kbeta = each(lambda a, b: a * b, k, beta)
    gq = each(lambda kb_, q_, k_: _dot_nt(jnp.concatenate([kb_, q_], axis=0).astype(BF16), bd(k_)),
              kbeta, q, k)
    a_intra = each(lambda g, d: g[c:] * d, gq, decay)
    m = each(lambda g, d, reverse: -jnp.where((r < cj) if reverse else (r > cj), g[:c] * d, 0.0),
             gq, decay, rev)
    p = each(lambda m_: eyef + m_, m)
    m = each(lambda m_: _dot(m_.astype(BF16), bd(m_)), m)
    for _ in range(4):
        rr = each(lambda m_, p_: _dot(jnp.concatenate([m_, p_], axis=0).astype(BF16), bd(m_)), m, p)
        p = each(lambda p_, rr_: p_ + rr_[c:], p, rr)
        m = each(lambda rr_: rr_[:c], rr)
    tmat = each(lambda p_, m_: (p_ + _dot(p_.astype(BF16), bd(m_))).astype(BF16), p, m)
    u = each(lambda t, v_, b: _dot(t, bd(v_ * b)), tmat, v, beta)
    w = each(lambda t, kb_, e: _dot(t, bd(kb_ * e)), tmat, kbeta, eg)
    wq = each(lambda w_, q_, e, s: _dot(jnp.concatenate([w_, q_ * e], axis=0).astype(BF16), s.astype(BF16)),
              w, q, eg, state)
    v_new = each(lambda u_, wq_: u_ - wq_[:c], u, wq)
    o = each(lambda wq_, a, vn: wq_[c:] + _dot(a.astype(BF16), bd(vn)), wq, a_intra, v_new)
    upd = each(lambda k_, gl, g, vn: _dot((k_ * jnp.exp(gl - g)).T.astype(BF16), vn.astype(BF16)),
               k, glast, gc, v_new)
    for ch, o_, s, gl, up in zip(chains, o, state, glast, upd):
        ch[0][5][ch[1], :, ch[2]] = o_
        s_ref[ch[4]] = s * jnp.exp(gl) + jnp.where(bdmask, up, 0.0)


def _delta_scan(q, k, v, gcf, gcb, bf, bb):
    b, lp, w = q.shape
    n = lp // CHUNK
    fwd = lambda i: (0, i, 0)
    bwd = lambda i: (0, n - 1 - i, 0)
    blk = (b, CHUNK, w)
    out = jax.ShapeDtypeStruct((b, lp, w), F32)
    return pl.pallas_call(
        _dscan_kernel,
        grid=(n,),
        in_specs=[pl.BlockSpec(blk, fwd)] * 5 + [pl.BlockSpec(blk, bwd)] * 5,
        out_specs=(pl.BlockSpec(blk, fwd), pl.BlockSpec(blk, bwd)),
        out_shape=(out, out),
        scratch_shapes=[pltpu.VMEM((b * 2 * N_HEADS_D // HEAD_GROUP, GROUP_W, GROUP_W), F32)],
        compiler_params=_cparams(("arbitrary",)),
        name="delta_scan",
    )(q, k, v, gcf, bf, q, k, v, gcb, bb)


def _post_kernel(h_ref, at_ref, of_ref, ob_ref, z_ref, ga_ref, gd_ref, on_ref, seg_ref,
                 wa_ref, wd_ref, wo_ref, gf_ref, wrh_ref, wrl_ref,
                 h_out, xn_out, aff_out, *, tm, n_tiles):
    od = of_ref[...] + ob_ref[...]
    msd = _split_dot(od * od, seg_ref[...]) * (1.0 / DV)
    z = z_ref[...].astype(F32)
    od = od * lax.rsqrt(msd + EPS) * on_ref[...] * (z * jax.nn.sigmoid(z))
    y_d = _dot(od.astype(BF16), wd_ref[...])
    y_a = _dot(at_ref[...], wa_ref[...])
    merged = (jax.nn.sigmoid(ga_ref[...].astype(F32)) * y_a +
              jax.nn.sigmoid(gd_ref[...].astype(F32)) * y_d)
    h = h_ref[...] + _dot(merged.astype(BF16), wo_ref[...])
    h_out[...] = h

    ms = jnp.mean(h * h, axis=-1, keepdims=True)
    xn = h * lax.rsqrt(ms + EPS) * gf_ref[...]
    xh = xn.astype(BF16)
    xn_out[...] = xh
    xl = (xn - xh.astype(F32)).astype(BF16)
    logits = _dot(xh, wrh_ref[...]) + _dot(xl, wrh_ref[...]) + _dot(xh, wrl_ref[...])
    lane = lax.broadcasted_iota(jnp.int32, (1, LANES), 1)
    logits = jnp.where(lane < N_EXPERTS, logits, -jnp.inf)
    e = jnp.exp(logits - jnp.max(logits, axis=-1, keepdims=True))
    aff = e / jnp.sum(e, axis=-1, keepdims=True)
    row = lax.broadcasted_iota(jnp.int32, (tm, 1), 0) + (pl.program_id(0) % n_tiles) * tm
    aff_out[...] = jnp.where(row >= N_NULL, aff, -1.0)


def _post(h2, at, of, ob, z, ga, gd, on, seg, wa, wd, wo, gf, wrh, wrl, lp):
    n = h2.shape[0]
    tm = ROW_TILE
    nt = lp // tm
    row = lambda i: (i, 0)
    const = lambda i: (0, 0)
    hw = N_HEADS_D * DV

    def wspec(shape):
        return pl.BlockSpec(shape, const, pipeline_mode=pl.Buffered(1))

    return pl.pallas_call(
        functools.partial(_post_kernel, tm=tm, n_tiles=nt),
        grid=(n // tm,),
        in_specs=[
            pl.BlockSpec((tm, D_MODEL), row),
            pl.BlockSpec((tm, hw), row),
            pl.BlockSpec((tm, hw), row),
            pl.BlockSpec((tm, hw), row),
            pl.BlockSpec((tm, hw), row),
            pl.BlockSpec((tm, D_MODEL), row),
            pl.BlockSpec((tm, D_MODEL), row),
            pl.BlockSpec((1, hw), const),
            wspec((hw, hw)),
            wspec((hw, D_MODEL)),
            wspec((hw, D_MODEL)),
            wspec((D_MODEL, D_MODEL)),
            pl.BlockSpec((1, D_MODEL), const),
            wspec((D_MODEL, LANES)),
            wspec((D_MODEL, LANES)),
        ],
        out_specs=(pl.BlockSpec((tm, D_MODEL), row), pl.BlockSpec((tm, D_MODEL), row),
                   pl.BlockSpec((tm, LANES), row)),
        out_shape=(jax.ShapeDtypeStruct((n, D_MODEL), F32), jax.ShapeDtypeStruct((n, D_MODEL), BF16),
                   jax.ShapeDtypeStruct((n, LANES), F32)),
        compiler_params=_cparams(("parallel",)),
        name="merge_out",
    )(h2, at, of, ob, z, ga, gd, on, seg, wa, wd, wo, gf, wrh, wrl)


def _expert_kernel(x_ref, gate_ref, wg_ref, wu_ref, wd_ref, y_ref):
    x = x_ref[0]
    a = _dot(x, wg_ref[0])
    b = _dot(x, wu_ref[0])
    hh = (a * jax.nn.sigmoid(a) * b).astype(BF16)
    y_ref[0] = _dot(hh, wd_ref[0]) * gate_ref[0]


def _experts(xe, gates, wg, wu, wd, tm):
    e, cp, d = xe.shape
    tile = lambda ei, i: (ei, i, 0)
    wsp = lambda ei, i: (ei, 0, 0)
    return pl.pallas_call(
        _expert_kernel,
        grid=(e, cp // tm),
        in_specs=[
            pl.BlockSpec((1, tm, d), tile),
            pl.BlockSpec((1, tm, 1), tile),
            pl.BlockSpec((1, d, D_EXPERT), wsp),
            pl.BlockSpec((1, d, D_EXPERT), wsp),
            pl.BlockSpec((1, D_EXPERT, d), wsp),
        ],
        out_specs=pl.BlockSpec((1, tm, d), tile),
        out_shape=jax.ShapeDtypeStruct((e, cp, d), F32),
        compiler_params=_cparams(("parallel", "arbitrary")),
        name="experts",
    )(xe, gates, wg, wu, wd)


def _final_kernel(x_ref, g_ref, o_ref):
    x = x_ref[0]
    ms = jnp.mean(x * x, axis=-1, keepdims=True)
    o_ref[0] = x * lax.rsqrt(ms + EPS) * g_ref[...]


def _final_norm(h3, g, s):
    b = h3.shape[0]
    tm = FRONT
    return pl.pallas_call(
        _final_kernel,
        grid=(b, s // tm),
        in_specs=[pl.BlockSpec((1, tm, D_MODEL), lambda bi, i: (bi, i + FRONT // tm, 0)),
                  pl.BlockSpec((1, D_MODEL), lambda bi, i: (0, 0))],
        out_specs=pl.BlockSpec((1, tm, D_MODEL), lambda bi, i: (bi, i, 0)),
        out_shape=jax.ShapeDtypeStruct((b, s, D_MODEL), F32),
        compiler_params=_cparams(("parallel", "parallel")),
        name="final_norm",
    )(h3, g)


def _rope_tables(s):
    lp = FRONT + s
    t = np.arange(lp) - FRONT
    real = t >= 0
    pos = np.stack([np.where(real, t // GRID_W, 0), np.where(real, t % GRID_W, 0)], axis=-1)
    n_freq = HEAD_DIM // 4
    inv_freq = jnp.asarray(ROPE_THETA, F32) ** (-jnp.arange(n_freq, dtype=F32) / n_freq)
    lane = np.arange(LANES) % HEAD_DIM
    axis = lane // (HEAD_DIM // 2)
    freq = lane % n_freq
    ang = jnp.asarray(pos, F32)[:, axis] * inv_freq[freq][None, :]
    sign = np.where((lane % (HEAD_DIM // 2)) < n_freq, -1.0, 1.0).astype(np.float32)
    return jnp.cos(ang), jnp.sin(ang) * sign[None, :]


def _pack_w_in(w):
    o = 0

    def take(n):
        nonlocal o
        part = w[:, o:o + n]
        o += n
        return part

    def head_blocks(part, n_heads):
        part = part.reshape(D_MODEL, n_heads, HEAD_DIM)
        part = jnp.pad(part, ((0, 0), (0, 0), (0, LANES - HEAD_DIM)))
        return part.reshape(D_MODEL, n_heads * LANES)

    q_a = head_blocks(take(N_HEADS_A * HEAD_DIM), N_HEADS_A)
    k_a = head_blocks(take(N_KV_HEADS * HEAD_DIM), N_KV_HEADS)
    v_a = head_blocks(take(N_KV_HEADS * HEAD_DIM), N_KV_HEADS)
    qkv_d = take(3 * N_HEADS_D * DK)
    z_d = take(N_HEADS_D * DV)
    small = take(4 * N_HEADS_D)
    gate_a = take(D_MODEL)
    gate_d = take(D_MODEL)
    small = jnp.pad(small, ((0, 0), (0, LANES - small.shape[1])))
    w_all = jnp.concatenate([q_a, k_a, qkv_d, z_d, small, gate_a, gate_d], axis=1).astype(BF16)
    return w_all, v_a.T.astype(BF16)


def _chunk_tri(tm, reverse):
    i = np.arange(tm)
    same = (i[:, None] // CHUNK) == (i[None, :] // CHUNK)
    tri = (i[None, :] >= i[:, None]) if reverse else (i[None, :] <= i[:, None])
    return jnp.asarray((same & tri).astype(np.float32))


def _expert_tile(cap):
    n_tiles = -(-cap // 512)
    tm = -(-cap // n_tiles)
    tm = -(-tm // 16) * 16
    return tm, n_tiles


def _trunk(x, meta_tokens, layers, norm_final):
    b, s, d = x.shape
    lp = FRONT + s
    n = b * lp
    n_tok = b * (N_META + s)
    cap = EC_CAPACITY * n_tok // N_EXPERTS
    etm, ent = _expert_tile(cap)
    cap_pad = etm * ent
    cos, sin = _rope_tables(s)
    seg = jnp.asarray(np.kron(np.eye(N_HEADS_D), np.ones((DK, DK))), BF16)
    trif = _chunk_tri(ROW_TILE, False)
    trib = _chunk_tri(ROW_TILE, True)

    front = jnp.concatenate([jnp.zeros((N_NULL, d), F32), meta_tokens.astype(F32)], axis=0)
    h = jnp.concatenate([jnp.broadcast_to(front[None], (b, FRONT, d)), x], axis=1).reshape(n, d)

    for lw in layers:
        q, k, vt, qkvd, z, sm, ga, gd = _inproj(h, lw["norm_mix"], lw["w_all"], lw["wvt"], cos, sin,
                                                lw["qn"], lw["kn"], lp)
        at = _attention(q.reshape(b, lp, -1), k.reshape(b, lp, -1), vt, s)
        qd, kd, vd, gcf, gcb, bf, bb = _delta_prep(qkvd.reshape(b, lp, -1), sm.reshape(b, lp, -1),
                                                   lw["conv_w"], lw["rate"], lw["dtb"], seg, trif, trib)
        of, ob = _delta_scan(qd, kd, vd, gcf, gcb, bf, bb)
        hw = N_HEADS_D * DV
        h, xn, aff = _post(h, at.reshape(n, hw), of.reshape(n, hw), ob.reshape(n, hw), z, ga, gd,
                           lw["on"], seg, lw["w_attn_proj"], lw["w_delta_proj"], lw["w_out"],
                           lw["norm_ffn"], lw["wr_hi"], lw["wr_lo"], lp)
        gates, idx = lax.top_k(aff[:, :N_EXPERTS].T, cap)
        idx = jnp.pad(idx, ((0, 0), (0, cap_pad - cap)))
        gates = jnp.pad(gates, ((0, 0), (0, cap_pad - cap)))
        xe = jnp.take(xn, idx.reshape(-1), axis=0).reshape(N_EXPERTS, cap_pad, d)
        ye = _experts(xe, gates[..., None], lw["w_gate"], lw["w_up"], lw["w_down"], etm)
        h = h.at[idx.reshape(-1)].add(ye.reshape(-1, d))

    return _final_norm(h.reshape(b, lp, d), norm_final, s)


def kernel(x_prompt, x_sample, meta_tokens, norm_mix, w_in, q_norm, k_norm, conv_w, a_log, dt_bias,
           o_norm, w_attn_proj, w_delta_proj, w_out, norm_ffn, w_router, w_gate, w_up, w_down,
           norm_final):
    depth = w_in.shape[0]
    layers = []
    for l in range(depth):
        rate = jnp.exp(a_log[l].astype(F32)).reshape(1, -1)
        dtb = dt_bias[l].astype(F32).reshape(1, -1)
        pad = LANES - rate.shape[1]
        wr = jnp.pad(w_router[l].astype(F32), ((0, 0), (0, LANES - N_EXPERTS)))
        wr_hi = wr.astype(BF16)
        w_all, wvt = _pack_w_in(w_in[l])
        layers.append(dict(
            norm_mix=norm_mix[l].astype(F32).reshape(1, -1),
            w_all=w_all,
            wvt=wvt,
            qn=jnp.tile(q_norm[l].astype(F32), LANES // HEAD_DIM).reshape(1, -1),
            kn=jnp.tile(k_norm[l].astype(F32), LANES // HEAD_DIM).reshape(1, -1),
            conv_w=conv_w[l].astype(F32),
            rate=jnp.pad(rate, ((0, 0), (0, pad))),
            dtb=jnp.pad(dtb, ((0, 0), (0, pad))),
            on=jnp.tile(o_norm[l].astype(F32), N_HEADS_D).reshape(1, -1),
            w_attn_proj=w_attn_proj[l].astype(BF16),
            w_delta_proj=w_delta_proj[l].astype(BF16),
            w_out=w_out[l].astype(BF16),
            norm_ffn=norm_ffn[l].astype(F32).reshape(1, -1),
            wr_hi=wr_hi,
            wr_lo=(wr - wr_hi.astype(F32)).astype(BF16),
            w_gate=w_gate[l].astype(BF16),
            w_up=w_up[l].astype(BF16),
            w_down=w_down[l].astype(BF16),
        ))
    nf = norm_final.astype(F32).reshape(1, -1)
    y_prompt = _trunk(x_prompt, meta_tokens, layers, nf)
    y_sample = _trunk(x_sample, meta_tokens, layers, nf)
    return (y_prompt, y_sample)
```

```python
import functools

import numpy as np
import jax
import jax.numpy as jnp
from jax import lax
from jax.experimental import pallas as pl
from jax.experimental.pallas import tpu as pltpu

F32 = jnp.float32
BF16 = jnp.bfloat16

D_MODEL = 1024
N_META = 16
GRID_W = 64
N_HEADS_A = 8
N_KV_HEADS = 2
HEAD_DIM = 64
ROPE_THETA = 10000.0
N_HEADS_D = 8
DK = 64
DV = 64
CHUNK = 64
N_EXPERTS = 16
EC_CAPACITY = 2
D_EXPERT = 1024
EPS = 1e-6

FRONT = 128
N_NULL = FRONT - N_META
LANES = 128
ROW_TILE = 384
HEAD_GROUP = 4
GROUP_W = HEAD_GROUP * DK
VMEM_LIMIT = 48 * 1024 * 1024
Q_SCALE = HEAD_DIM ** -0.5 * float(np.log2(np.e))
PAIRS_PER_TRIP = 3
V_ROWS = HEAD_DIM + 16

C_Q = 0
C_K = C_Q + N_HEADS_A * LANES
C_QKVD = C_K + N_KV_HEADS * LANES
C_Z = C_QKVD + 3 * N_HEADS_D * DK
C_SM = C_Z + N_HEADS_D * DV
C_GA = C_SM + LANES
C_GD = C_GA + D_MODEL
C_END = C_GD + D_MODEL


def _cparams(sem):
    return pltpu.CompilerParams(dimension_semantics=sem, vmem_limit_bytes=VMEM_LIMIT)


def _dot(a, b):
    return jnp.dot(a, b, preferred_element_type=F32)


def _dot_nt(a, b):
    return lax.dot_general(a, b, (((1,), (1,)), ((), ())), preferred_element_type=F32)


def _split_dot(t, w_bf16):
    hi = t.astype(BF16)
    lo = (t - hi.astype(F32)).astype(BF16)
    return _dot(hi, w_bf16) + _dot(lo, w_bf16)


def _inproj_kernel(x_ref, g_ref, w_ref, wvt_ref, cos_ref, sin_ref, qn_ref, kn_ref,
                   q_ref, k_ref, vt_ref, qkvd_ref, z_ref, sm_ref, ga_ref, gd_ref):
    x = x_ref[...]
    ms = jnp.mean(x * x, axis=-1, keepdims=True)
    u = (x * lax.rsqrt(ms + EPS) * g_ref[...]).astype(BF16)
    cos = cos_ref[...]
    sin = sin_ref[...]
    lane = lax.broadcasted_iota(jnp.int32, (1, LANES), 1)
    first = (lane % 32) < 16

    def rope(t):
        rot = jnp.where(first, pltpu.roll(t, LANES - 16, 1), pltpu.roll(t, 16, 1))
        return t * cos + rot * sin

    def mm(c0, n):
        return _dot(u, w_ref[:, c0:c0 + n])

    def head(c0, gain):
        t = mm(c0, LANES)
        msq = jnp.sum(t * t, axis=-1, keepdims=True) * (1.0 / HEAD_DIM)
        return rope(t * lax.rsqrt(msq + EPS) * gain)

    for h in range(N_HEADS_A):
        q_ref[:, LANES * h:LANES * (h + 1)] = (head(C_Q + LANES * h, qn_ref[...]) * Q_SCALE).astype(BF16)
    for j in range(N_KV_HEADS):
        k_ref[:, LANES * j:LANES * (j + 1)] = head(C_K + LANES * j, kn_ref[...]).astype(BF16)

    vt = _dot_nt(wvt_ref[...], u)
    vrow = lax.broadcasted_iota(jnp.int32, (N_KV_HEADS * V_ROWS, 1), 0) % V_ROWS
    vt_ref[...] = jnp.where(vrow < HEAD_DIM, vt, 1.0).astype(BF16)
    qkvd_ref[...] = mm(C_QKVD, C_Z - C_QKVD)
    z_ref[...] = mm(C_Z, C_SM - C_Z).astype(BF16)
    sm_ref[...] = mm(C_SM, LANES)
    ga_ref[...] = mm(C_GA, D_MODEL).astype(BF16)
    gd_ref[...] = mm(C_GD, D_MODEL).astype(BF16)


def _inproj(h2, g, w_all, wvt, cos, sin, qn, kn, lp):
    n = h2.shape[0]
    tm = ROW_TILE
    nt = lp // tm
    row = lambda i: (i, 0)
    const = lambda i: (0, 0)
    tab = lambda i: (i % nt, 0)
    out_shapes = (
        jax.ShapeDtypeStruct((n, N_HEADS_A * LANES), BF16),
        jax.ShapeDtypeStruct((n, N_KV_HEADS * LANES), BF16),
        jax.ShapeDtypeStruct((N_KV_HEADS * V_ROWS, n), BF16),
        jax.ShapeDtypeStruct((n, C_Z - C_QKVD), F32),
        jax.ShapeDtypeStruct((n, C_SM - C_Z), BF16),
        jax.ShapeDtypeStruct((n, LANES), F32),
        jax.ShapeDtypeStruct((n, D_MODEL), BF16),
        jax.ShapeDtypeStruct((n, D_MODEL), BF16),
    )
    out_specs = [pl.BlockSpec((tm, s.shape[1]), row) for s in out_shapes]
    out_specs[2] = pl.BlockSpec((N_KV_HEADS * V_ROWS, tm), lambda i: (0, i))
    return pl.pallas_call(
        _inproj_kernel,
        grid=(n // tm,),
        in_specs=[
            pl.BlockSpec((tm, D_MODEL), row),
            pl.BlockSpec((1, D_MODEL), const),
            pl.BlockSpec((D_MODEL, C_END), const, pipeline_mode=pl.Buffered(1)),
            pl.BlockSpec((N_KV_HEADS * V_ROWS, D_MODEL), const, pipeline_mode=pl.Buffered(1)),
            pl.BlockSpec((tm, LANES), tab),
            pl.BlockSpec((tm, LANES), tab),
            pl.BlockSpec((1, LANES), const),
            pl.BlockSpec((1, LANES), const),
        ],
        out_specs=tuple(out_specs),
        out_shape=out_shapes,
        compiler_params=_cparams(("parallel",)),
        name="inproj",
    )(h2, g, w_all, wvt, cos, sin, qn, kn)


def _flash_kernel(q_ref, k_ref, vt_ref, o_ref, qs_ref, m_ref, acc_ref, sa_ref, sb_ref, ca_ref, cb_ref,
                  *, tq, tk, s_real):
    g = N_HEADS_A // N_KV_HEADS
    r = g * tq
    for i in range(g):
        qs_ref[i * tq:(i + 1) * tq, :] = q_ref[0, :, LANES * i:LANES * (i + 1)]
    m_ref[...] = jnp.full(m_ref.shape, -jnp.inf, F32)
    acc_ref[...] = jnp.zeros(acc_ref.shape, F32)

    def scores(koff, size):
        return _dot_nt(k_ref[0, pl.ds(koff, size), :], qs_ref[...])

    def absorb(s, cmax, koff, size):
        m_prev = m_ref[...]
        m_new = jnp.maximum(m_prev, cmax)
        alpha = jnp.exp2(m_prev - m_new)
        p = jnp.exp2(s - m_new).astype(BF16)
        acc_ref[...] = alpha * acc_ref[...] + _dot(vt_ref[:, pl.ds(koff, size)], p)
        m_ref[...] = m_new

    def produce(koff, s_ref, c_ref):
        s = scores(koff, tk)
        s_ref[...] = s
        c_ref[...] = jnp.max(s, axis=0, keepdims=True)

    def main_off(c):
        off = FRONT + c * tk
        return off if isinstance(c, int) else pl.multiple_of(off, LANES)

    s = scores(0, FRONT)
    s = jnp.where(lax.broadcasted_iota(jnp.int32, (FRONT, 1), 0) >= N_NULL, s, -jnp.inf)
    absorb(s, jnp.max(s, axis=0, keepdims=True), 0, FRONT)

    n_main = s_real // tk
    if n_main:
        produce(main_off(0), sa_ref, ca_ref)
        n_pairs = (n_main - 1) // 2

        def pair(pi):
            produce(main_off(2 * pi + 1), sb_ref, cb_ref)
            absorb(sa_ref[...], ca_ref[...], main_off(2 * pi), tk)
            produce(main_off(2 * pi + 2), sa_ref, ca_ref)
            absorb(sb_ref[...], cb_ref[...], main_off(2 * pi + 1), tk)

        def body(t, carry):
            for u in range(PAIRS_PER_TRIP):
                pair(t * PAIRS_PER_TRIP + u)
            return carry

        lax.fori_loop(0, n_pairs // PAIRS_PER_TRIP, body, 0)
        for pi in range(n_pairs - n_pairs % PAIRS_PER_TRIP, n_pairs):
            pair(pi)
        if n_main - 2 * n_pairs == 2:
            produce(main_off(2 * n_pairs + 1), sb_ref, cb_ref)
            absorb(sa_ref[...], ca_ref[...], main_off(2 * n_pairs), tk)
            absorb(sb_ref[...], cb_ref[...], main_off(2 * n_pairs + 1), tk)
        else:
            absorb(sa_ref[...], ca_ref[...], main_off(2 * n_pairs), tk)
    if s_real % tk:
        s = scores(FRONT + n_main * tk, s_real % tk)
        absorb(s, jnp.max(s, axis=0, keepdims=True), FRONT + n_main * tk, s_real % tk)

    acc = acc_ref[...]
    out_t = acc[:HEAD_DIM] / acc[HEAD_DIM:HEAD_DIM + 1]
    for p in range(g // 2):
        pair = jnp.concatenate([out_t[:, (2 * p) * tq:(2 * p + 1) * tq],
                                out_t[:, (2 * p + 1) * tq:(2 * p + 2) * tq]], axis=0)
        o_ref[0, :, 2 * HEAD_DIM * p:2 * HEAD_DIM * (p + 1)] = pair.T.astype(BF16)


def _attention(q, k, vt, s_real):
    b, lp, _ = q.shape
    tq = ROW_TILE
    tk = 512
    g = N_HEADS_A // N_KV_HEADS
    return pl.pallas_call(
        functools.partial(_flash_kernel, tq=tq, tk=tk, s_real=s_real),
        grid=(b, N_KV_HEADS, lp // tq),
        in_specs=[
            pl.BlockSpec((1, tq, g * LANES), lambda bi, j, qi: (bi, qi, j)),
            pl.BlockSpec((1, lp, LANES), lambda bi, j, qi: (bi, 0, j)),
            pl.BlockSpec((V_ROWS, lp), lambda bi, j, qi: (j, bi)),
        ],
        out_specs=pl.BlockSpec((1, tq, g * HEAD_DIM), lambda bi, j, qi: (bi, qi, j)),
        out_shape=jax.ShapeDtypeStruct((b, lp, N_HEADS_A * HEAD_DIM), BF16),
        scratch_shapes=[
            pltpu.VMEM((g * tq, LANES), BF16),
            pltpu.VMEM((1, g * tq), F32),
            pltpu.VMEM((V_ROWS, g * tq), F32),
            pltpu.VMEM((tk, g * tq), F32),
            pltpu.VMEM((tk, g * tq), F32),
            pltpu.VMEM((1, g * tq), F32),
            pltpu.VMEM((1, g * tq), F32),
        ],
        compiler_params=_cparams(("parallel", "parallel", "arbitrary")),
        name="attention",
    )(q, k, vt)


def _dprep_kernel(prev_ref, cur_ref, next_ref, sm_ref, cw_ref, rate_ref, dtb_ref, seg_ref,
                  trif_ref, trib_ref, q_ref, k_ref, v_ref, gcf_ref, gcb_ref, bf_ref, bb_ref,
                  *, tm, n_tiles):
    i = pl.program_id(1)
    row = lax.broadcasted_iota(jnp.int32, (tm, 1), 0)
    valid = (row + i * tm) >= N_NULL
    x = jnp.where(valid, cur_ref[0], 0.0)
    prow = jnp.where(i > 0, prev_ref[0, 7:8, :], 0.0)
    nrow = jnp.where(i < n_tiles - 1, next_ref[0, 0:1, :], 0.0)
    xp = jnp.where(row == 0, prow, pltpu.roll(x, 1, 0))
    xn = jnp.where(row == tm - 1, nrow, pltpu.roll(x, tm - 1, 0))
    cw = cw_ref[...]
    y = cw[0:1] * xp + cw[1:2] * x + cw[2:3] * xn
    y = y * jax.nn.sigmoid(y)

    w = N_HEADS_D * DK
    seg = seg_ref[...]
    q = y[:, 0:w]
    q = q * lax.rsqrt(_split_dot(q * q, seg) + EPS) * (DK ** -0.5)
    k = y[:, w:2 * w]
    k = k * lax.rsqrt(_split_dot(k * k, seg) + EPS)
    q_ref[0] = jnp.where(valid, q, 0.0)
    k_ref[0] = jnp.where(valid, k, 0.0)
    v_ref[0] = jnp.where(valid, y[:, 2 * w:3 * w], 0.0)

    sm = sm_ref[0]
    t = sm + dtb_ref[...]
    softplus = jnp.maximum(t, 0.0) + jnp.log1p(jnp.exp(-jnp.abs(t)))
    g_all = jnp.where(valid, -rate_ref[...] * softplus, 0.0)
    beta_all = jnp.where(valid, jax.nn.sigmoid(sm), 0.0)
    gc_f = jnp.dot(trif_ref[...], g_all, preferred_element_type=F32, precision=lax.Precision.HIGHEST)
    gc_b = jnp.dot(trib_ref[...], g_all, preferred_element_type=F32, precision=lax.Precision.HIGHEST)

    lane = lax.broadcasted_iota(jnp.int32, (1, LANES), 1)
    low = lane < DK

    def expand(a, c0, out_ref):
        for p in range(N_HEADS_D // 2):
            e0 = jnp.broadcast_to(a[:, c0 + 2 * p:c0 + 2 * p + 1], (tm, LANES))
            e1 = jnp.broadcast_to(a[:, c0 + 2 * p + 1:c0 + 2 * p + 2], (tm, LANES))
            out_ref[0, :, LANES * p:LANES * (p + 1)] = jnp.where(low, e0, e1)

    expand(gc_f, 0, gcf_ref)
    expand(gc_b, N_HEADS_D, gcb_ref)
    expand(beta_all, 2 * N_HEADS_D, bf_ref)
    expand(beta_all, 3 * N_HEADS_D, bb_ref)


def _delta_prep(qkvd, sm, cw, rate, dtb, seg, trif, trib):
    b, lp, c = qkvd.shape
    tm = ROW_TILE
    nt = lp // tm
    w = N_HEADS_D * DK
    halo = 8
    nb8 = lp // halo
    cur = lambda bi, i: (bi, i, 0)
    const = lambda bi, i: (0, 0)
    out = jax.ShapeDtypeStruct((b, lp, w), F32)
    return pl.pallas_call(
        functools.partial(_dprep_kernel, tm=tm, n_tiles=nt),
        grid=(b, nt),
        in_specs=[
            pl.BlockSpec((1, halo, c), lambda bi, i: (bi, jnp.maximum(i * (tm // halo) - 1, 0), 0)),
            pl.BlockSpec((1, tm, c), cur),
            pl.BlockSpec((1, halo, c), lambda bi, i: (bi, jnp.minimum((i + 1) * (tm // halo), nb8 - 1), 0)),
            pl.BlockSpec((1, tm, LANES), cur),
            pl.BlockSpec((3, c), const),
            pl.BlockSpec((1, LANES), const),
            pl.BlockSpec((1, LANES), const),
            pl.BlockSpec((w, w), const),
            pl.BlockSpec((tm, tm), const),
            pl.BlockSpec((tm, tm), const),
        ],
        out_specs=tuple(pl.BlockSpec((1, tm, w), cur) for _ in range(7)),
        out_shape=(out,) * 7,
        compiler_params=_cparams(("parallel", "parallel")),
        name="delta_prep",
    )(qkvd, qkvd, qkvd, sm, cw, rate, dtb, seg, trif, trib)


def _dscan_kernel(qf_ref, kf_ref, vf_ref, gf_ref, bf_ref, qb_ref, kb_ref, vb_ref, gb_ref, bb_ref,
                  of_ref, ob_ref, s_ref):
    i = pl.program_id(0)

    @pl.when(i == 0)
    def _():
        s_ref[...] = jnp.zeros(s_ref.shape, F32)

    c = CHUNK
    r = lax.broadcasted_iota(jnp.int32, (c, GROUP_W), 0)
    cj = lax.broadcasted_iota(jnp.int32, (c, GROUP_W), 1) % c
    eye = r == cj
    eyef = eye.astype(F32)
    bdmask = (lax.broadcasted_iota(jnp.int32, (GROUP_W, GROUP_W), 0) // c ==
              lax.broadcasted_iota(jnp.int32, (GROUP_W, GROUP_W), 1) // c)

    def bd(t):
        return jnp.where(bdmask, jnp.concatenate([t] * HEAD_GROUP, axis=0), 0.0).astype(BF16)

    n_grp = N_HEADS_D // HEAD_GROUP
    fwd_refs = (qf_ref, kf_ref, vf_ref, gf_ref, bf_ref, of_ref)
    bwd_refs = (qb_ref, kb_ref, vb_ref, gb_ref, bb_ref, ob_ref)
    chains = []
    for bi in range(of_ref.shape[0]):
        for grp in range(n_grp):
            for reverse in (False, True):
                sl = slice(GROUP_W * grp, GROUP_W * (grp + 1))
                chains.append((fwd_refs if not reverse else bwd_refs, bi, sl, reverse, len(chains)))

    def each(fn, *cols):
        return [fn(*args) for args in zip(*cols)]

    rev = [ch[3] for ch in chains]
    q = [ch[0][0][ch[1], :, ch[2]] for ch in chains]
    k = [ch[0][1][ch[1], :, ch[2]] for ch in chains]
    v = [ch[0][2][ch[1], :, ch[2]] for ch in chains]
    gc = [ch[0][3][ch[1], :, ch[2]] for ch in chains]
    beta = [ch[0][4][ch[1], :, ch[2]] for ch in chains]
    state = [s_ref[ch[4]] for ch in chains]

    def decay_of(g, reverse):
        incl = (r <= cj) if reverse else (r >= cj)
        gcol = jnp.sum(jnp.where(eye, g, 0.0), axis=0, keepdims=True)
        return jnp.where(incl, jnp.exp(jnp.where(incl, g - gcol, 0.0)), 0.0)

    decay = each(decay_of, gc, rev)
    eg = each(jnp.exp, gc)
    glast = each(lambda g, reverse: g[0:1] if reverse else g[c - 1:c], gc, rev)
    kbeta = each(lambda a, b: a * b, k, beta)
    gq = each(lambda kb_, q_, k_: _dot_nt(jnp.concatenate([kb_, q_], axis=0).astype(BF16), bd(k_)),
              kbeta, q, k)
    a_intra = each(lambda g, d: g[c:] * d, gq, decay)
    m = each(lambda g, d, reverse: -jnp.where((r < cj) if reverse else (r > cj), g[:c] * d, 0.0),
             gq, decay, rev)
    p = each(lambda m_: eyef + m_, m)
    m = each(lambda m_: _dot(m_.astype(BF16), bd(m_)), m)
    for _ in range(4):
        rr = each(lambda m_, p_: _dot(jnp.concatenate([m_, p_], axis=0).astype(BF16), bd(m_)), m, p)
        p = each(lambda p_, rr_: p_ + rr_[c:], p, rr)
        m = each(lambda rr_: rr_[:c], rr)
    tmat = each(lambda p_, m_: (p_ + _dot(p_.astype(BF16), bd(m_))).astype(BF16), p, m)
    u = each(lambda t, v_, b: _dot(t, bd(v_ * b)), tmat, v, beta)
    w = each(lambda t, kb_, e: _dot(t, bd(kb_ * e)), tmat, kbeta, eg)
    wq = each(lambda w_, q_, e, s: _dot(jnp.concatenate([w_, q_ * e], axis=0).astype(BF16), s.astype(BF16)),
              w, q, eg, state)
    v_new = each(lambda u_, wq_: u_ - wq_[:c], u, wq)
    o = each(lambda wq_, a, vn: wq_[c:] + _dot(a.astype(BF16), bd(vn)), wq, a_intra, v_new)
    upd = each(lambda k_, gl, g, vn: _dot((k_ * jnp.exp(gl - g)).T.astype(BF16), vn.astype(BF16)),
               k, glast, gc, v_new)
    for ch, o_, s, gl, up in zip(chains, o, state, glast, upd):
        ch[0][5][ch[1], :, ch[2]] = o_
        s_ref[ch[4]] = s * jnp.exp(gl) + jnp.where(bdmask, up, 0.0)


def _delta_scan(q, k, v, gcf, gcb, bf, bb):
    b, lp, w = q.shape
    n = lp // CHUNK
    fwd = lambda i: (0, i, 0)
    bwd = lambda i: (0, n - 1 - i, 0)
    blk = (b, CHUNK, w)
    out = jax.ShapeDtypeStruct((b, lp, w), F32)
    return pl.pallas_call(
        _dscan_kernel,
        grid=(n,),
        in_specs=[pl.BlockSpec(blk, fwd)] * 5 + [pl.BlockSpec(blk, bwd)] * 5,
        out_specs=(pl.BlockSpec(blk, fwd), pl.BlockSpec(blk, bwd)),
        out_shape=(out, out),
        scratch_shapes=[pltpu.VMEM((b * 2 * N_HEADS_D // HEAD_GROUP, GROUP_W, GROUP_W), F32)],
        compiler_params=_cparams(("arbitrary",)),
        name="delta_scan",
    )(q, k, v, gcf, bf, q, k, v, gcb, bb)


def _post_kernel(h_ref, at_ref, of_ref, ob_ref, z_ref, ga_ref, gd_ref, on_ref, seg_ref,
                 wa_ref, wd_ref, wo_ref, gf_ref, wrh_ref, wrl_ref,
                 h_out, xn_out, aff_out, *, tm, n_tiles):
    od = of_ref[...] + ob_ref[...]
    msd = _split_dot(od * od, seg_ref[...]) * (1.0 / DV)
    z = z_ref[...].astype(F32)
    od = od * lax.rsqrt(msd + EPS) * on_ref[...] * (z * jax.nn.sigmoid(z))
    y_d = _dot(od.astype(BF16), wd_ref[...])
    y_a = _dot(at_ref[...], wa_ref[...])
    merged = (jax.nn.sigmoid(ga_ref[...].astype(F32)) * y_a +
              jax.nn.sigmoid(gd_ref[...].astype(F32)) * y_d)
    h = h_ref[...] + _dot(merged.astype(BF16), wo_ref[...])
    h_out[...] = h

    ms = jnp.mean(h * h, axis=-1, keepdims=True)
    xn = h * lax.rsqrt(ms + EPS) * gf_ref[...]
    xh = xn.astype(BF16)
    xn_out[...] = xh
    xl = (xn - xh.astype(F32)).astype(BF16)
    logits = _dot(xh, wrh_ref[...]) + _dot(xl, wrh_ref[...]) + _dot(xh, wrl_ref[...])
    lane = lax.broadcasted_iota(jnp.int32, (1, LANES), 1)
    logits = jnp.where(lane < N_EXPERTS, logits, -jnp.inf)
    e = jnp.exp(logits - jnp.max(logits, axis=-1, keepdims=True))
    aff = e / jnp.sum(e, axis=-1, keepdims=True)
    row = lax.broadcasted_iota(jnp.int32, (tm, 1), 0) + (pl.program_id(0) % n_tiles) * tm
    aff_out[...] = jnp.where(row >= N_NULL, aff, -1.0)


def _post(h2, at, of, ob, z, ga, gd, on, seg, wa, wd, wo, gf, wrh, wrl, lp):
    n = h2.shape[0]
    tm = ROW_TILE
    nt = lp // tm
    row = lambda i: (i, 0)
    const = lambda i: (0, 0)
    hw = N_HEADS_D * DV

    def wspec(shape):
        return pl.BlockSpec(shape, const, pipeline_mode=pl.Buffered(1))

    return pl.pallas_call(
        functools.partial(_post_kernel, tm=tm, n_tiles=nt),
        grid=(n // tm,),
        in_specs=[
            pl.BlockSpec((tm, D_MODEL), row),
            pl.BlockSpec((tm, hw), row),
            pl.BlockSpec((tm, hw), row),
            pl.BlockSpec((tm, hw), row),
            pl.BlockSpec((tm, hw), row),
            pl.BlockSpec((tm, D_MODEL), row),
            pl.BlockSpec((tm, D_MODEL), row),
            pl.BlockSpec((1, hw), const),
            wspec((hw, hw)),
            wspec((hw, D_MODEL)),
            wspec((hw, D_MODEL)),
            wspec((D_MODEL, D_MODEL)),
            pl.BlockSpec((1, D_MODEL), const),
            wspec((D_MODEL, LANES)),
            wspec((D_MODEL, LANES)),
        ],
        out_specs=(pl.BlockSpec((tm, D_MODEL), row), pl.BlockSpec((tm, D_MODEL), row),
                   pl.BlockSpec((tm, LANES), row)),
        out_shape=(jax.ShapeDtypeStruct((n, D_MODEL), F32), jax.ShapeDtypeStruct((n, D_MODEL), BF16),
                   jax.ShapeDtypeStruct((n, LANES), F32)),
        compiler_params=_cparams(("parallel",)),
        name="merge_out",
    )(h2, at, of, ob, z, ga, gd, on, seg, wa, wd, wo, gf, wrh, wrl)


def _expert_kernel(x_ref, gate_ref, wg_ref, wu_ref, wd_ref, y_ref):
    x = x_ref[0]
    a = _dot(x, wg_ref[0])
    b = _dot(x, wu_ref[0])
    hh = (a * jax.nn.sigmoid(a) * b).astype(BF16)
    y_ref[0] = _dot(hh, wd_ref[0]) * gate_ref[0]


def _experts(xe, gates, wg, wu, wd, tm):
    e, cp, d = xe.shape
    tile = lambda ei, i: (ei, i, 0)
    wsp = lambda ei, i: (ei, 0, 0)
    return pl.pallas_call(
        _expert_kernel,
        grid=(e, cp // tm),
        in_specs=[
            pl.BlockSpec((1, tm, d), tile),
            pl.BlockSpec((1, tm, 1), tile),
            pl.BlockSpec((1, d, D_EXPERT), wsp),
            pl.BlockSpec((1, d, D_EXPERT), wsp),
            pl.BlockSpec((1, D_EXPERT, d), wsp),
        ],
        out_specs=pl.BlockSpec((1, tm, d), tile),
        out_shape=jax.ShapeDtypeStruct((e, cp, d), F32),
        compiler_params=_cparams(("parallel", "arbitrary")),
        name="experts",
    )(xe, gates, wg, wu, wd)


ROUTE_CHUNK = 128
ROUTE_WINDOW = 2 * ROUTE_CHUNK


def _route_select_kernel(aff_ref, tri_ref, pos_ref, off_ref, *, cap):
    n = aff_ref.shape[0]
    rc = ROUTE_CHUNK
    n_chunks = n // rc

    def bits_of(c):
        return pltpu.bitcast(aff_ref[pl.ds(pl.multiple_of(c * rc, rc), rc), :], jnp.int32)

    def count(pred):
        def body(c, acc):
            return acc + jnp.sum(pred(bits_of(c)).astype(jnp.int32), axis=0, keepdims=True)
        return lax.fori_loop(0, n_chunks, body, jnp.zeros((1, LANES), jnp.int32))

    def search(i, thr):
        cand = thr | jnp.left_shift(jnp.int32(1), 30 - i)
        return jnp.where(count(lambda b: b >= cand) >= cap, cand, thr)

    thr = lax.fori_loop(0, 31, search, jnp.zeros((1, LANES), jnp.int32))
    need = (cap - count(lambda b: b > thr)).astype(F32)
    tri = tri_ref[...]

    def emit(c, carry):
        ties_before, picks_before = carry
        b = bits_of(c)
        tie = b == thr
        tie_incl = _dot(tri, jnp.where(tie, 1.0, 0.0).astype(BF16)) + ties_before
        pick = (b > thr) | (tie & (tie_incl - 1.0 < need))
        pick_incl = _dot(tri, jnp.where(pick, 1.0, 0.0).astype(BF16)) + picks_before
        pos_ref[pl.ds(pl.multiple_of(c * rc, rc), rc), :] = jnp.where(pick, pick_incl - 1.0, -1.0)
        off_ref[c] = picks_before.astype(jnp.int32)
        return tie_incl[rc - 1:rc], pick_incl[rc - 1:rc]

    zero = jnp.zeros((1, LANES), F32)
    lax.fori_loop(0, n_chunks, emit, (zero, zero))


def _route_compact_kernel(off_ref, pos_ref, aff_ref, idx_ref, gate_ref):
    c = pl.program_id(0)
    rc = ROUTE_CHUNK

    @pl.when(c == 0)
    def _():
        idx_ref[...] = jnp.zeros(idx_ref.shape, F32)
        gate_ref[...] = jnp.zeros(gate_ref.shape, F32)

    token = (lax.broadcasted_iota(jnp.int32, (rc, 1), 0) + c * rc).astype(F32)
    slot = lax.broadcasted_iota(jnp.int32, (1, ROUTE_WINDOW), 1).astype(F32)
    pos = pos_ref[...]
    aff = aff_ref[...]
    for e in range(N_EXPERTS):
        start = pl.multiple_of((off_ref[c * N_EXPERTS + e] // LANES) * LANES, LANES)
        hit = (pos[:, e:e + 1] - start.astype(F32)) == slot
        win = (slice(e, e + 1), pl.ds(start, ROUTE_WINDOW))
        idx_ref[win] += jnp.sum(jnp.where(hit, token, 0.0), axis=0, keepdims=True)
        gate_ref[win] += jnp.sum(jnp.where(hit, aff[:, e:e + 1], 0.0), axis=0, keepdims=True)


def _route(aff, cap, slots):
    n = aff.shape[0]
    rc = ROUTE_CHUNK
    n_chunks = n // rc
    tri = jnp.asarray(np.tril(np.ones((rc, rc), np.float32)), BF16)
    whole = pl.BlockSpec(memory_space=pltpu.VMEM)
    pos, off = pl.pallas_call(
        functools.partial(_route_select_kernel, cap=cap),
        in_specs=[whole, whole],
        out_specs=(whole, whole),
        out_shape=(jax.ShapeDtypeStruct((n, LANES), F32),
                   jax.ShapeDtypeStruct((n_chunks, 1, LANES), jnp.int32)),
        compiler_params=pltpu.CompilerParams(vmem_limit_bytes=VMEM_LIMIT),
        name="route_select",
    )(aff, tri)
    off = off[:, 0, :N_EXPERTS].reshape(-1)
    chunk = lambda c, off_ref: (c, 0)
    fixed = lambda c, off_ref: (0, 0)
    out = jax.ShapeDtypeStruct((N_EXPERTS, slots), F32)
    idx, gates = pl.pallas_call(
        _route_compact_kernel,
        grid_spec=pltpu.PrefetchScalarGridSpec(
            num_scalar_prefetch=1,
            grid=(n_chunks,),
            in_specs=[pl.BlockSpec((rc, LANES), chunk), pl.BlockSpec((rc, LANES), chunk)],
            out_specs=(pl.BlockSpec((N_EXPERTS, slots), fixed), pl.BlockSpec((N_EXPERTS, slots), fixed)),
        ),
        out_shape=(out, out),
        compiler_params=_cparams(("arbitrary",)),
        name="route_compact",
    )(off, pos, aff)
    return idx.astype(jnp.int32), gates


def _final_kernel(x_ref, g_ref, o_ref):
    x = x_ref[0]
    ms = jnp.mean(x * x, axis=-1, keepdims=True)
    o_ref[0] = x * lax.rsqrt(ms + EPS) * g_ref[...]


def _final_norm(h3, g, s):
    b = h3.shape[0]
    tm = FRONT
    return pl.pallas_call(
        _final_kernel,
        grid=(b, s // tm),
        in_specs=[pl.BlockSpec((1, tm, D_MODEL), lambda bi, i: (bi, i + FRONT // tm, 0)),
                  pl.BlockSpec((1, D_MODEL), lambda bi, i: (0, 0))],
        out_specs=pl.BlockSpec((1, tm, D_MODEL), lambda bi, i: (bi, i, 0)),
        out_shape=jax.ShapeDtypeStruct((b, s, D_MODEL), F32),
        compiler_params=_cparams(("parallel", "parallel")),
        name="final_norm",
    )(h3, g)


def _rope_tables(s):
    lp = FRONT + s
    t = np.arange(lp) - FRONT
    real = t >= 0
    pos = np.stack([np.where(real, t // GRID_W, 0), np.where(real, t % GRID_W, 0)], axis=-1)
    n_freq = HEAD_DIM // 4
    inv_freq = jnp.asarray(ROPE_THETA, F32) ** (-jnp.arange(n_freq, dtype=F32) / n_freq)
    lane = np.arange(LANES) % HEAD_DIM
    axis = lane // (HEAD_DIM // 2)
    freq = lane % n_freq
    ang = jnp.asarray(pos, F32)[:, axis] * inv_freq[freq][None, :]
    sign = np.where((lane % (HEAD_DIM // 2)) < n_freq, -1.0, 1.0).astype(np.float32)
    return jnp.cos(ang), jnp.sin(ang) * sign[None, :]


def _pack_w_in(w):
    o = 0

    def take(n):
        nonlocal o
        part = w[:, o:o + n]
        o += n
        return part

    def head_blocks(part, n_heads):
        part = part.reshape(D_MODEL, n_heads, HEAD_DIM)
        part = jnp.pad(part, ((0, 0), (0, 0), (0, LANES - HEAD_DIM)))
        return part.reshape(D_MODEL, n_heads * LANES)

    q_a = head_blocks(take(N_HEADS_A * HEAD_DIM), N_HEADS_A)
    k_a = head_blocks(take(N_KV_HEADS * HEAD_DIM), N_KV_HEADS)
    v_a = take(N_KV_HEADS * HEAD_DIM).reshape(D_MODEL, N_KV_HEADS, HEAD_DIM)
    v_a = jnp.pad(v_a, ((0, 0), (0, 0), (0, V_ROWS - HEAD_DIM))).reshape(D_MODEL, N_KV_HEADS * V_ROWS)
    qkv_d = take(3 * N_HEADS_D * DK)
    z_d = take(N_HEADS_D * DV)
    small = take(4 * N_HEADS_D)
    gate_a = take(D_MODEL)
    gate_d = take(D_MODEL)
    small = jnp.pad(small, ((0, 0), (0, LANES - small.shape[1])))
    w_all = jnp.concatenate([q_a, k_a, qkv_d, z_d, small, gate_a, gate_d], axis=1).astype(BF16)
    return w_all, v_a.T.astype(BF16)


def _chunk_tri(tm, reverse):
    i = np.arange(tm)
    same = (i[:, None] // CHUNK) == (i[None, :] // CHUNK)
    tri = (i[None, :] >= i[:, None]) if reverse else (i[None, :] <= i[:, None])
    return jnp.asarray((same & tri).astype(np.float32))


def _expert_tile(cap):
    n_tiles = -(-cap // 512)
    tm = -(-cap // n_tiles)
    tm = -(-tm // 16) * 16
    return tm, n_tiles


def _trunk(x, meta_tokens, layers, norm_final):
    b, s, d = x.shape
    lp = FRONT + s
    n = b * lp
    n_tok = b * (N_META + s)
    cap = EC_CAPACITY * n_tok // N_EXPERTS
    etm, ent = _expert_tile(cap)
    cap_pad = etm * ent
    assert cap_pad - cap <= N_NULL
    slots = -(-max(cap_pad, cap + ROUTE_WINDOW) // LANES) * LANES
    cos, sin = _rope_tables(s)
    seg = jnp.asarray(np.kron(np.eye(N_HEADS_D), np.ones((DK, DK))), BF16)
    trif = _chunk_tri(ROW_TILE, False)
    trib = _chunk_tri(ROW_TILE, True)

    front = jnp.concatenate([jnp.zeros((N_NULL, d), F32), meta_tokens.astype(F32)], axis=0)
    h = jnp.concatenate([jnp.broadcast_to(front[None], (b, FRONT, d)), x], axis=1).reshape(n, d)

    for lw in layers:
        q, k, vt, qkvd, z, sm, ga, gd = _inproj(h, lw["norm_mix"], lw["w_all"], lw["wvt"], cos, sin,
                                                lw["qn"], lw["kn"], lp)
        at = _attention(q.reshape(b, lp, -1), k.reshape(b, lp, -1), vt, s)
        qd, kd, vd, gcf, gcb, bf, bb = _delta_prep(qkvd.reshape(b, lp, -1), sm.reshape(b, lp, -1),
                                                   lw["conv_w"], lw["rate"], lw["dtb"], seg, trif, trib)
        of, ob = _delta_scan(qd, kd, vd, gcf, gcb, bf, bb)
        hw = N_HEADS_D * DV
        h, xn, aff = _post(h, at.reshape(n, hw), of.reshape(n, hw), ob.reshape(n, hw), z, ga, gd,
                           lw["on"], seg, lw["w_attn_proj"], lw["w_delta_proj"], lw["w_out"],
                           lw["norm_ffn"], lw["wr_hi"], lw["wr_lo"], lp)
        idx, gates = _route(aff, cap, slots)
        spare = np.arange(cap_pad) - cap
        idx = jnp.where(spare[None, :] >= 0, spare[None, :], idx[:, :cap_pad])
        gates = gates[:, :cap_pad]
        xe = jnp.take(xn, idx.reshape(-1), axis=0).reshape(N_EXPERTS, cap_pad, d)
        ye = _experts(xe, gates[..., None], lw["w_gate"], lw["w_up"], lw["w_down"], etm)
        for e in range(N_EXPERTS):
            h = h.at[idx[e]].add(ye[e], unique_indices=True)

    return _final_norm(h.reshape(b, lp, d), norm_final, s)


def kernel(x_prompt, x_sample, meta_tokens, norm_mix, w_in, q_norm, k_norm, conv_w, a_log, dt_bias,
           o_norm, w_attn_proj, w_delta_proj, w_out, norm_ffn, w_router, w_gate, w_up, w_down,
           norm_final):
    depth = w_in.shape[0]
    layers = []
    for l in range(depth):
        rate = jnp.exp(a_log[l].astype(F32)).reshape(1, -1)
        dtb = dt_bias[l].astype(F32).reshape(1, -1)
        pad = LANES - rate.shape[1]
        wr = jnp.pad(w_router[l].astype(F32), ((0, 0), (0, LANES - N_EXPERTS)))
        wr_hi = wr.astype(BF16)
        w_all, wvt = _pack_w_in(w_in[l])
        layers.append(dict(
            norm_mix=norm_mix[l].astype(F32).reshape(1, -1),
            w_all=w_all,
            wvt=wvt,
            qn=jnp.tile(q_norm[l].astype(F32), LANES // HEAD_DIM).reshape(1, -1),
            kn=jnp.tile(k_norm[l].astype(F32), LANES // HEAD_DIM).reshape(1, -1),
            conv_w=conv_w[l].astype(F32),
            rate=jnp.pad(rate, ((0, 0), (0, pad))),
            dtb=jnp.pad(dtb, ((0, 0), (0, pad))),
            on=jnp.tile(o_norm[l].astype(F32), N_HEADS_D).reshape(1, -1),
            w_attn_proj=w_attn_proj[l].astype(BF16),
            w_delta_proj=w_delta_proj[l].astype(BF16),
            w_out=w_out[l].astype(BF16),
            norm_ffn=norm_ffn[l].astype(F32).reshape(1, -1),
            wr_hi=wr_hi,
            wr_lo=(wr - wr_hi.astype(F32)).astype(BF16),
            w_gate=w_gate[l].astype(BF16),
            w_up=w_up[l].astype(BF16),
            w_down=w_down[l].astype(BF16),
        ))
    nf = norm_final.astype(F32).reshape(1, -1)
    y_prompt = _trunk(x_prompt, meta_tokens, layers, nf)
    y_sample = _trunk(x_sample, meta_tokens, layers, nf)
    return (y_prompt, y_sample)
```

```python
import functools

import numpy as np
import jax
import jax.numpy as jnp
from jax import lax
from jax.experimental import pallas as pl
from jax.experimental.pallas import tpu as pltpu

F32 = jnp.float32
BF16 = jnp.bfloat16

D_MODEL = 1024
N_META = 16
GRID_W = 64
N_HEADS_A = 8
N_KV_HEADS = 2
HEAD_DIM = 64
ROPE_THETA = 10000.0
N_HEADS_D = 8
DK = 64
DV = 64
CHUNK = 64
N_EXPERTS = 16
EC_CAPACITY = 2
D_EXPERT = 1024
EPS = 1e-6

FRONT = 128
N_NULL = FRONT - N_META
LANES = 128
ROW_TILE = 384
HEAD_GROUP = 4
GROUP_W = HEAD_GROUP * DK
VMEM_LIMIT = 48 * 1024 * 1024
Q_SCALE = HEAD_DIM ** -0.5 * float(np.log2(np.e))
PAIRS_PER_TRIP = 3
V_ROWS = HEAD_DIM + 16

C_Q = 0
C_K = C_Q + N_HEADS_A * LANES
C_QKVD = C_K + N_KV_HEADS * LANES
C_Z = C_QKVD + 3 * N_HEADS_D * DK
C_SM = C_Z + N_HEADS_D * DV
C_GA = C_SM + LANES
C_GD = C_GA + D_MODEL
C_END = C_GD + D_MODEL


def _cparams(sem):
    return pltpu.CompilerParams(dimension_semantics=sem, vmem_limit_bytes=VMEM_LIMIT)


def _dot(a, b):
    return jnp.dot(a, b, preferred_element_type=F32)


def _dot_nt(a, b):
    return lax.dot_general(a, b, (((1,), (1,)), ((), ())), preferred_element_type=F32)


def _split_dot(t, w_bf16):
    hi = t.astype(BF16)
    lo = (t - hi.astype(F32)).astype(BF16)
    return _dot(hi, w_bf16) + _dot(lo, w_bf16)


def _inproj_kernel(x_ref, g_ref, w_ref, wvt_ref, cos_ref, sin_ref, qn_ref, kn_ref,
                   q_ref, k_ref, vt_ref, qkvd_ref, z_ref, sm_ref, ga_ref, gd_ref):
    x = x_ref[...]
    ms = jnp.mean(x * x, axis=-1, keepdims=True)
    u = (x * lax.rsqrt(ms + EPS) * g_ref[...]).astype(BF16)
    cos = cos_ref[...]
    sin = sin_ref[...]
    lane = lax.broadcasted_iota(jnp.int32, (1, LANES), 1)
    first = (lane % 32) < 16

    def rope(t):
        rot = jnp.where(first, pltpu.roll(t, LANES - 16, 1), pltpu.roll(t, 16, 1))
        return t * cos + rot * sin

    def mm(c0, n):
        return _dot(u, w_ref[:, c0:c0 + n])

    def head(c0, gain):
        t = mm(c0, LANES)
        msq = jnp.sum(t * t, axis=-1, keepdims=True) * (1.0 / HEAD_DIM)
        return rope(t * lax.rsqrt(msq + EPS) * gain)

    for h in range(N_HEADS_A):
        q_ref[:, LANES * h:LANES * (h + 1)] = (head(C_Q + LANES * h, qn_ref[...]) * Q_SCALE).astype(BF16)
    for j in range(N_KV_HEADS):
        k_ref[:, LANES * j:LANES * (j + 1)] = head(C_K + LANES * j, kn_ref[...]).astype(BF16)

    vt = _dot_nt(wvt_ref[...], u)
    vrow = lax.broadcasted_iota(jnp.int32, (N_KV_HEADS * V_ROWS, 1), 0) % V_ROWS
    vt_ref[...] = jnp.where(vrow < HEAD_DIM, vt, 1.0).astype(BF16)
    qkvd_ref[...] = mm(C_QKVD, C_Z - C_QKVD)
    z_ref[...] = mm(C_Z, C_SM - C_Z).astype(BF16)
    sm_ref[...] = mm(C_SM, LANES)
    ga_ref[...] = mm(C_GA, D_MODEL).astype(BF16)
    gd_ref[...] = mm(C_GD, D_MODEL).astype(BF16)


def _inproj(h2, g, w_all, wvt, cos, sin, qn, kn, lp):
    n = h2.shape[0]
    tm = ROW_TILE
    nt = lp // tm
    row = lambda i: (i, 0)
    const = lambda i: (0, 0)
    tab = lambda i: (i % nt, 0)
    out_shapes = (
        jax.ShapeDtypeStruct((n, N_HEADS_A * LANES), BF16),
        jax.ShapeDtypeStruct((n, N_KV_HEADS * LANES), BF16),
        jax.ShapeDtypeStruct((N_KV_HEADS * V_ROWS, n), BF16),
        jax.ShapeDtypeStruct((n, C_Z - C_QKVD), F32),
        jax.ShapeDtypeStruct((n, C_SM - C_Z), BF16),
        jax.ShapeDtypeStruct((n, LANES), F32),
        jax.ShapeDtypeStruct((n, D_MODEL), BF16),
        jax.ShapeDtypeStruct((n, D_MODEL), BF16),
    )
    out_specs = [pl.BlockSpec((tm, s.shape[1]), row) for s in out_shapes]
    out_specs[2] = pl.BlockSpec((N_KV_HEADS * V_ROWS, tm), lambda i: (0, i))
    return pl.pallas_call(
        _inproj_kernel,
        grid=(n // tm,),
        in_specs=[
            pl.BlockSpec((tm, D_MODEL), row),
            pl.BlockSpec((1, D_MODEL), const),
            pl.BlockSpec((D_MODEL, C_END), const, pipeline_mode=pl.Buffered(1)),
            pl.BlockSpec((N_KV_HEADS * V_ROWS, D_MODEL), const, pipeline_mode=pl.Buffered(1)),
            pl.BlockSpec((tm, LANES), tab),
            pl.BlockSpec((tm, LANES), tab),
            pl.BlockSpec((1, LANES), const),
            pl.BlockSpec((1, LANES), const),
        ],
        out_specs=tuple(out_specs),
        out_shape=out_shapes,
        compiler_params=_cparams(("parallel",)),
        name="inproj",
    )(h2, g, w_all, wvt, cos, sin, qn, kn)


def _flash_kernel(q_ref, k_ref, vt_ref, o_ref, qs_ref, m_ref, acc_ref, sa_ref, sb_ref, ca_ref, cb_ref,
                  *, tq, tk, s_real):
    g = N_HEADS_A // N_KV_HEADS
    r = g * tq
    for i in range(g):
        qs_ref[i * tq:(i + 1) * tq, :] = q_ref[0, :, LANES * i:LANES * (i + 1)]
    m_ref[...] = jnp.full(m_ref.shape, -jnp.inf, F32)
    acc_ref[...] = jnp.zeros(acc_ref.shape, F32)

    def scores(koff, size):
        return _dot_nt(k_ref[0, pl.ds(koff, size), :], qs_ref[...])

    def absorb(s, cmax, koff, size):
        m_prev = m_ref[...]
        m_new = jnp.maximum(m_prev, cmax)
        alpha = jnp.exp2(m_prev - m_new)
        p = jnp.exp2(s - m_new).astype(BF16)
        acc_ref[...] = alpha * acc_ref[...] + _dot(vt_ref[:, pl.ds(koff, size)], p)
        m_ref[...] = m_new

    def produce(koff, s_ref, c_ref):
        s = scores(koff, tk)
        s_ref[...] = s
        c_ref[...] = jnp.max(s, axis=0, keepdims=True)

    def main_off(c):
        off = FRONT + c * tk
        return off if isinstance(c, int) else pl.multiple_of(off, LANES)

    s = scores(0, FRONT)
    s = jnp.where(lax.broadcasted_iota(jnp.int32, (FRONT, 1), 0) >= N_NULL, s, -jnp.inf)
    absorb(s, jnp.max(s, axis=0, keepdims=True), 0, FRONT)

    n_main = s_real // tk
    if n_main:
        produce(main_off(0), sa_ref, ca_ref)
        n_pairs = (n_main - 1) // 2

        def pair(pi):
            produce(main_off(2 * pi + 1), sb_ref, cb_ref)
            absorb(sa_ref[...], ca_ref[...], main_off(2 * pi), tk)
            produce(main_off(2 * pi + 2), sa_ref, ca_ref)
            absorb(sb_ref[...], cb_ref[...], main_off(2 * pi + 1), tk)

        def body(t, carry):
            for u in range(PAIRS_PER_TRIP):
                pair(t * PAIRS_PER_TRIP + u)
            return carry

        lax.fori_loop(0, n_pairs // PAIRS_PER_TRIP, body, 0)
        for pi in range(n_pairs - n_pairs % PAIRS_PER_TRIP, n_pairs):
            pair(pi)
        if n_main - 2 * n_pairs == 2:
            produce(main_off(2 * n_pairs + 1), sb_ref, cb_ref)
            absorb(sa_ref[...], ca_ref[...], main_off(2 * n_pairs), tk)
            absorb(sb_ref[...], cb_ref[...], main_off(2 * n_pairs + 1), tk)
        else:
            absorb(sa_ref[...], ca_ref[...], main_off(2 * n_pairs), tk)
    if s_real % tk:
        s = scores(FRONT + n_main * tk, s_real % tk)
        absorb(s, jnp.max(s, axis=0, keepdims=True), FRONT + n_main * tk, s_real % tk)

    acc = acc_ref[...]
    out_t = acc[:HEAD_DIM] / acc[HEAD_DIM:HEAD_DIM + 1]
    for p in range(g // 2):
        pair = jnp.concatenate([out_t[:, (2 * p) * tq:(2 * p + 1) * tq],
                                out_t[:, (2 * p + 1) * tq:(2 * p + 2) * tq]], axis=0)
        o_ref[0, :, 2 * HEAD_DIM * p:2 * HEAD_DIM * (p + 1)] = pair.T.astype(BF16)


def _attention(q, k, vt, s_real):
    b, lp, _ = q.shape
    tq = ROW_TILE
    tk = 512
    g = N_HEADS_A // N_KV_HEADS
    return pl.pallas_call(
        functools.partial(_flash_kernel, tq=tq, tk=tk, s_real=s_real),
        grid=(b, N_KV_HEADS, lp // tq),
        in_specs=[
            pl.BlockSpec((1, tq, g * LANES), lambda bi, j, qi: (bi, qi, j)),
            pl.BlockSpec((1, lp, LANES), lambda bi, j, qi: (bi, 0, j)),
            pl.BlockSpec((V_ROWS, lp), lambda bi, j, qi: (j, bi)),
        ],
        out_specs=pl.BlockSpec((1, tq, g * HEAD_DIM), lambda bi, j, qi: (bi, qi, j)),
        out_shape=jax.ShapeDtypeStruct((b, lp, N_HEADS_A * HEAD_DIM), BF16),
        scratch_shapes=[
            pltpu.VMEM((g * tq, LANES), BF16),
            pltpu.VMEM((1, g * tq), F32),
            pltpu.VMEM((V_ROWS, g * tq), F32),
            pltpu.VMEM((tk, g * tq), F32),
            pltpu.VMEM((tk, g * tq), F32),
            pltpu.VMEM((1, g * tq), F32),
            pltpu.VMEM((1, g * tq), F32),
        ],
        compiler_params=_cparams(("parallel", "parallel", "arbitrary")),
        name="attention",
    )(q, k, vt)


def _dprep_kernel(prev_ref, cur_ref, next_ref, sm_ref, cw_ref, rate_ref, dtb_ref, seg_ref,
                  trif_ref, trib_ref, q_ref, k_ref, v_ref, gcf_ref, gcb_ref, bf_ref, bb_ref,
                  *, tm, n_tiles):
    i = pl.program_id(1)
    row = lax.broadcasted_iota(jnp.int32, (tm, 1), 0)
    valid = (row + i * tm) >= N_NULL
    x = jnp.where(valid, cur_ref[0], 0.0)
    prow = jnp.where(i > 0, prev_ref[0, 7:8, :], 0.0)
    nrow = jnp.where(i < n_tiles - 1, next_ref[0, 0:1, :], 0.0)
    xp = jnp.where(row == 0, prow, pltpu.roll(x, 1, 0))
    xn = jnp.where(row == tm - 1, nrow, pltpu.roll(x, tm - 1, 0))
    cw = cw_ref[...]
    y = cw[0:1] * xp + cw[1:2] * x + cw[2:3] * xn
    y = y * jax.nn.sigmoid(y)

    w = N_HEADS_D * DK
    seg = seg_ref[...]
    q = y[:, 0:w]
    q = q * lax.rsqrt(_split_dot(q * q, seg) + EPS) * (DK ** -0.5)
    k = y[:, w:2 * w]
    k = k * lax.rsqrt(_split_dot(k * k, seg) + EPS)
    q_ref[0] = jnp.where(valid, q, 0.0)
    k_ref[0] = jnp.where(valid, k, 0.0)
    v_ref[0] = jnp.where(valid, y[:, 2 * w:3 * w], 0.0)

    sm = sm_ref[0]
    t = sm + dtb_ref[...]
    softplus = jnp.maximum(t, 0.0) + jnp.log1p(jnp.exp(-jnp.abs(t)))
    g_all = jnp.where(valid, -rate_ref[...] * softplus, 0.0)
    beta_all = jnp.where(valid, jax.nn.sigmoid(sm), 0.0)
    gc_f = jnp.dot(trif_ref[...], g_all, preferred_element_type=F32, precision=lax.Precision.HIGHEST)
    gc_b = jnp.dot(trib_ref[...], g_all, preferred_element_type=F32, precision=lax.Precision.HIGHEST)

    lane = lax.broadcasted_iota(jnp.int32, (1, LANES), 1)
    low = lane < DK

    def expand(a, c0, out_ref):
        for p in range(N_HEADS_D // 2):
            e0 = jnp.broadcast_to(a[:, c0 + 2 * p:c0 + 2 * p + 1], (tm, LANES))
            e1 = jnp.broadcast_to(a[:, c0 + 2 * p + 1:c0 + 2 * p + 2], (tm, LANES))
            out_ref[0, :, LANES * p:LANES * (p + 1)] = jnp.where(low, e0, e1)

    expand(gc_f, 0, gcf_ref)
    expand(gc_b, N_HEADS_D, gcb_ref)
    expand(beta_all, 2 * N_HEADS_D, bf_ref)
    expand(beta_all, 3 * N_HEADS_D, bb_ref)


def _delta_prep(qkvd, sm, cw, rate, dtb, seg, trif, trib):
    b, lp, c = qkvd.shape
    tm = ROW_TILE
    nt = lp // tm
    w = N_HEADS_D * DK
    halo = 8
    nb8 = lp // halo
    cur = lambda bi, i: (bi, i, 0)
    const = lambda bi, i: (0, 0)
    out = jax.ShapeDtypeStruct((b, lp, w), F32)
    return pl.pallas_call(
        functools.partial(_dprep_kernel, tm=tm, n_tiles=nt),
        grid=(b, nt),
        in_specs=[
            pl.BlockSpec((1, halo, c), lambda bi, i: (bi, jnp.maximum(i * (tm // halo) - 1, 0), 0)),
            pl.BlockSpec((1, tm, c), cur),
            pl.BlockSpec((1, halo, c), lambda bi, i: (bi, jnp.minimum((i + 1) * (tm // halo), nb8 - 1), 0)),
            pl.BlockSpec((1, tm, LANES), cur),
            pl.BlockSpec((3, c), const),
            pl.BlockSpec((1, LANES), const),
            pl.BlockSpec((1, LANES), const),
            pl.BlockSpec((w, w), const),
            pl.BlockSpec((tm, tm), const),
            pl.BlockSpec((tm, tm), const),
        ],
        out_specs=tuple(pl.BlockSpec((1, tm, w), cur) for _ in range(7)),
        out_shape=(out,) * 7,
        compiler_params=_cparams(("parallel", "parallel")),
        name="delta_prep",
    )(qkvd, qkvd, qkvd, sm, cw, rate, dtb, seg, trif, trib)


def _dscan_kernel(qf_ref, kf_ref, vf_ref, gf_ref, bf_ref, qb_ref, kb_ref, vb_ref, gb_ref, bb_ref,
                  of_ref, ob_ref, s_ref):
    i = pl.program_id(0)

    @pl.when(i == 0)
    def _():
        s_ref[...] = jnp.zeros(s_ref.shape, F32)

    c = CHUNK
    r = lax.broadcasted_iota(jnp.int32, (c, GROUP_W), 0)
    cj = lax.broadcasted_iota(jnp.int32, (c, GROUP_W), 1) % c
    eye = r == cj
    eyef = eye.astype(F32)
    bdmask = (lax.broadcasted_iota(jnp.int32, (GROUP_W, GROUP_W), 0) // c ==
              lax.broadcasted_iota(jnp.int32, (GROUP_W, GROUP_W), 1) // c)

    def bd(t):
        return jnp.where(bdmask, jnp.concatenate([t] * HEAD_GROUP, axis=0), 0.0).astype(BF16)

    n_grp = N_HEADS_D // HEAD_GROUP
    fwd_refs = (qf_ref, kf_ref, vf_ref, gf_ref, bf_ref, of_ref)
    bwd_refs = (qb_ref, kb_ref, vb_ref, gb_ref, bb_ref, ob_ref)
    chains = []
    for bi in range(of_ref.shape[0]):
        for grp in range(n_grp):
            for reverse in (False, True):
                sl = slice(GROUP_W * grp, GROUP_W * (grp + 1))
                chains.append((fwd_refs if not reverse else bwd_refs, bi, sl, reverse, len(chains)))

    def each(fn, *cols):
        return [fn(*args) for args in zip(*cols)]

    rev = [ch[3] for ch in chains]
    q = [ch[0][0][ch[1], :, ch[2]] for ch in chains]
    k = [ch[0][1][ch[1], :, ch[2]] for ch in chains]
    v = [ch[0][2][ch[1], :, ch[2]] for ch in chains]
    gc = [ch[0][3][ch[1], :, ch[2]] for ch in chains]
    beta = [ch[0][4][ch[1], :, ch[2]] for ch in chains]
    state = [s_ref[ch[4]] for ch in chains]

    def decay_of(g, reverse):
        incl = (r <= cj) if reverse else (r >= cj)
        gcol = jnp.sum(jnp.where(eye, g, 0.0), axis=0, keepdims=True)
        return jnp.where(incl, jnp.exp(jnp.where(incl, g - gcol, 0.0)), 0.0)

    decay = each(decay_of, gc, rev)
    eg = each(jnp.exp, gc)
    glast = each(lambda g, reverse: g[0:1] if reverse else g[c - 1:c], gc, rev)
    kbeta = each(lambda a, b: a * b, k, beta)
    gq = each(lambda kb_, q_, k_: _dot_nt(jnp.concatenate([kb_, q_], axis=0).astype(BF16), bd(k_)),
              kbeta, q, k)
    a_intra = each(lambda g, d: g[c:] * d, gq, decay)
    m = each(lambda g, d, reverse: -jnp.where((r < cj) if reverse else (r > cj), g[:c] * d, 0.0),
             gq, decay, rev)
    p = each(lambda m_: eyef + m_, m)
    m = each(lambda m_: _dot(m_.astype(BF16), bd(m_)), m)
    for _ in range(4):
        rr = each(lambda m_, p_: _dot(jnp.concatenate([m_, p_], axis=0).astype(BF16), bd(m_)), m, p)
        p = each(lambda p_, rr_: p_ + rr_[c:], p, rr)
        m = each(lambda rr_: rr_[:c], rr)
    tmat = each(lambda p_, m_: (p_ + _dot(p_.astype(BF16), bd(m_))).astype(BF16), p, m)
    u = each(lambda t, v_, b: _dot(t, bd(v_ * b)), tmat, v, beta)
    w = each(lambda t, kb_, e: _dot(t, bd(kb_ * e)), tmat, kbeta, eg)
    wq = each(lambda w_, q_, e, s: _dot(jnp.concatenate([w_, q_ * e], axis=0).astype(BF16), s.astype(BF16)),
              w, q, eg, state)
    v_new = each(lambda u_, wq_: u_ - wq_[:c], u, wq)
    o = each(lambda wq_, a, vn: wq_[c:] + _dot(a.astype(BF16), bd(vn)), wq, a_intra, v_new)
    upd = each(lambda k_, gl, g, vn: _dot((k_ * jnp.exp(gl - g)).T.astype(BF16), vn.astype(BF16)),
               k, glast, gc, v_new)
    for ch, o_, s, gl, up in zip(chains, o, state, glast, upd):
        ch[0][5][ch[1], :, ch[2]] = o_
        s_ref[ch[4]] = s * jnp.exp(gl) + jnp.where(bdmask, up, 0.0)


def _delta_scan(q, k, v, gcf, gcb, bf, bb):
    b, lp, w = q.shape
    n = lp // CHUNK
    fwd = lambda i: (0, i, 0)
    bwd = lambda i: (0, n - 1 - i, 0)
    blk = (b, CHUNK, w)
    out = jax.ShapeDtypeStruct((b, lp, w), F32)
    return pl.pallas_call(
        _dscan_kernel,
        grid=(n,),
        in_specs=[pl.BlockSpec(blk, fwd)] * 5 + [pl.BlockSpec(blk, bwd)] * 5,
        out_specs=(pl.BlockSpec(blk, fwd), pl.BlockSpec(blk, bwd)),
        out_shape=(out, out),
        scratch_shapes=[pltpu.VMEM((b * 2 * N_HEADS_D // HEAD_GROUP, GROUP_W, GROUP_W), F32)],
        compiler_params=_cparams(("arbitrary",)),
        name="delta_scan",
    )(q, k, v, gcf, bf, q, k, v, gcb, bb)


def _post_kernel(h_ref, at_ref, of_ref, ob_ref, z_ref, ga_ref, gd_ref, on_ref, seg_ref,
                 wa_ref, wd_ref, wo_ref, gf_ref, wrh_ref, wrl_ref,
                 h_out, xn_out, aff_out, *, tm, n_tiles):
    od = of_ref[...] + ob_ref[...]
    msd = _split_dot(od * od, seg_ref[...]) * (1.0 / DV)
    z = z_ref[...].astype(F32)
    od = od * lax.rsqrt(msd + EPS) * on_ref[...] * (z * jax.nn.sigmoid(z))
    y_d = _dot(od.astype(BF16), wd_ref[...])
    y_a = _dot(at_ref[...], wa_ref[...])
    merged = (jax.nn.sigmoid(ga_ref[...].astype(F32)) * y_a +
              jax.nn.sigmoid(gd_ref[...].astype(F32)) * y_d)
    h = h_ref[...] + _dot(merged.astype(BF16), wo_ref[...])
    h_out[...] = h

    ms = jnp.mean(h * h, axis=-1, keepdims=True)
    xn = h * lax.rsqrt(ms + EPS) * gf_ref[...]
    xh = xn.astype(BF16)
    xn_out[...] = xh
    xl = (xn - xh.astype(F32)).astype(BF16)
    logits = _dot(xh, wrh_ref[...]) + _dot(xl, wrh_ref[...]) + _dot(xh, wrl_ref[...])
    lane = lax.broadcasted_iota(jnp.int32, (1, LANES), 1)
    logits = jnp.where(lane < N_EXPERTS, logits, -jnp.inf)
    e = jnp.exp(logits - jnp.max(logits, axis=-1, keepdims=True))
    aff = e / jnp.sum(e, axis=-1, keepdims=True)
    row = lax.broadcasted_iota(jnp.int32, (tm, 1), 0) + (pl.program_id(0) % n_tiles) * tm
    aff_out[...] = jnp.where(row >= N_NULL, aff, -1.0)


def _post(h2, at, of, ob, z, ga, gd, on, seg, wa, wd, wo, gf, wrh, wrl, lp):
    n = h2.shape[0]
    tm = ROW_TILE
    nt = lp // tm
    row = lambda i: (i, 0)
    const = lambda i: (0, 0)
    hw = N_HEADS_D * DV

    def wspec(shape):
        return pl.BlockSpec(shape, const, pipeline_mode=pl.Buffered(1))

    return pl.pallas_call(
        functools.partial(_post_kernel, tm=tm, n_tiles=nt),
        grid=(n // tm,),
        in_specs=[
            pl.BlockSpec((tm, D_MODEL), row),
            pl.BlockSpec((tm, hw), row),
            pl.BlockSpec((tm, hw), row),
            pl.BlockSpec((tm, hw), row),
            pl.BlockSpec((tm, hw), row),
            pl.BlockSpec((tm, D_MODEL), row),
            pl.BlockSpec((tm, D_MODEL), row),
            pl.BlockSpec((1, hw), const),
            wspec((hw, hw)),
            wspec((hw, D_MODEL)),
            wspec((hw, D_MODEL)),
            wspec((D_MODEL, D_MODEL)),
            pl.BlockSpec((1, D_MODEL), const),
            wspec((D_MODEL, LANES)),
            wspec((D_MODEL, LANES)),
        ],
        out_specs=(pl.BlockSpec((tm, D_MODEL), row), pl.BlockSpec((tm, D_MODEL), row),
                   pl.BlockSpec((tm, LANES), row)),
        out_shape=(jax.ShapeDtypeStruct((n, D_MODEL), F32), jax.ShapeDtypeStruct((n, D_MODEL), BF16),
                   jax.ShapeDtypeStruct((n, LANES), F32)),
        compiler_params=_cparams(("parallel",)),
        name="merge_out",
    )(h2, at, of, ob, z, ga, gd, on, seg, wa, wd, wo, gf, wrh, wrl)


def _expert_kernel(x_ref, gate_ref, wg_ref, wu_ref, wd_ref, y_ref):
    x = x_ref[0]
    a = _dot(x, wg_ref[0])
    b = _dot(x, wu_ref[0])
    hh = (a * jax.nn.sigmoid(a) * b).astype(BF16)
    y_ref[0] = _dot(hh, wd_ref[0]) * gate_ref[0]


def _experts(xe, gates, wg, wu, wd, tm):
    e, cp, d = xe.shape
    tile = lambda ei, i: (ei, i, 0)
    wsp = lambda ei, i: (ei, 0, 0)
    return pl.pallas_call(
        _expert_kernel,
        grid=(e, cp // tm),
        in_specs=[
            pl.BlockSpec((1, tm, d), tile),
            pl.BlockSpec((1, tm, 1), tile),
            pl.BlockSpec((1, d, D_EXPERT), wsp),
            pl.BlockSpec((1, d, D_EXPERT), wsp),
            pl.BlockSpec((1, D_EXPERT, d), wsp),
        ],
        out_specs=pl.BlockSpec((1, tm, d), tile),
        out_shape=jax.ShapeDtypeStruct((e, cp, d), F32),
        compiler_params=_cparams(("parallel", "arbitrary")),
        name="experts",
    )(xe, gates, wg, wu, wd)


ROUTE_CHUNK = 128
ROUTE_WINDOW = 2 * ROUTE_CHUNK


def _route_select_kernel(aff_ref, tri_ref, pos_ref, off_ref, *, cap):
    n = aff_ref.shape[0]
    rc = ROUTE_CHUNK
    n_chunks = n // rc

    def bits_of(c):
        return pltpu.bitcast(aff_ref[pl.ds(pl.multiple_of(c * rc, rc), rc), :], jnp.int32)

    def count(pred):
        def body(c, acc):
            return acc + jnp.sum(pred(bits_of(c)).astype(jnp.int32), axis=0, keepdims=True)
        return lax.fori_loop(0, n_chunks, body, jnp.zeros((1, LANES), jnp.int32))

    def search(i, thr):
        cand = thr | jnp.left_shift(jnp.int32(1), 30 - i)
        return jnp.where(count(lambda b: b >= cand) >= cap, cand, thr)

    thr = lax.fori_loop(0, 31, search, jnp.zeros((1, LANES), jnp.int32))
    need = (cap - count(lambda b: b > thr)).astype(F32)
    tri = tri_ref[...]

    def emit(c, carry):
        ties_before, picks_before = carry
        b = bits_of(c)
        tie = b == thr
        tie_incl = _dot(tri, jnp.where(tie, 1.0, 0.0).astype(BF16)) + ties_before
        pick = (b > thr) | (tie & (tie_incl - 1.0 < need))
        pick_incl = _dot(tri, jnp.where(pick, 1.0, 0.0).astype(BF16)) + picks_before
        pos_ref[pl.ds(pl.multiple_of(c * rc, rc), rc), :] = jnp.where(pick, pick_incl - 1.0, -1.0)
        off_ref[c] = picks_before.astype(jnp.int32)
        return tie_incl[rc - 1:rc], pick_incl[rc - 1:rc]

    zero = jnp.zeros((1, LANES), F32)
    lax.fori_loop(0, n_chunks, emit, (zero, zero))


def _route_compact_kernel(off_ref, pos_ref, aff_ref, idx_ref, gate_ref):
    c = pl.program_id(0)
    rc = ROUTE_CHUNK

    @pl.when(c == 0)
    def _():
        idx_ref[...] = jnp.zeros(idx_ref.shape, F32)
        gate_ref[...] = jnp.zeros(gate_ref.shape, F32)

    token = (lax.broadcasted_iota(jnp.int32, (rc, 1), 0) + c * rc).astype(F32)
    slot = lax.broadcasted_iota(jnp.int32, (1, ROUTE_WINDOW), 1).astype(F32)
    pos = pos_ref[...]
    aff = aff_ref[...]
    for e in range(N_EXPERTS):
        start = pl.multiple_of((off_ref[c * N_EXPERTS + e] // LANES) * LANES, LANES)
        hit = (pos[:, e:e + 1] - start.astype(F32)) == slot
        win = (slice(e, e + 1), pl.ds(start, ROUTE_WINDOW))
        idx_ref[win] += jnp.sum(jnp.where(hit, token, 0.0), axis=0, keepdims=True)
        gate_ref[win] += jnp.sum(jnp.where(hit, aff[:, e:e + 1], 0.0), axis=0, keepdims=True)


def _route(aff, cap, slots):
    n = aff.shape[0]
    rc = ROUTE_CHUNK
    n_chunks = n // rc
    tri = jnp.asarray(np.tril(np.ones((rc, rc), np.float32)), BF16)
    whole = pl.BlockSpec(memory_space=pltpu.VMEM)
    pos, off = pl.pallas_call(
        functools.partial(_route_select_kernel, cap=cap),
        in_specs=[whole, whole],
        out_specs=(whole, whole),
        out_shape=(jax.ShapeDtypeStruct((n, LANES), F32),
                   jax.ShapeDtypeStruct((n_chunks, 1, LANES), jnp.int32)),
        compiler_params=pltpu.CompilerParams(vmem_limit_bytes=VMEM_LIMIT),
        name="route_select",
    )(aff, tri)
    off = off[:, 0, :N_EXPERTS].reshape(-1)
    chunk = lambda c, off_ref: (c, 0)
    fixed = lambda c, off_ref: (0, 0)
    out = jax.ShapeDtypeStruct((N_EXPERTS, slots), F32)
    idx, gates = pl.pallas_call(
        _route_compact_kernel,
        grid_spec=pltpu.PrefetchScalarGridSpec(
            num_scalar_prefetch=1,
            grid=(n_chunks,),
            in_specs=[pl.BlockSpec((rc, LANES), chunk), pl.BlockSpec((rc, LANES), chunk)],
            out_specs=(pl.BlockSpec((N_EXPERTS, slots), fixed), pl.BlockSpec((N_EXPERTS, slots), fixed)),
        ),
        out_shape=(out, out),
        compiler_params=_cparams(("arbitrary",)),
        name="route_compact",
    )(off, pos, aff)
    return idx.astype(jnp.int32), gates


def _final_kernel(x_ref, g_ref, o_ref):
    x = x_ref[0]
    ms = jnp.mean(x * x, axis=-1, keepdims=True)
    o_ref[0] = x * lax.rsqrt(ms + EPS) * g_ref[...]


def _final_norm(h3, g, s):
    b = h3.shape[0]
    tm = FRONT
    return pl.pallas_call(
        _final_kernel,
        grid=(b, s // tm),
        in_specs=[pl.BlockSpec((1, tm, D_MODEL), lambda bi, i: (bi, i + FRONT // tm, 0)),
                  pl.BlockSpec((1, D_MODEL), lambda bi, i: (0, 0))],
        out_specs=pl.BlockSpec((1, tm, D_MODEL), lambda bi, i: (bi, i, 0)),
        out_shape=jax.ShapeDtypeStruct((b, s, D_MODEL), F32),
        compiler_params=_cparams(("parallel", "parallel")),
        name="final_norm",
    )(h3, g)


def _rope_tables(s):
    lp = FRONT + s
    t = np.arange(lp) - FRONT
    real = t >= 0
    pos = np.stack([np.where(real, t // GRID_W, 0), np.where(real, t % GRID_W, 0)], axis=-1)
    n_freq = HEAD_DIM // 4
    inv_freq = jnp.asarray(ROPE_THETA, F32) ** (-jnp.arange(n_freq, dtype=F32) / n_freq)
    lane = np.arange(LANES) % HEAD_DIM
    axis = lane // (HEAD_DIM // 2)
    freq = lane % n_freq
    ang = jnp.asarray(pos, F32)[:, axis] * inv_freq[freq][None, :]
    sign = np.where((lane % (HEAD_DIM // 2)) < n_freq, -1.0, 1.0).astype(np.float32)
    return jnp.cos(ang), jnp.sin(ang) * sign[None, :]


def _pack_w_in(w):
    o = 0

    def take(n):
        nonlocal o
        part = w[:, o:o + n]
        o += n
        return part

    def head_blocks(part, n_heads):
        part = part.reshape(D_MODEL, n_heads, HEAD_DIM)
        part = jnp.pad(part, ((0, 0), (0, 0), (0, LANES - HEAD_DIM)))
        return part.reshape(D_MODEL, n_heads * LANES)

    q_a = head_blocks(take(N_HEADS_A * HEAD_DIM), N_HEADS_A)
    k_a = head_blocks(take(N_KV_HEADS * HEAD_DIM), N_KV_HEADS)
    v_a = take(N_KV_HEADS * HEAD_DIM).reshape(D_MODEL, N_KV_HEADS, HEAD_DIM)
    v_a = jnp.pad(v_a, ((0, 0), (0, 0), (0, V_ROWS - HEAD_DIM))).reshape(D_MODEL, N_KV_HEADS * V_ROWS)
    qkv_d = take(3 * N_HEADS_D * DK)
    z_d = take(N_HEADS_D * DV)
    small = take(4 * N_HEADS_D)
    gate_a = take(D_MODEL)
    gate_d = take(D_MODEL)
    small = jnp.pad(small, ((0, 0), (0, LANES - small.shape[1])))
    w_all = jnp.concatenate([q_a, k_a, qkv_d, z_d, small, gate_a, gate_d], axis=1).astype(BF16)
    return w_all, v_a.T.astype(BF16)


def _chunk_tri(tm, reverse):
    i = np.arange(tm)
    same = (i[:, None] // CHUNK) == (i[None, :] // CHUNK)
    tri = (i[None, :] >= i[:, None]) if reverse else (i[None, :] <= i[:, None])
    return jnp.asarray((same & tri).astype(np.float32))


def _expert_tile(cap):
    n_tiles = -(-cap // 512)
    tm = -(-cap // n_tiles)
    tm = -(-tm // 16) * 16
    return tm, n_tiles


def _trunk(x, meta_tokens, layers, norm_final):
    b, s, d = x.shape
    lp = FRONT + s
    n = b * lp
    n_tok = b * (N_META + s)
    cap = EC_CAPACITY * n_tok // N_EXPERTS
    etm, ent = _expert_tile(cap)
    cap_pad = etm * ent
    slots = -(-max(cap_pad, cap + ROUTE_WINDOW) // LANES) * LANES
    cos, sin = _rope_tables(s)
    seg = jnp.asarray(np.kron(np.eye(N_HEADS_D), np.ones((DK, DK))), BF16)
    trif = _chunk_tri(ROW_TILE, False)
    trib = _chunk_tri(ROW_TILE, True)

    front = jnp.concatenate([jnp.zeros((N_NULL, d), F32), meta_tokens.astype(F32)], axis=0)
    h = jnp.concatenate([jnp.broadcast_to(front[None], (b, FRONT, d)), x], axis=1).reshape(n, d)

    for lw in layers:
        q, k, vt, qkvd, z, sm, ga, gd = _inproj(h, lw["norm_mix"], lw["w_all"], lw["wvt"], cos, sin,
                                                lw["qn"], lw["kn"], lp)
        at = _attention(q.reshape(b, lp, -1), k.reshape(b, lp, -1), vt, s)
        qd, kd, vd, gcf, gcb, bf, bb = _delta_prep(qkvd.reshape(b, lp, -1), sm.reshape(b, lp, -1),
                                                   lw["conv_w"], lw["rate"], lw["dtb"], seg, trif, trib)
        of, ob = _delta_scan(qd, kd, vd, gcf, gcb, bf, bb)
        hw = N_HEADS_D * DV
        h, xn, aff = _post(h, at.reshape(n, hw), of.reshape(n, hw), ob.reshape(n, hw), z, ga, gd,
                           lw["on"], seg, lw["w_attn_proj"], lw["w_delta_proj"], lw["w_out"],
                           lw["norm_ffn"], lw["wr_hi"], lw["wr_lo"], lp)
        idx, gates = _route(aff, cap, slots)
        idx = idx[:, :cap_pad]
        gates = gates[:, :cap_pad]
        xe = jnp.take(xn, idx.reshape(-1), axis=0).reshape(N_EXPERTS, cap_pad, d)
        ye = _experts(xe, gates[..., None], lw["w_gate"], lw["w_up"], lw["w_down"], etm)
        h = h.at[idx.reshape(-1)].add(ye.reshape(-1, d))

    return _final_norm(h.reshape(b, lp, d), norm_final, s)


def kernel(x_prompt, x_sample, meta_tokens, norm_mix, w_in, q_norm, k_norm, conv_w, a_log, dt_bias,
           o_norm, w_attn_proj, w_delta_proj, w_out, norm_ffn, w_router, w_gate, w_up, w_down,
           norm_final):
    depth = w_in.shape[0]
    layers = []
    for l in range(depth):
        rate = jnp.exp(a_log[l].astype(F32)).reshape(1, -1)
        dtb = dt_bias[l].astype(F32).reshape(1, -1)
        pad = LANES - rate.shape[1]
        wr = jnp.pad(w_router[l].astype(F32), ((0, 0), (0, LANES - N_EXPERTS)))
        wr_hi = wr.astype(BF16)
        w_all, wvt = _pack_w_in(w_in[l])
        layers.append(dict(
            norm_mix=norm_mix[l].astype(F32).reshape(1, -1),
            w_all=w_all,
            wvt=wvt,
            qn=jnp.tile(q_norm[l].astype(F32), LANES // HEAD_DIM).reshape(1, -1),
            kn=jnp.tile(k_norm[l].astype(F32), LANES // HEAD_DIM).reshape(1, -1),
            conv_w=conv_w[l].astype(F32),
            rate=jnp.pad(rate, ((0, 0), (0, pad))),
            dtb=jnp.pad(dtb, ((0, 0), (0, pad))),
            on=jnp.tile(o_norm[l].astype(F32), N_HEADS_D).reshape(1, -1),
            w_attn_proj=w_attn_proj[l].astype(BF16),
            w_delta_proj=w_delta_proj[l].astype(BF16),
            w_out=w_out[l].astype(BF16),
            norm_ffn=norm_ffn[l].astype(F32).reshape(1, -1),
            wr_hi=wr_hi,
            wr_lo=(wr - wr_hi.astype(F32)).astype(BF16),
            w_gate=w_gate[l].astype(BF16),
            w_up=w_up[l].astype(BF16),
            w_down=w_down[l].astype(BF16),
        ))
    nf = norm_final.astype(F32).reshape(1, -1)
    y_prompt = _trunk(x_prompt, meta_tokens, layers, nf)
    y_sample = _trunk(x_sample, meta_tokens, layers, nf)
    return (y_prompt, y_sample)
```

```python
import functools

import numpy as np
import jax
import jax.numpy as jnp
from jax import lax
from jax.experimental import pallas as pl
from jax.experimental.pallas import tpu as pltpu

F32 = jnp.float32
BF16 = jnp.bfloat16

D_MODEL = 1024
N_META = 16
GRID_W = 64
N_HEADS_A = 8
N_KV_HEADS = 2
HEAD_DIM = 64
ROPE_THETA = 10000.0
N_HEADS_D = 8
DK = 64
DV = 64
CHUNK = 64
N_EXPERTS = 16
EC_CAPACITY = 2
D_EXPERT = 1024
EPS = 1e-6

FRONT = 128
N_NULL = FRONT - N_META
LANES = 128
ROW_TILE = 384
HEAD_GROUP = 4
GROUP_W = HEAD_GROUP * DK
VMEM_LIMIT = 48 * 1024 * 1024
Q_SCALE = HEAD_DIM ** -0.5 * float(np.log2(np.e))
PAIRS_PER_TRIP = 3
LANE_SLICES = 2
V_ROWS = HEAD_DIM + 16

C_Q = 0
C_K = C_Q + N_HEADS_A * LANES
C_QKVD = C_K + N_KV_HEADS * LANES
C_Z = C_QKVD + 3 * N_HEADS_D * DK
C_SM = C_Z + N_HEADS_D * DV
C_GA = C_SM + LANES
C_GD = C_GA + D_MODEL
C_END = C_GD + D_MODEL


def _cparams(sem):
    return pltpu.CompilerParams(dimension_semantics=sem, vmem_limit_bytes=VMEM_LIMIT)


def _dot(a, b):
    return jnp.dot(a, b, preferred_element_type=F32)


def _dot_nt(a, b):
    return lax.dot_general(a, b, (((1,), (1,)), ((), ())), preferred_element_type=F32)


def _split_dot(t, w_bf16):
    hi = t.astype(BF16)
    lo = (t - hi.astype(F32)).astype(BF16)
    return _dot(hi, w_bf16) + _dot(lo, w_bf16)


def _inproj_kernel(x_ref, g_ref, w_ref, wvt_ref, cos_ref, sin_ref, qn_ref, kn_ref,
                   q_ref, k_ref, vt_ref, qkvd_ref, z_ref, sm_ref, ga_ref, gd_ref):
    x = x_ref[...]
    ms = jnp.mean(x * x, axis=-1, keepdims=True)
    u = (x * lax.rsqrt(ms + EPS) * g_ref[...]).astype(BF16)
    cos = cos_ref[...]
    sin = sin_ref[...]
    lane = lax.broadcasted_iota(jnp.int32, (1, LANES), 1)
    first = (lane % 32) < 16

    def rope(t):
        rot = jnp.where(first, pltpu.roll(t, LANES - 16, 1), pltpu.roll(t, 16, 1))
        return t * cos + rot * sin

    def mm(c0, n):
        return _dot(u, w_ref[:, c0:c0 + n])

    def head(t, gain):
        msq = jnp.sum(t * t, axis=-1, keepdims=True) * (1.0 / HEAD_DIM)
        return rope(t * lax.rsqrt(msq + EPS) * gain)

    for hp in range(N_HEADS_A // 2):
        t2 = mm(C_Q + 2 * LANES * hp, 2 * LANES)
        for i in range(2):
            h = 2 * hp + i
            q_ref[:, LANES * h:LANES * (h + 1)] = (
                head(t2[:, LANES * i:LANES * (i + 1)], qn_ref[...]) * Q_SCALE).astype(BF16)
    t2 = mm(C_K, N_KV_HEADS * LANES)
    for j in range(N_KV_HEADS):
        k_ref[:, LANES * j:LANES * (j + 1)] = head(t2[:, LANES * j:LANES * (j + 1)], kn_ref[...]).astype(BF16)

    vt = _dot_nt(wvt_ref[...], u)
    vrow = lax.broadcasted_iota(jnp.int32, (N_KV_HEADS * V_ROWS, 1), 0) % V_ROWS
    vt_ref[...] = jnp.where(vrow < HEAD_DIM, vt, 1.0).astype(BF16)
    qkvd_ref[...] = mm(C_QKVD, C_Z - C_QKVD)
    z_ref[...] = mm(C_Z, C_SM - C_Z).astype(BF16)
    sm_ref[...] = mm(C_SM, LANES)
    ga_ref[...] = mm(C_GA, D_MODEL).astype(BF16)
    gd_ref[...] = mm(C_GD, D_MODEL).astype(BF16)


def _inproj(h2, g, w_all, wvt, cos, sin, qn, kn, lp):
    n = h2.shape[0]
    tm = ROW_TILE
    nt = lp // tm
    row = lambda i: (i, 0)
    const = lambda i: (0, 0)
    tab = lambda i: (i % nt, 0)
    out_shapes = (
        jax.ShapeDtypeStruct((n, N_HEADS_A * LANES), BF16),
        jax.ShapeDtypeStruct((n, N_KV_HEADS * LANES), BF16),
        jax.ShapeDtypeStruct((N_KV_HEADS * V_ROWS, n), BF16),
        jax.ShapeDtypeStruct((n, C_Z - C_QKVD), F32),
        jax.ShapeDtypeStruct((n, C_SM - C_Z), BF16),
        jax.ShapeDtypeStruct((n, LANES), F32),
        jax.ShapeDtypeStruct((n, D_MODEL), BF16),
        jax.ShapeDtypeStruct((n, D_MODEL), BF16),
    )
    out_specs = [pl.BlockSpec((tm, s.shape[1]), row) for s in out_shapes]
    out_specs[2] = pl.BlockSpec((N_KV_HEADS * V_ROWS, tm), lambda i: (0, i))
    return pl.pallas_call(
        _inproj_kernel,
        grid=(n // tm,),
        in_specs=[
            pl.BlockSpec((tm, D_MODEL), row),
            pl.BlockSpec((1, D_MODEL), const),
            pl.BlockSpec((D_MODEL, C_END), const, pipeline_mode=pl.Buffered(1)),
            pl.BlockSpec((N_KV_HEADS * V_ROWS, D_MODEL), const, pipeline_mode=pl.Buffered(1)),
            pl.BlockSpec((tm, LANES), tab),
            pl.BlockSpec((tm, LANES), tab),
            pl.BlockSpec((1, LANES), const),
            pl.BlockSpec((1, LANES), const),
        ],
        out_specs=tuple(out_specs),
        out_shape=out_shapes,
        compiler_params=_cparams(("parallel",)),
        name="inproj",
    )(h2, g, w_all, wvt, cos, sin, qn, kn)


def _flash_kernel(q_ref, k_ref, vt_ref, o_ref, qs_ref, m_ref, acc_ref, sa_ref, sb_ref, ca_ref, cb_ref,
                  *, tq, tk, s_real):
    g = N_HEADS_A // N_KV_HEADS
    r = g * tq
    for i in range(g):
        qs_ref[i * tq:(i + 1) * tq, :] = q_ref[0, :, LANES * i:LANES * (i + 1)]
    m_ref[...] = jnp.full(m_ref.shape, -jnp.inf, F32)
    acc_ref[...] = jnp.zeros(acc_ref.shape, F32)

    def scores(koff, size):
        return _dot_nt(k_ref[0, pl.ds(koff, size), :], qs_ref[...])

    def absorb(s, cmax, koff, size):
        m_prev = m_ref[...]
        m_new = jnp.maximum(m_prev, cmax)
        alpha = jnp.exp2(m_prev - m_new)
        p = jnp.exp2(s - m_new).astype(BF16)
        acc_ref[...] = alpha * acc_ref[...] + _dot(vt_ref[:, pl.ds(koff, size)], p)
        m_ref[...] = m_new

    def produce(koff, s_ref, c_ref):
        s = scores(koff, tk)
        s_ref[...] = s
        c_ref[...] = jnp.max(s, axis=0, keepdims=True)

    def main_off(c):
        off = FRONT + c * tk
        return off if isinstance(c, int) else pl.multiple_of(off, LANES)

    s = scores(0, FRONT)
    s = jnp.where(lax.broadcasted_iota(jnp.int32, (FRONT, 1), 0) >= N_NULL, s, -jnp.inf)

    n_main = s_real // tk
    if n_main:
        produce(main_off(0), sa_ref, ca_ref)
    absorb(s, jnp.max(s, axis=0, keepdims=True), 0, FRONT)
    if n_main:
        n_pairs = (n_main - 1) // 2

        hw = r // LANE_SLICES

        def step(koff_next, nxt, koff_cur, cur):
            (sn_ref, cn_ref), (sc_ref, cc_ref) = nxt, cur
            for hh in range(LANE_SLICES):
                sl = slice(hh * hw, (hh + 1) * hw)
                s_new = _dot_nt(k_ref[0, pl.ds(koff_next, tk), :], qs_ref[sl, :])
                sn_ref[:, sl] = s_new
                cn_ref[:, sl] = jnp.max(s_new, axis=0, keepdims=True)
                m_prev = m_ref[:, sl]
                m_new = jnp.maximum(m_prev, cc_ref[:, sl])
                alpha = jnp.exp2(m_prev - m_new)
                p = jnp.exp2(sc_ref[:, sl] - m_new).astype(BF16)
                acc_ref[:, sl] = alpha * acc_ref[:, sl] + _dot(vt_ref[:, pl.ds(koff_cur, tk)], p)
                m_ref[:, sl] = m_new

        def pair(pi):
            buf_a, buf_b = (sa_ref, ca_ref), (sb_ref, cb_ref)
            step(main_off(2 * pi + 1), buf_b, main_off(2 * pi), buf_a)
            step(main_off(2 * pi + 2), buf_a, main_off(2 * pi + 1), buf_b)

        def body(t, carry):
            for u in range(PAIRS_PER_TRIP):
                pair(t * PAIRS_PER_TRIP + u)
            return carry

        lax.fori_loop(0, n_pairs // PAIRS_PER_TRIP, body, 0)
        for pi in range(n_pairs - n_pairs % PAIRS_PER_TRIP, n_pairs):
            pair(pi)
        if n_main - 2 * n_pairs == 2:
            step(main_off(2 * n_pairs + 1), (sb_ref, cb_ref), main_off(2 * n_pairs), (sa_ref, ca_ref))
            absorb(sb_ref[...], cb_ref[...], main_off(2 * n_pairs + 1), tk)
        else:
            absorb(sa_ref[...], ca_ref[...], main_off(2 * n_pairs), tk)
    if s_real % tk:
        s = scores(FRONT + n_main * tk, s_real % tk)
        absorb(s, jnp.max(s, axis=0, keepdims=True), FRONT + n_main * tk, s_real % tk)

    acc = acc_ref[...]
    out_t = acc[:HEAD_DIM] / acc[HEAD_DIM:HEAD_DIM + 1]
    for p in range(g // 2):
        pair = jnp.concatenate([out_t[:, (2 * p) * tq:(2 * p + 1) * tq],
                                out_t[:, (2 * p + 1) * tq:(2 * p + 2) * tq]], axis=0)
        o_ref[0, :, 2 * HEAD_DIM * p:2 * HEAD_DIM * (p + 1)] = pair.T.astype(BF16)


def _attention(q, k, vt, s_real):
    b, lp, _ = q.shape
    tq = ROW_TILE
    tk = 512
    g = N_HEADS_A // N_KV_HEADS
    return pl.pallas_call(
        functools.partial(_flash_kernel, tq=tq, tk=tk, s_real=s_real),
        grid=(b, N_KV_HEADS, lp // tq),
        in_specs=[
            pl.BlockSpec((1, tq, g * LANES), lambda bi, j, qi: (bi, qi, j)),
            pl.BlockSpec((1, lp, LANES), lambda bi, j, qi: (bi, 0, j)),
            pl.BlockSpec((V_ROWS, lp), lambda bi, j, qi: (j, bi)),
        ],
        out_specs=pl.BlockSpec((1, tq, g * HEAD_DIM), lambda bi, j, qi: (bi, qi, j)),
        out_shape=jax.ShapeDtypeStruct((b, lp, N_HEADS_A * HEAD_DIM), BF16),
        scratch_shapes=[
            pltpu.VMEM((g * tq, LANES), BF16),
            pltpu.VMEM((1, g * tq), F32),
            pltpu.VMEM((V_ROWS, g * tq), F32),
            pltpu.VMEM((tk, g * tq), F32),
            pltpu.VMEM((tk, g * tq), F32),
            pltpu.VMEM((1, g * tq), F32),
            pltpu.VMEM((1, g * tq), F32),
        ],
        compiler_params=_cparams(("parallel", "parallel", "arbitrary")),
        name="attention",
    )(q, k, vt)


def _dprep_kernel(prev_ref, cur_ref, next_ref, sm_ref, cw_ref, rate_ref, dtb_ref, seg_ref,
                  trif_ref, trib_ref, q_ref, k_ref, v_ref, gcf_ref, gcb_ref, bf_ref, bb_ref,
                  *, tm, n_tiles):
    i = pl.program_id(1)
    row = lax.broadcasted_iota(jnp.int32, (tm, 1), 0)
    valid = (row + i * tm) >= N_NULL
    x = jnp.where(valid, cur_ref[0], 0.0)
    prow = jnp.where(i > 0, prev_ref[0, 7:8, :], 0.0)
    nrow = jnp.where(i < n_tiles - 1, next_ref[0, 0:1, :], 0.0)
    xp = jnp.where(row == 0, prow, pltpu.roll(x, 1, 0))
    xn = jnp.where(row == tm - 1, nrow, pltpu.roll(x, tm - 1, 0))
    cw = cw_ref[...]
    y = cw[0:1] * xp + cw[1:2] * x + cw[2:3] * xn
    y = y * jax.nn.sigmoid(y)

    w = N_HEADS_D * DK
    seg = seg_ref[...]
    q = y[:, 0:w]
    q = q * lax.rsqrt(_split_dot(q * q, seg) + EPS) * (DK ** -0.5)
    k = y[:, w:2 * w]
    k = k * lax.rsqrt(_split_dot(k * k, seg) + EPS)
    q_ref[0] = jnp.where(valid, q, 0.0)
    k_ref[0] = jnp.where(valid, k, 0.0)
    v_ref[0] = jnp.where(valid, y[:, 2 * w:3 * w], 0.0)

    sm = sm_ref[0]
    t = sm + dtb_ref[...]
    softplus = jnp.maximum(t, 0.0) + jnp.log1p(jnp.exp(-jnp.abs(t)))
    g_all = jnp.where(valid, -rate_ref[...] * softplus, 0.0)
    beta_all = jnp.where(valid, jax.nn.sigmoid(sm), 0.0)
    gc_f = jnp.dot(trif_ref[...], g_all, preferred_element_type=F32, precision=lax.Precision.HIGHEST)
    gc_b = jnp.dot(trib_ref[...], g_all, preferred_element_type=F32, precision=lax.Precision.HIGHEST)

    lane = lax.broadcasted_iota(jnp.int32, (1, LANES), 1)
    low = lane < DK

    def expand(a, c0, out_ref):
        for p in range(N_HEADS_D // 2):
            e0 = jnp.broadcast_to(a[:, c0 + 2 * p:c0 + 2 * p + 1], (tm, LANES))
            e1 = jnp.broadcast_to(a[:, c0 + 2 * p + 1:c0 + 2 * p + 2], (tm, LANES))
            out_ref[0, :, LANES * p:LANES * (p + 1)] = jnp.where(low, e0, e1)

    expand(gc_f, 0, gcf_ref)
    expand(gc_b, N_HEADS_D, gcb_ref)
    expand(beta_all, 2 * N_HEADS_D, bf_ref)
    expand(beta_all, 3 * N_HEADS_D, bb_ref)


def _delta_prep(qkvd, sm, cw, rate, dtb, seg, trif, trib):
    b, lp, c = qkvd.shape
    tm = ROW_TILE
    nt = lp // tm
    w = N_HEADS_D * DK
    halo = 8
    nb8 = lp // halo
    cur = lambda bi, i: (bi, i, 0)
    const = lambda bi, i: (0, 0)
    out = jax.ShapeDtypeStruct((b, lp, w), F32)
    return pl.pallas_call(
        functools.partial(_dprep_kernel, tm=tm, n_tiles=nt),
        grid=(b, nt),
        in_specs=[
            pl.BlockSpec((1, halo, c), lambda bi, i: (bi, jnp.maximum(i * (tm // halo) - 1, 0), 0)),
            pl.BlockSpec((1, tm, c), cur),
            pl.BlockSpec((1, halo, c), lambda bi, i: (bi, jnp.minimum((i + 1) * (tm // halo), nb8 - 1), 0)),
            pl.BlockSpec((1, tm, LANES), cur),
            pl.BlockSpec((3, c), const),
            pl.BlockSpec((1, LANES), const),
            pl.BlockSpec((1, LANES), const),
            pl.BlockSpec((w, w), const),
            pl.BlockSpec((tm, tm), const),
            pl.BlockSpec((tm, tm), const),
        ],
        out_specs=tuple(pl.BlockSpec((1, tm, w), cur) for _ in range(7)),
        out_shape=(out,) * 7,
        compiler_params=_cparams(("parallel", "parallel")),
        name="delta_prep",
    )(qkvd, qkvd, qkvd, sm, cw, rate, dtb, seg, trif, trib)


def _dscan_kernel(qf_ref, kf_ref, vf_ref, gf_ref, bf_ref, qb_ref, kb_ref, vb_ref, gb_ref, bb_ref,
                  of_ref, ob_ref, s_ref):
    i = pl.program_id(0)

    @pl.when(i == 0)
    def _():
        s_ref[...] = jnp.zeros(s_ref.shape, F32)

    c = CHUNK
    r = lax.broadcasted_iota(jnp.int32, (c, GROUP_W), 0)
    cj = lax.broadcasted_iota(jnp.int32, (c, GROUP_W), 1) % c
    eye = r == cj
    eyef = eye.astype(F32)
    bdmask = (lax.broadcasted_iota(jnp.int32, (GROUP_W, GROUP_W), 0) // c ==
              lax.broadcasted_iota(jnp.int32, (GROUP_W, GROUP_W), 1) // c)

    def bd(t):
        return jnp.where(bdmask, jnp.concatenate([t] * HEAD_GROUP, axis=0), 0.0).astype(BF16)

    n_grp = N_HEADS_D // HEAD_GROUP
    fwd_refs = (qf_ref, kf_ref, vf_ref, gf_ref, bf_ref, of_ref)
    bwd_refs = (qb_ref, kb_ref, vb_ref, gb_ref, bb_ref, ob_ref)
    chains = []
    for bi in range(of_ref.shape[0]):
        for grp in range(n_grp):
            for reverse in (False, True):
                sl = slice(GROUP_W * grp, GROUP_W * (grp + 1))
                chains.append((fwd_refs if not reverse else bwd_refs, bi, sl, reverse, len(chains)))

    def each(fn, *cols):
        return [fn(*args) for args in zip(*cols)]

    rev = [ch[3] for ch in chains]
    q = [ch[0][0][ch[1], :, ch[2]] for ch in chains]
    k = [ch[0][1][ch[1], :, ch[2]] for ch in chains]
    v = [ch[0][2][ch[1], :, ch[2]] for ch in chains]
    gc = [ch[0][3][ch[1], :, ch[2]] for ch in chains]
    beta = [ch[0][4][ch[1], :, ch[2]] for ch in chains]
    state = [s_ref[ch[4]] for ch in chains]

    def decay_of(g, reverse):
        incl = (r <= cj) if reverse else (r >= cj)
        gcol = jnp.sum(jnp.where(eye, g, 0.0), axis=0, keepdims=True)
        return jnp.where(incl, jnp.exp(jnp.where(incl, g - gcol, 0.0)), 0.0)

    decay = each(decay_of, gc, rev)
    eg = each(jnp.exp, gc)
    glast = each(lambda g, reverse: g[0:1] if reverse else g[c - 1:c], gc, rev)
    kbeta = each(lambda a, b: a * b, k, beta)
    gq = each(lambda kb_, q_, k_: _dot_nt(jnp.concatenate([kb_, q_], axis=0).astype(BF16), bd(k_)),
              kbeta, q, k)
    a_intra = each(lambda g, d: g[c:] * d, gq, decay)
    m = each(lambda g, d, reverse: -jnp.where((r < cj) if reverse else (r > cj), g[:c] * d, 0.0),
             gq, decay, rev)
    p = each(lambda m_: eyef + m_, m)
    m = each(lambda m_: _dot(m_.astype(BF16), bd(m_)), m)
    for _ in range(4):
        rr = each(lambda m_, p_: _dot(jnp.concatenate([m_, p_], axis=0).astype(BF16), bd(m_)), m, p)
        p = each(lambda p_, rr_: p_ + rr_[c:], p, rr)
        m = each(lambda rr_: rr_[:c], rr)
    tmat = each(lambda p_, m_: (p_ + _dot(p_.astype(BF16), bd(m_))).astype(BF16), p, m)
    u = each(lambda t, v_, b: _dot(t, bd(v_ * b)), tmat, v, beta)
    w = each(lambda t, kb_, e: _dot(t, bd(kb_ * e)), tmat, kbeta, eg)
    wq = each(lambda w_, q_, e, s: _dot(jnp.concatenate([w_, q_ * e], axis=0).astype(BF16), s.astype(BF16)),
              w, q, eg, state)
    v_new = each(lambda u_, wq_: u_ - wq_[:c], u, wq)
    o = each(lambda wq_, a, vn: wq_[c:] + _dot(a.astype(BF16), bd(vn)), wq, a_intra, v_new)
    upd = each(lambda k_, gl, g, vn: _dot((k_ * jnp.exp(gl - g)).T.astype(BF16), vn.astype(BF16)),
               k, glast, gc, v_new)
    for ch, o_, s, gl, up in zip(chains, o, state, glast, upd):
        ch[0][5][ch[1], :, ch[2]] = o_
        s_ref[ch[4]] = s * jnp.exp(gl) + jnp.where(bdmask, up, 0.0)


def _delta_scan(q, k, v, gcf, gcb, bf, bb):
    b, lp, w = q.shape
    n = lp // CHUNK
    fwd = lambda i: (0, i, 0)
    bwd = lambda i: (0, n - 1 - i, 0)
    blk = (b, CHUNK, w)
    out = jax.ShapeDtypeStruct((b, lp, w), F32)
    return pl.pallas_call(
        _dscan_kernel,
        grid=(n,),
        in_specs=[pl.BlockSpec(blk, fwd)] * 5 + [pl.BlockSpec(blk, bwd)] * 5,
        out_specs=(pl.BlockSpec(blk, fwd), pl.BlockSpec(blk, bwd)),
        out_shape=(out, out),
        scratch_shapes=[pltpu.VMEM((b * 2 * N_HEADS_D // HEAD_GROUP, GROUP_W, GROUP_W), F32)],
        compiler_params=_cparams(("arbitrary",)),
        name="delta_scan",
    )(q, k, v, gcf, bf, q, k, v, gcb, bb)


def _post_kernel(h_ref, at_ref, of_ref, ob_ref, z_ref, ga_ref, gd_ref, on_ref, seg_ref,
                 wa_ref, wd_ref, wo_ref, gf_ref, wrh_ref, wrl_ref,
                 h_out, xn_out, aff_out, *, tm, n_tiles):
    od = of_ref[...] + ob_ref[...]
    msd = _split_dot(od * od, seg_ref[...]) * (1.0 / DV)
    z = z_ref[...].astype(F32)
    od = od * lax.rsqrt(msd + EPS) * on_ref[...] * (z * jax.nn.sigmoid(z))
    y_d = _dot(od.astype(BF16), wd_ref[...])
    y_a = _dot(at_ref[...], wa_ref[...])
    merged = (jax.nn.sigmoid(ga_ref[...].astype(F32)) * y_a +
              jax.nn.sigmoid(gd_ref[...].astype(F32)) * y_d)
    h = h_ref[...] + _dot(merged.astype(BF16), wo_ref[...])
    h_out[...] = h

    ms = jnp.mean(h * h, axis=-1, keepdims=True)
    xn = h * lax.rsqrt(ms + EPS) * gf_ref[...]
    xh = xn.astype(BF16)
    xn_out[...] = xh
    xl = (xn - xh.astype(F32)).astype(BF16)
    logits = _dot(xh, wrh_ref[...]) + _dot(xl, wrh_ref[...]) + _dot(xh, wrl_ref[...])
    lane = lax.broadcasted_iota(jnp.int32, (1, LANES), 1)
    logits = jnp.where(lane < N_EXPERTS, logits, -jnp.inf)
    e = jnp.exp(logits - jnp.max(logits, axis=-1, keepdims=True))
    aff = e / jnp.sum(e, axis=-1, keepdims=True)
    row = lax.broadcasted_iota(jnp.int32, (tm, 1), 0) + (pl.program_id(0) % n_tiles) * tm
    aff_out[...] = jnp.where(row >= N_NULL, aff, -1.0)


def _post(h2, at, of, ob, z, ga, gd, on, seg, wa, wd, wo, gf, wrh, wrl, lp):
    n = h2.shape[0]
    tm = ROW_TILE
    nt = lp // tm
    row = lambda i: (i, 0)
    const = lambda i: (0, 0)
    hw = N_HEADS_D * DV

    def wspec(shape):
        return pl.BlockSpec(shape, const, pipeline_mode=pl.Buffered(1))

    return pl.pallas_call(
        functools.partial(_post_kernel, tm=tm, n_tiles=nt),
        grid=(n // tm,),
        in_specs=[
            pl.BlockSpec((tm, D_MODEL), row),
            pl.BlockSpec((tm, hw), row),
            pl.BlockSpec((tm, hw), row),
            pl.BlockSpec((tm, hw), row),
            pl.BlockSpec((tm, hw), row),
            pl.BlockSpec((tm, D_MODEL), row),
            pl.BlockSpec((tm, D_MODEL), row),
            pl.BlockSpec((1, hw), const),
            wspec((hw, hw)),
            wspec((hw, D_MODEL)),
            wspec((hw, D_MODEL)),
            wspec((D_MODEL, D_MODEL)),
            pl.BlockSpec((1, D_MODEL), const),
            wspec((D_MODEL, LANES)),
            wspec((D_MODEL, LANES)),
        ],
        out_specs=(pl.BlockSpec((tm, D_MODEL), row), pl.BlockSpec((tm, D_MODEL), row),
                   pl.BlockSpec((tm, LANES), row)),
        out_shape=(jax.ShapeDtypeStruct((n, D_MODEL), F32), jax.ShapeDtypeStruct((n, D_MODEL), BF16),
                   jax.ShapeDtypeStruct((n, LANES), F32)),
        compiler_params=_cparams(("parallel",)),
        name="merge_out",
    )(h2, at, of, ob, z, ga, gd, on, seg, wa, wd, wo, gf, wrh, wrl)


def _expert_kernel(x_ref, gate_ref, wg_ref, wu_ref, wd_ref, y_ref, wgb_ref, wub_ref, wdb_ref):
    @pl.when(pl.program_id(1) == 0)
    def _():
        wgb_ref[...] = wg_ref[0].astype(BF16)
        wub_ref[...] = wu_ref[0].astype(BF16)
        wdb_ref[...] = wd_ref[0].astype(BF16)

    x = x_ref[0]
    a = _dot(x, wgb_ref[...])
    b = _dot(x, wub_ref[...])
    hh = (a * jax.nn.sigmoid(a) * b).astype(BF16)
    y_ref[0] = _dot(hh, wdb_ref[...]) * gate_ref[0]


def _experts(xe, gates, wg, wu, wd, layer, tm):
    e, cp, d = xe.shape
    tile = lambda ei, i: (ei, i, 0)
    wsp = lambda ei, i: (layer, ei, 0, 0)
    return pl.pallas_call(
        _expert_kernel,
        grid=(e, cp // tm),
        in_specs=[
            pl.BlockSpec((1, tm, d), tile),
            pl.BlockSpec((1, tm, 1), tile),
            pl.BlockSpec((None, 1, d, D_EXPERT), wsp),
            pl.BlockSpec((None, 1, d, D_EXPERT), wsp),
            pl.BlockSpec((None, 1, D_EXPERT, d), wsp),
        ],
        out_specs=pl.BlockSpec((1, tm, d), tile),
        out_shape=jax.ShapeDtypeStruct((e, cp, d), F32),
        scratch_shapes=[pltpu.VMEM((d, D_EXPERT), BF16), pltpu.VMEM((d, D_EXPERT), BF16),
                        pltpu.VMEM((D_EXPERT, d), BF16)],
        compiler_params=_cparams(("parallel", "arbitrary")),
        name="experts",
    )(xe, gates, wg, wu, wd)


ROUTE_CHUNK = 128
ROUTE_WINDOW = 2 * ROUTE_CHUNK


def _route_select_kernel(aff_ref, tri_ref, pos_ref, off_ref, *, cap):
    n = aff_ref.shape[0]
    rc = ROUTE_CHUNK
    n_chunks = n // rc

    def bits_of(c):
        return pltpu.bitcast(aff_ref[pl.ds(pl.multiple_of(c * rc, rc), rc), :], jnp.int32)

    def count(pred):
        def body(c, acc):
            return acc + jnp.sum(pred(bits_of(c)).astype(jnp.int32), axis=0, keepdims=True)
        return lax.fori_loop(0, n_chunks, body, jnp.zeros((1, LANES), jnp.int32))

    def search(i, thr):
        cand = thr | jnp.left_shift(jnp.int32(1), 30 - i)
        return jnp.where(count(lambda b: b >= cand) >= cap, cand, thr)

    thr = lax.fori_loop(0, 31, search, jnp.zeros((1, LANES), jnp.int32))
    need = (cap - count(lambda b: b > thr)).astype(F32)
    tri = tri_ref[...]

    def emit(c, carry):
        ties_before, picks_before = carry
        b = bits_of(c)
        tie = b == thr
        tie_incl = _dot(tri, jnp.where(tie, 1.0, 0.0).astype(BF16)) + ties_before
        pick = (b > thr) | (tie & (tie_incl - 1.0 < need))
        pick_incl = _dot(tri, jnp.where(pick, 1.0, 0.0).astype(BF16)) + picks_before
        pos_ref[pl.ds(pl.multiple_of(c * rc, rc), rc), :] = jnp.where(pick, pick_incl - 1.0, -1.0)
        off_ref[c] = picks_before.astype(jnp.int32)
        return tie_incl[rc - 1:rc], pick_incl[rc - 1:rc]

    zero = jnp.zeros((1, LANES), F32)
    lax.fori_loop(0, n_chunks, emit, (zero, zero))


def _route_compact_kernel(off_ref, pos_ref, aff_ref, idx_ref, gate_ref):
    c = pl.program_id(0)
    rc = ROUTE_CHUNK

    @pl.when(c == 0)
    def _():
        idx_ref[...] = jnp.zeros(idx_ref.shape, F32)
        gate_ref[...] = jnp.zeros(gate_ref.shape, F32)

    token = (lax.broadcasted_iota(jnp.int32, (rc, 1), 0) + c * rc).astype(F32)
    slot = lax.broadcasted_iota(jnp.int32, (1, ROUTE_WINDOW), 1).astype(F32)
    pos = pos_ref[...]
    aff = aff_ref[...]
    for e in range(N_EXPERTS):
        start = pl.multiple_of((off_ref[c * N_EXPERTS + e] // LANES) * LANES, LANES)
        hit = (pos[:, e:e + 1] - start.astype(F32)) == slot
        win = (slice(e, e + 1), pl.ds(start, ROUTE_WINDOW))
        idx_ref[win] += jnp.sum(jnp.where(hit, token, 0.0), axis=0, keepdims=True)
        gate_ref[win] += jnp.sum(jnp.where(hit, aff[:, e:e + 1], 0.0), axis=0, keepdims=True)


def _route(aff, cap, slots):
    n = aff.shape[0]
    rc = ROUTE_CHUNK
    n_chunks = n // rc
    tri = jnp.asarray(np.tril(np.ones((rc, rc), np.float32)), BF16)
    whole = pl.BlockSpec(memory_space=pltpu.VMEM)
    pos, off = pl.pallas_call(
        functools.partial(_route_select_kernel, cap=cap),
        in_specs=[whole, whole],
        out_specs=(whole, whole),
        out_shape=(jax.ShapeDtypeStruct((n, LANES), F32),
                   jax.ShapeDtypeStruct((n_chunks, 1, LANES), jnp.int32)),
        compiler_params=pltpu.CompilerParams(vmem_limit_bytes=VMEM_LIMIT),
        name="route_select",
    )(aff, tri)
    off = off[:, 0, :N_EXPERTS].reshape(-1)
    chunk = lambda c, off_ref: (c, 0)
    fixed = lambda c, off_ref: (0, 0)
    out = jax.ShapeDtypeStruct((N_EXPERTS, slots), F32)
    idx, gates = pl.pallas_call(
        _route_compact_kernel,
        grid_spec=pltpu.PrefetchScalarGridSpec(
            num_scalar_prefetch=1,
            grid=(n_chunks,),
            in_specs=[pl.BlockSpec((rc, LANES), chunk), pl.BlockSpec((rc, LANES), chunk)],
            out_specs=(pl.BlockSpec((N_EXPERTS, slots), fixed), pl.BlockSpec((N_EXPERTS, slots), fixed)),
        ),
        out_shape=(out, out),
        compiler_params=_cparams(("arbitrary",)),
        name="route_compact",
    )(off, pos, aff)
    return idx.astype(jnp.int32), gates


def _final_kernel(x_ref, g_ref, o_ref):
    x = x_ref[0]
    ms = jnp.mean(x * x, axis=-1, keepdims=True)
    o_ref[0] = x * lax.rsqrt(ms + EPS) * g_ref[...]


def _final_norm(h3, g, s):
    b = h3.shape[0]
    tm = FRONT
    return pl.pallas_call(
        _final_kernel,
        grid=(b, s // tm),
        in_specs=[pl.BlockSpec((1, tm, D_MODEL), lambda bi, i: (bi, i + FRONT // tm, 0)),
                  pl.BlockSpec((1, D_MODEL), lambda bi, i: (0, 0))],
        out_specs=pl.BlockSpec((1, tm, D_MODEL), lambda bi, i: (bi, i, 0)),
        out_shape=jax.ShapeDtypeStruct((b, s, D_MODEL), F32),
        compiler_params=_cparams(("parallel", "parallel")),
        name="final_norm",
    )(h3, g)


def _rope_tables(s):
    lp = FRONT + s
    t = np.arange(lp) - FRONT
    real = t >= 0
    pos = np.stack([np.where(real, t // GRID_W, 0), np.where(real, t % GRID_W, 0)], axis=-1)
    n_freq = HEAD_DIM // 4
    inv_freq = jnp.asarray(ROPE_THETA, F32) ** (-jnp.arange(n_freq, dtype=F32) / n_freq)
    lane = np.arange(LANES) % HEAD_DIM
    axis = lane // (HEAD_DIM // 2)
    freq = lane % n_freq
    ang = jnp.asarray(pos, F32)[:, axis] * inv_freq[freq][None, :]
    sign = np.where((lane % (HEAD_DIM // 2)) < n_freq, -1.0, 1.0).astype(np.float32)
    return jnp.cos(ang), jnp.sin(ang) * sign[None, :]


def _pack_w_in(w):
    o = 0

    def take(n):
        nonlocal o
        part = w[:, o:o + n]
        o += n
        return part

    def head_blocks(part, n_heads):
        part = part.reshape(D_MODEL, n_heads, HEAD_DIM)
        part = jnp.pad(part, ((0, 0), (0, 0), (0, LANES - HEAD_DIM)))
        return part.reshape(D_MODEL, n_heads * LANES)

    q_a = head_blocks(take(N_HEADS_A * HEAD_DIM), N_HEADS_A)
    k_a = head_blocks(take(N_KV_HEADS * HEAD_DIM), N_KV_HEADS)
    v_a = take(N_KV_HEADS * HEAD_DIM).reshape(D_MODEL, N_KV_HEADS, HEAD_DIM)
    v_a = jnp.pad(v_a, ((0, 0), (0, 0), (0, V_ROWS - HEAD_DIM))).reshape(D_MODEL, N_KV_HEADS * V_ROWS)
    qkv_d = take(3 * N_HEADS_D * DK)
    z_d = take(N_HEADS_D * DV)
    small = take(4 * N_HEADS_D)
    gate_a = take(D_MODEL)
    gate_d = take(D_MODEL)
    small = jnp.pad(small, ((0, 0), (0, LANES - small.shape[1])))
    w_all = jnp.concatenate([q_a, k_a, qkv_d, z_d, small, gate_a, gate_d], axis=1).astype(BF16)
    return w_all, v_a.T.astype(BF16)


def _chunk_tri(tm, reverse):
    i = np.arange(tm)
    same = (i[:, None] // CHUNK) == (i[None, :] // CHUNK)
    tri = (i[None, :] >= i[:, None]) if reverse else (i[None, :] <= i[:, None])
    return jnp.asarray((same & tri).astype(np.float32))


def _expert_tile(cap):
    n_tiles = -(-cap // 512)
    tm = -(-cap // n_tiles)
    tm = -(-tm // 16) * 16
    return tm, n_tiles


def _trunk(x, meta_tokens, layers, ffn_w, norm_final):
    b, s, d = x.shape
    lp = FRONT + s
    n = b * lp
    n_tok = b * (N_META + s)
    cap = EC_CAPACITY * n_tok // N_EXPERTS
    etm, ent = _expert_tile(cap)
    cap_pad = etm * ent
    slots = -(-max(cap_pad, cap + ROUTE_WINDOW) // LANES) * LANES
    cos, sin = _rope_tables(s)
    seg = jnp.asarray(np.kron(np.eye(N_HEADS_D), np.ones((DK, DK))), BF16)
    trif = _chunk_tri(ROW_TILE, False)
    trib = _chunk_tri(ROW_TILE, True)

    front = jnp.concatenate([jnp.zeros((N_NULL, d), F32), meta_tokens.astype(F32)], axis=0)
    h = jnp.concatenate([jnp.broadcast_to(front[None], (b, FRONT, d)), x], axis=1).reshape(n, d)

    for li, lw in enumerate(layers):
        q, k, vt, qkvd, z, sm, ga, gd = _inproj(h, lw["norm_mix"], lw["w_all"], lw["wvt"], cos, sin,
                                                lw["qn"], lw["kn"], lp)
        at = _attention(q.reshape(b, lp, -1), k.reshape(b, lp, -1), vt, s)
        qd, kd, vd, gcf, gcb, bf, bb = _delta_prep(qkvd.reshape(b, lp, -1), sm.reshape(b, lp, -1),
                                                   lw["conv_w"], lw["rate"], lw["dtb"], seg, trif, trib)
        of, ob = _delta_scan(qd, kd, vd, gcf, gcb, bf, bb)
        hw = N_HEADS_D * DV
        h, xn, aff = _post(h, at.reshape(n, hw), of.reshape(n, hw), ob.reshape(n, hw), z, ga, gd,
                           lw["on"], seg, lw["w_attn_proj"], lw["w_delta_proj"], lw["w_out"],
                           lw["norm_ffn"], lw["wr_hi"], lw["wr_lo"], lp)
        idx, gates = _route(aff, cap, slots)
        idx = idx[:, :cap_pad]
        gates = gates[:, :cap_pad]
        xe = jnp.take(xn, idx.reshape(-1), axis=0).reshape(N_EXPERTS, cap_pad, d)
        ye = _experts(xe, gates[..., None], ffn_w[0], ffn_w[1], ffn_w[2], li, etm)
        h = h.at[idx.reshape(-1)].add(ye.reshape(-1, d))

    return _final_norm(h.reshape(b, lp, d), norm_final, s)


def kernel(x_prompt, x_sample, meta_tokens, norm_mix, w_in, q_norm, k_norm, conv_w, a_log, dt_bias,
           o_norm, w_attn_proj, w_delta_proj, w_out, norm_ffn, w_router, w_gate, w_up, w_down,
           norm_final):
    depth = w_in.shape[0]
    layers = []
    for l in range(depth):
        rate = jnp.exp(a_log[l].astype(F32)).reshape(1, -1)
        dtb = dt_bias[l].astype(F32).reshape(1, -1)
        pad = LANES - rate.shape[1]
        wr = jnp.pad(w_router[l].astype(F32), ((0, 0), (0, LANES - N_EXPERTS)))
        wr_hi = wr.astype(BF16)
        w_all, wvt = _pack_w_in(w_in[l])
        layers.append(dict(
            norm_mix=norm_mix[l].astype(F32).reshape(1, -1),
            w_all=w_all,
            wvt=wvt,
            qn=jnp.tile(q_norm[l].astype(F32), LANES // HEAD_DIM).reshape(1, -1),
            kn=jnp.tile(k_norm[l].astype(F32), LANES // HEAD_DIM).reshape(1, -1),
            conv_w=conv_w[l].astype(F32),
            rate=jnp.pad(rate, ((0, 0), (0, pad))),
            dtb=jnp.pad(dtb, ((0, 0), (0, pad))),
            on=jnp.tile(o_norm[l].astype(F32), N_HEADS_D).reshape(1, -1),
            w_attn_proj=w_attn_proj[l].astype(BF16),
            w_delta_proj=w_delta_proj[l].astype(BF16),
            w_out=w_out[l].astype(BF16),
            norm_ffn=norm_ffn[l].astype(F32).reshape(1, -1),
            wr_hi=wr_hi,
            wr_lo=(wr - wr_hi.astype(F32)).astype(BF16),
        ))
    ffn_w = (w_gate.astype(F32), w_up.astype(F32), w_down.astype(F32))
    nf = norm_final.astype(F32).reshape(1, -1)
    y_prompt = _trunk(x_prompt, meta_tokens, layers, ffn_w, nf)
    y_sample = _trunk(x_sample, meta_tokens, layers, ffn_w, nf)
    return (y_prompt, y_sample)
```

```python
import functools

import numpy as np
import jax
import jax.numpy as jnp
from jax import lax
from jax.experimental import pallas as pl
from jax.experimental.pallas import tpu as pltpu

F32 = jnp.float32
BF16 = jnp.bfloat16

D_MODEL = 1024
N_META = 16
GRID_W = 64
N_HEADS_A = 8
N_KV_HEADS = 2
HEAD_DIM = 64
ROPE_THETA = 10000.0
N_HEADS_D = 8
DK = 64
DV = 64
CHUNK = 64
N_EXPERTS = 16
EC_CAPACITY = 2
D_EXPERT = 1024
EPS = 1e-6

FRONT = 128
N_NULL = FRONT - N_META
LANES = 128
ROW_TILE = 384
HEAD_GROUP = 4
GROUP_W = HEAD_GROUP * DK
VMEM_LIMIT = 48 * 1024 * 1024
Q_SCALE = HEAD_DIM ** -0.5 * float(np.log2(np.e))
PAIRS_PER_TRIP = 3
LANE_SLICES = 2
V_ROWS = HEAD_DIM + 16

C_Q = 0
C_K = C_Q + N_HEADS_A * LANES
C_QKVD = C_K + N_KV_HEADS * LANES
C_Z = C_QKVD + 3 * N_HEADS_D * DK
C_SM = C_Z + N_HEADS_D * DV
C_GA = C_SM + LANES
C_GD = C_GA + D_MODEL
C_END = C_GD + D_MODEL


def _cparams(sem):
    return pltpu.CompilerParams(dimension_semantics=sem, vmem_limit_bytes=VMEM_LIMIT)


def _dot(a, b):
    return jnp.dot(a, b, preferred_element_type=F32)


def _dot_nt(a, b):
    return lax.dot_general(a, b, (((1,), (1,)), ((), ())), preferred_element_type=F32)


def _split_dot(t, w_bf16):
    hi = t.astype(BF16)
    lo = (t - hi.astype(F32)).astype(BF16)
    return _dot(hi, w_bf16) + _dot(lo, w_bf16)


def _inproj_kernel(x_ref, g_ref, w_ref, wvt_ref, cos_ref, sin_ref, qn_ref, kn_ref,
                   q_ref, k_ref, vt_ref, qkvd_ref, z_ref, sm_ref, ga_ref, gd_ref):
    x = x_ref[...]
    ms = jnp.mean(x * x, axis=-1, keepdims=True)
    u = (x * lax.rsqrt(ms + EPS) * g_ref[...]).astype(BF16)
    cos = cos_ref[...]
    sin = sin_ref[...]
    lane = lax.broadcasted_iota(jnp.int32, (1, LANES), 1)
    first = (lane % 32) < 16

    def rope(t):
        rot = jnp.where(first, pltpu.roll(t, LANES - 16, 1), pltpu.roll(t, 16, 1))
        return t * cos + rot * sin

    def mm(c0, n):
        return _dot(u, w_ref[:, c0:c0 + n])

    def head(t, gain):
        msq = jnp.sum(t * t, axis=-1, keepdims=True) * (1.0 / HEAD_DIM)
        return rope(t * lax.rsqrt(msq + EPS) * gain)

    for hp in range(N_HEADS_A // 2):
        t2 = mm(C_Q + 2 * LANES * hp, 2 * LANES)
        for i in range(2):
            h = 2 * hp + i
            q_ref[:, LANES * h:LANES * (h + 1)] = (
                head(t2[:, LANES * i:LANES * (i + 1)], qn_ref[...]) * Q_SCALE).astype(BF16)
    t2 = mm(C_K, N_KV_HEADS * LANES)
    for j in range(N_KV_HEADS):
        k_ref[:, LANES * j:LANES * (j + 1)] = head(t2[:, LANES * j:LANES * (j + 1)], kn_ref[...]).astype(BF16)

    vt = _dot_nt(wvt_ref[...], u)
    vrow = lax.broadcasted_iota(jnp.int32, (N_KV_HEADS * V_ROWS, 1), 0) % V_ROWS
    vt_ref[...] = jnp.where(vrow < HEAD_DIM, vt, 1.0).astype(BF16)
    qkvd_ref[...] = mm(C_QKVD, C_Z - C_QKVD)
    z_ref[...] = mm(C_Z, C_SM - C_Z).astype(BF16)
    sm_ref[...] = mm(C_SM, LANES)
    ga_ref[...] = mm(C_GA, D_MODEL).astype(BF16)
    gd_ref[...] = mm(C_GD, D_MODEL).astype(BF16)


def _inproj(h2, g, w_all, wvt, cos, sin, qn, kn, lp):
    n = h2.shape[0]
    tm = ROW_TILE
    nt = lp // tm
    row = lambda i: (i, 0)
    const = lambda i: (0, 0)
    tab = lambda i: (i % nt, 0)
    out_shapes = (
        jax.ShapeDtypeStruct((n, N_HEADS_A * LANES), BF16),
        jax.ShapeDtypeStruct((n, N_KV_HEADS * LANES), BF16),
        jax.ShapeDtypeStruct((N_KV_HEADS * V_ROWS, n), BF16),
        jax.ShapeDtypeStruct((n, C_Z - C_QKVD), F32),
        jax.ShapeDtypeStruct((n, C_SM - C_Z), BF16),
        jax.ShapeDtypeStruct((n, LANES), F32),
        jax.ShapeDtypeStruct((n, D_MODEL), BF16),
        jax.ShapeDtypeStruct((n, D_MODEL), BF16),
    )
    out_specs = [pl.BlockSpec((tm, s.shape[1]), row) for s in out_shapes]
    out_specs[2] = pl.BlockSpec((N_KV_HEADS * V_ROWS, tm), lambda i: (0, i))
    return pl.pallas_call(
        _inproj_kernel,
        grid=(n // tm,),
        in_specs=[
            pl.BlockSpec((tm, D_MODEL), row),
            pl.BlockSpec((1, D_MODEL), const),
            pl.BlockSpec((D_MODEL, C_END), const, pipeline_mode=pl.Buffered(1)),
            pl.BlockSpec((N_KV_HEADS * V_ROWS, D_MODEL), const, pipeline_mode=pl.Buffered(1)),
            pl.BlockSpec((tm, LANES), tab),
            pl.BlockSpec((tm, LANES), tab),
            pl.BlockSpec((1, LANES), const),
            pl.BlockSpec((1, LANES), const),
        ],
        out_specs=tuple(out_specs),
        out_shape=out_shapes,
        compiler_params=_cparams(("parallel",)),
        name="inproj",
    )(h2, g, w_all, wvt, cos, sin, qn, kn)


def _flash_kernel(q_ref, k_ref, vt_ref, o_ref, qs_ref, m_ref, acc_ref, sa_ref, sb_ref, ca_ref, cb_ref,
                  *, tq, tk, s_real):
    g = N_HEADS_A // N_KV_HEADS
    r = g * tq
    for i in range(g):
        qs_ref[i * tq:(i + 1) * tq, :] = q_ref[0, :, LANES * i:LANES * (i + 1)]
    m_ref[...] = jnp.full(m_ref.shape, -jnp.inf, F32)
    acc_ref[...] = jnp.zeros(acc_ref.shape, F32)

    def scores(koff, size):
        return _dot_nt(k_ref[0, pl.ds(koff, size), :], qs_ref[...])

    def absorb(s, cmax, koff, size):
        m_prev = m_ref[...]
        m_new = jnp.maximum(m_prev, cmax)
        alpha = jnp.exp2(m_prev - m_new)
        p = jnp.exp2(s - m_new).astype(BF16)
        acc_ref[...] = alpha * acc_ref[...] + _dot(vt_ref[:, pl.ds(koff, size)], p)
        m_ref[...] = m_new

    def produce(koff, s_ref, c_ref):
        s = scores(koff, tk)
        s_ref[...] = s
        c_ref[...] = jnp.max(s, axis=0, keepdims=True)

    def main_off(c):
        off = FRONT + c * tk
        return off if isinstance(c, int) else pl.multiple_of(off, LANES)

    s = scores(0, FRONT)
    s = jnp.where(lax.broadcasted_iota(jnp.int32, (FRONT, 1), 0) >= N_NULL, s, -jnp.inf)

    n_main = s_real // tk
    if n_main:
        produce(main_off(0), sa_ref, ca_ref)
    absorb(s, jnp.max(s, axis=0, keepdims=True), 0, FRONT)
    if n_main:
        n_pairs = (n_main - 1) // 2

        hw = r // LANE_SLICES

        def step(koff_next, nxt, koff_cur, cur):
            (sn_ref, cn_ref), (sc_ref, cc_ref) = nxt, cur
            for hh in range(LANE_SLICES):
                sl = slice(hh * hw, (hh + 1) * hw)
                s_new = _dot_nt(k_ref[0, pl.ds(koff_next, tk), :], qs_ref[sl, :])
                sn_ref[:, sl] = s_new
                cn_ref[:, sl] = jnp.max(s_new, axis=0, keepdims=True)
                m_prev = m_ref[:, sl]
                m_new = jnp.maximum(m_prev, cc_ref[:, sl])
                alpha = jnp.exp2(m_prev - m_new)
                p = jnp.exp2(sc_ref[:, sl] - m_new).astype(BF16)
                acc_ref[:, sl] = alpha * acc_ref[:, sl] + _dot(vt_ref[:, pl.ds(koff_cur, tk)], p)
                m_ref[:, sl] = m_new

        def pair(pi):
            buf_a, buf_b = (sa_ref, ca_ref), (sb_ref, cb_ref)
            step(main_off(2 * pi + 1), buf_b, main_off(2 * pi), buf_a)
            step(main_off(2 * pi + 2), buf_a, main_off(2 * pi + 1), buf_b)

        def body(t, carry):
            for u in range(PAIRS_PER_TRIP):
                pair(t * PAIRS_PER_TRIP + u)
            return carry

        lax.fori_loop(0, n_pairs // PAIRS_PER_TRIP, body, 0)
        for pi in range(n_pairs - n_pairs % PAIRS_PER_TRIP, n_pairs):
            pair(pi)
        if n_main - 2 * n_pairs == 2:
            step(main_off(2 * n_pairs + 1), (sb_ref, cb_ref), main_off(2 * n_pairs), (sa_ref, ca_ref))
            absorb(sb_ref[...], cb_ref[...], main_off(2 * n_pairs + 1), tk)
        else:
            absorb(sa_ref[...], ca_ref[...], main_off(2 * n_pairs), tk)
    if s_real % tk:
        s = scores(FRONT + n_main * tk, s_real % tk)
        absorb(s, jnp.max(s, axis=0, keepdims=True), FRONT + n_main * tk, s_real % tk)

    acc = acc_ref[...]
    out_t = acc[:HEAD_DIM] / acc[HEAD_DIM:HEAD_DIM + 1]
    for p in range(g // 2):
        pair = jnp.concatenate([out_t[:, (2 * p) * tq:(2 * p + 1) * tq],
                                out_t[:, (2 * p + 1) * tq:(2 * p + 2) * tq]], axis=0)
        o_ref[0, :, 2 * HEAD_DIM * p:2 * HEAD_DIM * (p + 1)] = pair.T.astype(BF16)


def _attention(q, k, vt, s_real):
    b, lp, _ = q.shape
    tq = ROW_TILE
    tk = 512
    g = N_HEADS_A // N_KV_HEADS
    return pl.pallas_call(
        functools.partial(_flash_kernel, tq=tq, tk=tk, s_real=s_real),
        grid=(b, N_KV_HEADS, lp // tq),
        in_specs=[
            pl.BlockSpec((1, tq, g * LANES), lambda bi, j, qi: (bi, qi, j)),
            pl.BlockSpec((1, lp, LANES), lambda bi, j, qi: (bi, 0, j)),
            pl.BlockSpec((V_ROWS, lp), lambda bi, j, qi: (j, bi)),
        ],
        out_specs=pl.BlockSpec((1, tq, g * HEAD_DIM), lambda bi, j, qi: (bi, qi, j)),
        out_shape=jax.ShapeDtypeStruct((b, lp, N_HEADS_A * HEAD_DIM), BF16),
        scratch_shapes=[
            pltpu.VMEM((g * tq, LANES), BF16),
            pltpu.VMEM((1, g * tq), F32),
            pltpu.VMEM((V_ROWS, g * tq), F32),
            pltpu.VMEM((tk, g * tq), F32),
            pltpu.VMEM((tk, g * tq), F32),
            pltpu.VMEM((1, g * tq), F32),
            pltpu.VMEM((1, g * tq), F32),
        ],
        compiler_params=_cparams(("parallel", "parallel", "arbitrary")),
        name="attention",
    )(q, k, vt)


def _dprep_kernel(prev_ref, cur_ref, next_ref, sm_ref, cw_ref, rate_ref, dtb_ref, seg_ref,
                  trif_ref, trib_ref, q_ref, k_ref, v_ref, gcf_ref, gcb_ref, bf_ref, bb_ref,
                  *, tm, n_tiles):
    i = pl.program_id(1)
    row = lax.broadcasted_iota(jnp.int32, (tm, 1), 0)
    valid = (row + i * tm) >= N_NULL
    x = jnp.where(valid, cur_ref[0], 0.0)
    prow = jnp.where(i > 0, prev_ref[0, 7:8, :], 0.0)
    nrow = jnp.where(i < n_tiles - 1, next_ref[0, 0:1, :], 0.0)
    xp = jnp.where(row == 0, prow, pltpu.roll(x, 1, 0))
    xn = jnp.where(row == tm - 1, nrow, pltpu.roll(x, tm - 1, 0))
    cw = cw_ref[...]
    y = cw[0:1] * xp + cw[1:2] * x + cw[2:3] * xn
    y = y * jax.nn.sigmoid(y)

    w = N_HEADS_D * DK
    seg = seg_ref[...]
    q = y[:, 0:w]
    q = q * lax.rsqrt(_split_dot(q * q, seg) + EPS) * (DK ** -0.5)
    k = y[:, w:2 * w]
    k = k * lax.rsqrt(_split_dot(k * k, seg) + EPS)
    q_ref[0] = jnp.where(valid, q, 0.0)
    k_ref[0] = jnp.where(valid, k, 0.0)
    v_ref[0] = jnp.where(valid, y[:, 2 * w:3 * w], 0.0)

    sm = sm_ref[0]
    t = sm + dtb_ref[...]
    softplus = jnp.maximum(t, 0.0) + jnp.log1p(jnp.exp(-jnp.abs(t)))
    g_all = jnp.where(valid, -rate_ref[...] * softplus, 0.0)
    beta_all = jnp.where(valid, jax.nn.sigmoid(sm), 0.0)
    gc_f = jnp.dot(trif_ref[...], g_all, preferred_element_type=F32, precision=lax.Precision.HIGHEST)
    gc_b = jnp.dot(trib_ref[...], g_all, preferred_element_type=F32, precision=lax.Precision.HIGHEST)

    lane = lax.broadcasted_iota(jnp.int32, (1, LANES), 1)
    low = lane < DK

    def expand(a, c0, out_ref):
        for p in range(N_HEADS_D // 2):
            e0 = jnp.broadcast_to(a[:, c0 + 2 * p:c0 + 2 * p + 1], (tm, LANES))
            e1 = jnp.broadcast_to(a[:, c0 + 2 * p + 1:c0 + 2 * p + 2], (tm, LANES))
            out_ref[0, :, LANES * p:LANES * (p + 1)] = jnp.where(low, e0, e1)

    expand(gc_f, 0, gcf_ref)
    expand(gc_b, N_HEADS_D, gcb_ref)
    expand(beta_all, 2 * N_HEADS_D, bf_ref)
    expand(beta_all, 3 * N_HEADS_D, bb_ref)


def _delta_prep(qkvd, sm, cw, rate, dtb, seg, trif, trib):
    b, lp, c = qkvd.shape
    tm = ROW_TILE
    nt = lp // tm
    w = N_HEADS_D * DK
    halo = 8
    nb8 = lp // halo
    cur = lambda bi, i: (bi, i, 0)
    const = lambda bi, i: (0, 0)
    out = jax.ShapeDtypeStruct((b, lp, w), F32)
    return pl.pallas_call(
        functools.partial(_dprep_kernel, tm=tm, n_tiles=nt),
        grid=(b, nt),
        in_specs=[
            pl.BlockSpec((1, halo, c), lambda bi, i: (bi, jnp.maximum(i * (tm // halo) - 1, 0), 0)),
            pl.BlockSpec((1, tm, c), cur),
            pl.BlockSpec((1, halo, c), lambda bi, i: (bi, jnp.minimum((i + 1) * (tm // halo), nb8 - 1), 0)),
            pl.BlockSpec((1, tm, LANES), cur),
            pl.BlockSpec((3, c), const),
            pl.BlockSpec((1, LANES), const),
            pl.BlockSpec((1, LANES), const),
            pl.BlockSpec((w, w), const),
            pl.BlockSpec((tm, tm), const),
            pl.BlockSpec((tm, tm), const),
        ],
        out_specs=tuple(pl.BlockSpec((1, tm, w), cur) for _ in range(7)),
        out_shape=(out,) * 7,
        compiler_params=_cparams(("parallel", "parallel")),
        name="delta_prep",
    )(qkvd, qkvd, qkvd, sm, cw, rate, dtb, seg, trif, trib)


def _dscan_kernel(qf_ref, kf_ref, vf_ref, gf_ref, bf_ref, qb_ref, kb_ref, vb_ref, gb_ref, bb_ref,
                  of_ref, ob_ref, s_ref):
    i = pl.program_id(0)

    @pl.when(i == 0)
    def _():
        s_ref[...] = jnp.zeros(s_ref.shape, F32)

    c = CHUNK
    r = lax.broadcasted_iota(jnp.int32, (c, GROUP_W), 0)
    cj = lax.broadcasted_iota(jnp.int32, (c, GROUP_W), 1) % c
    eye = r == cj
    eyef = eye.astype(F32)
    bdmask = (lax.broadcasted_iota(jnp.int32, (GROUP_W, GROUP_W), 0) // c ==
              lax.broadcasted_iota(jnp.int32, (GROUP_W, GROUP_W), 1) // c)

    lane_low = lax.broadcasted_iota(jnp.int32, (c, LANES), 1) < DK
    zero_tile = jnp.zeros((c, LANES), BF16)

    def bd(t):
        rows = []
        for hd in range(HEAD_GROUP):
            tile = t[:, LANES * (hd // 2):LANES * (hd // 2 + 1)]
            tile = jnp.where(lane_low == (hd % 2 == 0), tile, 0.0).astype(BF16)
            rows.append(jnp.concatenate([tile, zero_tile] if hd < 2 else [zero_tile, tile], axis=1))
        return jnp.concatenate(rows, axis=0)

    n_grp = N_HEADS_D // HEAD_GROUP
    fwd_refs = (qf_ref, kf_ref, vf_ref, gf_ref, bf_ref, of_ref)
    bwd_refs = (qb_ref, kb_ref, vb_ref, gb_ref, bb_ref, ob_ref)
    chains = []
    for bi in range(of_ref.shape[0]):
        for grp in range(n_grp):
            for reverse in (False, True):
                sl = slice(GROUP_W * grp, GROUP_W * (grp + 1))
                chains.append((fwd_refs if not reverse else bwd_refs, bi, sl, reverse, len(chains)))

    def each(fn, *cols):
        return [fn(*args) for args in zip(*cols)]

    rev = [ch[3] for ch in chains]
    q = [ch[0][0][ch[1], :, ch[2]] for ch in chains]
    k = [ch[0][1][ch[1], :, ch[2]] for ch in chains]
    v = [ch[0][2][ch[1], :, ch[2]] for ch in chains]
    gc = [ch[0][3][ch[1], :, ch[2]] for ch in chains]
    beta = [ch[0][4][ch[1], :, ch[2]] for ch in chains]
    state = [s_ref[ch[4]] for ch in chains]

    def decay_of(g, reverse):
        incl = (r <= cj) if reverse else (r >= cj)
        gcol = jnp.sum(jnp.where(eye, g, 0.0), axis=0, keepdims=True)
        return jnp.where(incl, jnp.exp(jnp.where(incl, g - gcol, 0.0)), 0.0)

    decay = each(decay_of, gc, rev)
    eg = each(jnp.exp, gc)
    glast = each(lambda g, reverse: g[0:1] if reverse else g[c - 1:c], gc, rev)
    kbeta = each(lambda a, b: a * b, k, beta)
    gq = each(lambda kb_, q_, k_: _dot_nt(jnp.concatenate([kb_, q_], axis=0).astype(BF16), bd(k_)),
              kbeta, q, k)
    a_intra = each(lambda g, d: g[c:] * d, gq, decay)
    m = each(lambda g, d, reverse: -jnp.where((r < cj) if reverse else (r > cj), g[:c] * d, 0.0),
             gq, decay, rev)
    p = each(lambda m_: eyef + m_, m)
    m = each(lambda m_: _dot(m_.astype(BF16), bd(m_)), m)
    for _ in range(4):
        rr = each(lambda m_, p_: _dot(jnp.concatenate([m_, p_], axis=0).astype(BF16), bd(m_)), m, p)
        p = each(lambda p_, rr_: p_ + rr_[c:], p, rr)
        m = each(lambda rr_: rr_[:c], rr)
    tmat = each(lambda p_, m_: (p_ + _dot(p_.astype(BF16), bd(m_))).astype(BF16), p, m)
    u = each(lambda t, v_, b: _dot(t, bd(v_ * b)), tmat, v, beta)
    w = each(lambda t, kb_, e: _dot(t, bd(kb_ * e)), tmat, kbeta, eg)
    wq = each(lambda w_, q_, e, s: _dot(jnp.concatenate([w_, q_ * e], axis=0).astype(BF16), s.astype(BF16)),
              w, q, eg, state)
    v_new = each(lambda u_, wq_: u_ - wq_[:c], u, wq)
    o = each(lambda wq_, a, vn: wq_[c:] + _dot(a.astype(BF16), bd(vn)), wq, a_intra, v_new)
    upd = each(lambda k_, gl, g, vn: _dot((k_ * jnp.exp(gl - g)).T.astype(BF16), vn.astype(BF16)),
               k, glast, gc, v_new)
    for ch, o_, s, gl, up in zip(chains, o, state, glast, upd):
        ch[0][5][ch[1], :, ch[2]] = o_
        s_ref[ch[4]] = s * jnp.exp(gl) + jnp.where(bdmask, up, 0.0)


def _delta_scan(q, k, v, gcf, gcb, bf, bb):
    b, lp, w = q.shape
    n = lp // CHUNK
    fwd = lambda i: (0, i, 0)
    bwd = lambda i: (0, n - 1 - i, 0)
    blk = (b, CHUNK, w)
    out = jax.ShapeDtypeStruct((b, lp, w), F32)
    return pl.pallas_call(
        _dscan_kernel,
        grid=(n,),
        in_specs=[pl.BlockSpec(blk, fwd)] * 5 + [pl.BlockSpec(blk, bwd)] * 5,
        out_specs=(pl.BlockSpec(blk, fwd), pl.BlockSpec(blk, bwd)),
        out_shape=(out, out),
        scratch_shapes=[pltpu.VMEM((b * 2 * N_HEADS_D // HEAD_GROUP, GROUP_W, GROUP_W), F32)],
        compiler_params=_cparams(("arbitrary",)),
        name="delta_scan",
    )(q, k, v, gcf, bf, q, k, v, gcb, bb)


def _post_kernel(h_ref, at_ref, of_ref, ob_ref, z_ref, ga_ref, gd_ref, on_ref, seg_ref,
                 wa_ref, wd_ref, wo_ref, gf_ref, wrh_ref, wrl_ref,
                 h_out, xn_out, aff_out, *, tm, n_tiles):
    od = of_ref[...] + ob_ref[...]
    msd = _split_dot(od * od, seg_ref[...]) * (1.0 / DV)
    z = z_ref[...].astype(F32)
    od = od * lax.rsqrt(msd + EPS) * on_ref[...] * (z * jax.nn.sigmoid(z))
    y_d = _dot(od.astype(BF16), wd_ref[...])
    y_a = _dot(at_ref[...], wa_ref[...])
    merged = (jax.nn.sigmoid(ga_ref[...].astype(F32)) * y_a +
              jax.nn.sigmoid(gd_ref[...].astype(F32)) * y_d)
    h = h_ref[...] + _dot(merged.astype(BF16), wo_ref[...])
    h_out[...] = h

    ms = jnp.mean(h * h, axis=-1, keepdims=True)
    xn = h * lax.rsqrt(ms + EPS) * gf_ref[...]
    xh = xn.astype(BF16)
    xn_out[...] = xh
    xl = (xn - xh.astype(F32)).astype(BF16)
    logits = _dot(xh, wrh_ref[...]) + _dot(xl, wrh_ref[...]) + _dot(xh, wrl_ref[...])
    lane = lax.broadcasted_iota(jnp.int32, (1, LANES), 1)
    logits = jnp.where(lane < N_EXPERTS, logits, -jnp.inf)
    e = jnp.exp(logits - jnp.max(logits, axis=-1, keepdims=True))
    aff = e / jnp.sum(e, axis=-1, keepdims=True)
    row = lax.broadcasted_iota(jnp.int32, (tm, 1), 0) + (pl.program_id(0) % n_tiles) * tm
    aff_out[...] = jnp.where(row >= N_NULL, aff, -1.0)


def _post(h2, at, of, ob, z, ga, gd, on, seg, wa, wd, wo, gf, wrh, wrl, lp):
    n = h2.shape[0]
    tm = ROW_TILE
    nt = lp // tm
    row = lambda i: (i, 0)
    const = lambda i: (0, 0)
    hw = N_HEADS_D * DV

    def wspec(shape):
        return pl.BlockSpec(shape, const, pipeline_mode=pl.Buffered(1))

    return pl.pallas_call(
        functools.partial(_post_kernel, tm=tm, n_tiles=nt),
        grid=(n // tm,),
        in_specs=[
            pl.BlockSpec((tm, D_MODEL), row),
            pl.BlockSpec((tm, hw), row),
            pl.BlockSpec((tm, hw), row),
            pl.BlockSpec((tm, hw), row),
            pl.BlockSpec((tm, hw), row),
            pl.BlockSpec((tm, D_MODEL), row),
            pl.BlockSpec((tm, D_MODEL), row),
            pl.BlockSpec((1, hw), const),
            wspec((hw, hw)),
            wspec((hw, D_MODEL)),
            wspec((hw, D_MODEL)),
            wspec((D_MODEL, D_MODEL)),
            pl.BlockSpec((1, D_MODEL), const),
            wspec((D_MODEL, LANES)),
            wspec((D_MODEL, LANES)),
        ],
        out_specs=(pl.BlockSpec((tm, D_MODEL), row), pl.BlockSpec((tm, D_MODEL), row),
                   pl.BlockSpec((tm, LANES), row)),
        out_shape=(jax.ShapeDtypeStruct((n, D_MODEL), F32), jax.ShapeDtypeStruct((n, D_MODEL), BF16),
                   jax.ShapeDtypeStruct((n, LANES), F32)),
        compiler_params=_cparams(("parallel",)),
        name="merge_out",
    )(h2, at, of, ob, z, ga, gd, on, seg, wa, wd, wo, gf, wrh, wrl)


def _expert_kernel(x_ref, gate_ref, wg_ref, wu_ref, wd_ref, y_ref, wgb_ref, wub_ref, wdb_ref):
    @pl.when(pl.program_id(1) == 0)
    def _():
        wgb_ref[...] = wg_ref[0].astype(BF16)
        wub_ref[...] = wu_ref[0].astype(BF16)
        wdb_ref[...] = wd_ref[0].astype(BF16)

    x = x_ref[0]
    a = _dot(x, wgb_ref[...])
    b = _dot(x, wub_ref[...])
    hh = (a * jax.nn.sigmoid(a) * b).astype(BF16)
    y_ref[0] = _dot(hh, wdb_ref[...]) * gate_ref[0]


def _experts(xe, gates, wg, wu, wd, layer, tm):
    e, cp, d = xe.shape
    tile = lambda ei, i: (ei, i, 0)
    wsp = lambda ei, i: (layer, ei, 0, 0)
    return pl.pallas_call(
        _expert_kernel,
        grid=(e, cp // tm),
        in_specs=[
            pl.BlockSpec((1, tm, d), tile),
            pl.BlockSpec((1, tm, 1), tile),
            pl.BlockSpec((None, 1, d, D_EXPERT), wsp),
            pl.BlockSpec((None, 1, d, D_EXPERT), wsp),
            pl.BlockSpec((None, 1, D_EXPERT, d), wsp),
        ],
        out_specs=pl.BlockSpec((1, tm, d), tile),
        out_shape=jax.ShapeDtypeStruct((e, cp, d), F32),
        scratch_shapes=[pltpu.VMEM((d, D_EXPERT), BF16), pltpu.VMEM((d, D_EXPERT), BF16),
                        pltpu.VMEM((D_EXPERT, d), BF16)],
        compiler_params=_cparams(("parallel", "arbitrary")),
        name="experts",
    )(xe, gates, wg, wu, wd)


ROUTE_CHUNK = 128
ROUTE_WINDOW = 2 * ROUTE_CHUNK


def _route_select_kernel(aff_ref, tri_ref, pos_ref, off_ref, *, cap):
    n = aff_ref.shape[0]
    rc = ROUTE_CHUNK
    n_chunks = n // rc

    def bits_of(c):
        return pltpu.bitcast(aff_ref[pl.ds(pl.multiple_of(c * rc, rc), rc), :], jnp.int32)

    def count(pred):
        cb = ROW_TILE

        def body(c, acc):
            blk = pltpu.bitcast(aff_ref[pl.ds(pl.multiple_of(c * cb, cb), cb), :], jnp.int32)
            return acc + jnp.sum(pred(blk).astype(jnp.int32), axis=0, keepdims=True)
        return lax.fori_loop(0, n // cb, body, jnp.zeros((1, LANES), jnp.int32))

    def search(i, thr):
        cand = thr | jnp.left_shift(jnp.int32(1), 30 - i)
        return jnp.where(count(lambda b: b >= cand) >= cap, cand, thr)

    thr = lax.fori_loop(0, 31, search, jnp.zeros((1, LANES), jnp.int32))
    need = (cap - count(lambda b: b > thr)).astype(F32)
    tri = tri_ref[...]

    def emit(c, carry):
        ties_before, picks_before = carry
        b = bits_of(c)
        tie = b == thr
        tie_incl = _dot(tri, jnp.where(tie, 1.0, 0.0).astype(BF16)) + ties_before
        pick = (b > thr) | (tie & (tie_incl - 1.0 < need))
        pick_incl = _dot(tri, jnp.where(pick, 1.0, 0.0).astype(BF16)) + picks_before
        pos_ref[pl.ds(pl.multiple_of(c * rc, rc), rc), :] = jnp.where(pick, pick_incl - 1.0, -1.0)
        off_ref[c] = picks_before.astype(jnp.int32)
        return tie_incl[rc - 1:rc], pick_incl[rc - 1:rc]

    zero = jnp.zeros((1, LANES), F32)
    lax.fori_loop(0, n_chunks, emit, (zero, zero))


def _route_compact_kernel(off_ref, pos_ref, aff_ref, idx_ref, gate_ref):
    c = pl.program_id(0)
    rc = ROUTE_CHUNK

    @pl.when(c == 0)
    def _():
        idx_ref[...] = jnp.zeros(idx_ref.shape, F32)
        gate_ref[...] = jnp.zeros(gate_ref.shape, F32)

    token = (lax.broadcasted_iota(jnp.int32, (rc, 1), 0) + c * rc).astype(F32)
    slot = lax.broadcasted_iota(jnp.int32, (1, ROUTE_WINDOW), 1).astype(F32)
    pos = pos_ref[...]
    aff = aff_ref[...]
    for e in range(N_EXPERTS):
        start = pl.multiple_of((off_ref[c * N_EXPERTS + e] // LANES) * LANES, LANES)
        hit = (pos[:, e:e + 1] - start.astype(F32)) == slot
        win = (slice(e, e + 1), pl.ds(start, ROUTE_WINDOW))
        idx_ref[win] += jnp.sum(jnp.where(hit, token, 0.0), axis=0, keepdims=True)
        gate_ref[win] += jnp.sum(jnp.where(hit, aff[:, e:e + 1], 0.0), axis=0, keepdims=True)


def _route(aff, cap, slots):
    n = aff.shape[0]
    rc = ROUTE_CHUNK
    n_chunks = n // rc
    tri = jnp.asarray(np.tril(np.ones((rc, rc), np.float32)), BF16)
    whole = pl.BlockSpec(memory_space=pltpu.VMEM)
    pos, off = pl.pallas_call(
        functools.partial(_route_select_kernel, cap=cap),
        in_specs=[whole, whole],
        out_specs=(whole, whole),
        out_shape=(jax.ShapeDtypeStruct((n, LANES), F32),
                   jax.ShapeDtypeStruct((n_chunks, 1, LANES), jnp.int32)),
        compiler_params=pltpu.CompilerParams(vmem_limit_bytes=VMEM_LIMIT),
        name="route_select",
    )(aff, tri)
    off = off[:, 0, :N_EXPERTS].reshape(-1)
    chunk = lambda c, off_ref: (c, 0)
    fixed = lambda c, off_ref: (0, 0)
    out = jax.ShapeDtypeStruct((N_EXPERTS, slots), F32)
    idx, gates = pl.pallas_call(
        _route_compact_kernel,
        grid_spec=pltpu.PrefetchScalarGridSpec(
            num_scalar_prefetch=1,
            grid=(n_chunks,),
            in_specs=[pl.BlockSpec((rc, LANES), chunk), pl.BlockSpec((rc, LANES), chunk)],
            out_specs=(pl.BlockSpec((N_EXPERTS, slots), fixed), pl.BlockSpec((N_EXPERTS, slots), fixed)),
        ),
        out_shape=(out, out),
        compiler_params=_cparams(("arbitrary",)),
        name="route_compact",
    )(off, pos, aff)
    return idx.astype(jnp.int32), gates


def _final_kernel(*refs):
    *x_refs, g_ref, o_ref = refs
    tm = x_refs[0].shape[1]
    for i, x_ref in enumerate(x_refs):
        x = x_ref[0]
        ms = jnp.mean(x * x, axis=-1, keepdims=True)
        o_ref[0, i * tm:(i + 1) * tm, :] = x * lax.rsqrt(ms + EPS) * g_ref[...]


def _final_norm(h3, g, s):
    b = h3.shape[0]
    tm = FRONT
    per_step = 4 if s % (4 * tm) == 0 else 1

    def in_spec(k):
        return pl.BlockSpec((1, tm, D_MODEL), lambda bi, i: (bi, per_step * i + k + 1, 0))

    return pl.pallas_call(
        _final_kernel,
        grid=(b, s // (per_step * tm)),
        in_specs=[in_spec(k) for k in range(per_step)] + [pl.BlockSpec((1, D_MODEL), lambda bi, i: (0, 0))],
        out_specs=pl.BlockSpec((1, per_step * tm, D_MODEL), lambda bi, i: (bi, i, 0)),
        out_shape=jax.ShapeDtypeStruct((b, s, D_MODEL), F32),
        compiler_params=_cparams(("parallel", "parallel")),
        name="final_norm",
    )(*([h3] * per_step), g)


def _rope_tables(s):
    lp = FRONT + s
    t = np.arange(lp) - FRONT
    real = t >= 0
    pos = np.stack([np.where(real, t // GRID_W, 0), np.where(real, t % GRID_W, 0)], axis=-1)
    n_freq = HEAD_DIM // 4
    inv_freq = jnp.asarray(ROPE_THETA, F32) ** (-jnp.arange(n_freq, dtype=F32) / n_freq)
    lane = np.arange(LANES) % HEAD_DIM
    axis = lane // (HEAD_DIM // 2)
    freq = lane % n_freq
    ang = jnp.asarray(pos, F32)[:, axis] * inv_freq[freq][None, :]
    sign = np.where((lane % (HEAD_DIM // 2)) < n_freq, -1.0, 1.0).astype(np.float32)
    return jnp.cos(ang), jnp.sin(ang) * sign[None, :]


def _pack_w_in(w):
    o = 0

    def take(n):
        nonlocal o
        part = w[:, o:o + n]
        o += n
        return part

    def head_blocks(part, n_heads):
        part = part.reshape(D_MODEL, n_heads, HEAD_DIM)
        part = jnp.pad(part, ((0, 0), (0, 0), (0, LANES - HEAD_DIM)))
        return part.reshape(D_MODEL, n_heads * LANES)

    q_a = head_blocks(take(N_HEADS_A * HEAD_DIM), N_HEADS_A)
    k_a = head_blocks(take(N_KV_HEADS * HEAD_DIM), N_KV_HEADS)
    v_a = take(N_KV_HEADS * HEAD_DIM).reshape(D_MODEL, N_KV_HEADS, HEAD_DIM)
    v_a = jnp.pad(v_a, ((0, 0), (0, 0), (0, V_ROWS - HEAD_DIM))).reshape(D_MODEL, N_KV_HEADS * V_ROWS)
    qkv_d = take(3 * N_HEADS_D * DK)
    z_d = take(N_HEADS_D * DV)
    small = take(4 * N_HEADS_D)
    gate_a = take(D_MODEL)
    gate_d = take(D_MODEL)
    small = jnp.pad(small, ((0, 0), (0, LANES - small.shape[1])))
    w_all = jnp.concatenate([q_a, k_a, qkv_d, z_d, small, gate_a, gate_d], axis=1).astype(BF16)
    return w_all, v_a.T.astype(BF16)


def _chunk_tri(tm, reverse):
    i = np.arange(tm)
    same = (i[:, None] // CHUNK) == (i[None, :] // CHUNK)
    tri = (i[None, :] >= i[:, None]) if reverse else (i[None, :] <= i[:, None])
    return jnp.asarray((same & tri).astype(np.float32))


def _expert_tile(cap):
    n_tiles = -(-cap // 512)
    tm = -(-cap // n_tiles)
    tm = -(-tm // 16) * 16
    return tm, n_tiles


def _trunk(x, meta_tokens, layers, ffn_w, norm_final):
    b, s, d = x.shape
    lp = FRONT + s
    n = b * lp
    n_tok = b * (N_META + s)
    cap = EC_CAPACITY * n_tok // N_EXPERTS
    etm, ent = _expert_tile(cap)
    cap_pad = etm * ent
    slots = -(-max(cap_pad, cap + ROUTE_WINDOW) // LANES) * LANES
    cos, sin = _rope_tables(s)
    seg = jnp.asarray(np.kron(np.eye(N_HEADS_D), np.ones((DK, DK))), BF16)
    trif = _chunk_tri(ROW_TILE, False)
    trib = _chunk_tri(ROW_TILE, True)

    front = jnp.concatenate([jnp.zeros((N_NULL, d), F32), meta_tokens.astype(F32)], axis=0)
    h = jnp.concatenate([jnp.broadcast_to(front[None], (b, FRONT, d)), x], axis=1).reshape(n, d)

    for li, lw in enumerate(layers):
        q, k, vt, qkvd, z, sm, ga, gd = _inproj(h, lw["norm_mix"], lw["w_all"], lw["wvt"], cos, sin,
                                                lw["qn"], lw["kn"], lp)
        at = _attention(q.reshape(b, lp, -1), k.reshape(b, lp, -1), vt, s)
        qd, kd, vd, gcf, gcb, bf, bb = _delta_prep(qkvd.reshape(b, lp, -1), sm.reshape(b, lp, -1),
                                                   lw["conv_w"], lw["rate"], lw["dtb"], seg, trif, trib)
        of, ob = _delta_scan(qd, kd, vd, gcf, gcb, bf, bb)
        hw = N_HEADS_D * DV
        h, xn, aff = _post(h, at.reshape(n, hw), of.reshape(n, hw), ob.reshape(n, hw), z, ga, gd,
                           lw["on"], seg, lw["w_attn_proj"], lw["w_delta_proj"], lw["w_out"],
                           lw["norm_ffn"], lw["wr_hi"], lw["wr_lo"], lp)
        idx, gates = _route(aff, cap, slots)
        idx = idx[:, :cap_pad]
        gates = gates[:, :cap_pad]
        xe = xn.at[idx.reshape(-1)].get(mode="promise_in_bounds").reshape(N_EXPERTS, cap_pad, d)
        ye = _experts(xe, gates[..., None], ffn_w[0], ffn_w[1], ffn_w[2], li, etm)
        h = h.at[idx.reshape(-1)].add(ye.reshape(-1, d), mode="promise_in_bounds")

    return _final_norm(h.reshape(b, lp, d), norm_final, s)


def kernel(x_prompt, x_sample, meta_tokens, norm_mix, w_in, q_norm, k_norm, conv_w, a_log, dt_bias,
           o_norm, w_attn_proj, w_delta_proj, w_out, norm_ffn, w_router, w_gate, w_up, w_down,
           norm_final):
    depth = w_in.shape[0]
    layers = []
    for l in range(depth):
        rate = jnp.exp(a_log[l].astype(F32)).reshape(1, -1)
        dtb = dt_bias[l].astype(F32).reshape(1, -1)
        pad = LANES - rate.shape[1]
        wr = jnp.pad(w_router[l].astype(F32), ((0, 0), (0, LANES - N_EXPERTS)))
        wr_hi = wr.astype(BF16)
        w_all, wvt = _pack_w_in(w_in[l])
        layers.append(dict(
            norm_mix=norm_mix[l].astype(F32).reshape(1, -1),
            w_all=w_all,
            wvt=wvt,
            qn=jnp.tile(q_norm[l].astype(F32), LANES // HEAD_DIM).reshape(1, -1),
            kn=jnp.tile(k_norm[l].astype(F32), LANES // HEAD_DIM).reshape(1, -1),
            conv_w=conv_w[l].astype(F32),
            rate=jnp.pad(rate, ((0, 0), (0, pad))),
            dtb=jnp.pad(dtb, ((0, 0), (0, pad))),
            on=jnp.tile(o_norm[l].astype(F32), N_HEADS_D).reshape(1, -1),
            w_attn_proj=w_attn_proj[l].astype(BF16),
            w_delta_proj=w_delta_proj[l].astype(BF16),
            w_out=w_out[l].astype(BF16),
            norm_ffn=norm_ffn[l].astype(F32).reshape(1, -1),
            wr_hi=wr_hi,
            wr_lo=(wr - wr_hi.astype(F32)).astype(BF16),
        ))
    ffn_w = (w_gate.astype(F32), w_up.astype(F32), w_down.astype(F32))
    nf = norm_final.astype(F32).reshape(1, -1)
    y_prompt = _trunk(x_prompt, meta_tokens, layers, ffn_w, nf)
    y_sample = _trunk(x_sample, meta_tokens, layers, ffn_w, nf)
    return (y_prompt, y_sample)
```

```python
import functools

import numpy as np
import jax
import jax.numpy as jnp
from jax import lax
from jax.experimental import pallas as pl
from jax.experimental.pallas import tpu as pltpu

F32 = jnp.float32
BF16 = jnp.bfloat16

D_MODEL = 1024
N_META = 16
GRID_W = 64
N_HEADS_A = 8
N_KV_HEADS = 2
HEAD_DIM = 64
ROPE_THETA = 10000.0
N_HEADS_D = 8
DK = 64
DV = 64
CHUNK = 64
N_EXPERTS = 16
EC_CAPACITY = 2
D_EXPERT = 1024
EPS = 1e-6

FRONT = 128
N_NULL = FRONT - N_META
LANES = 128
ROW_TILE = 384
HEAD_GROUP = 4
GROUP_W = HEAD_GROUP * DK
CHUNKS_PER_STEP = 2
VMEM_LIMIT = 48 * 1024 * 1024
Q_SCALE = HEAD_DIM ** -0.5 * float(np.log2(np.e))
PAIRS_PER_TRIP = 3
LANE_SLICES = 2
V_ROWS = HEAD_DIM + 16

C_Q = 0
C_K = C_Q + N_HEADS_A * LANES
C_QKVD = C_K + N_KV_HEADS * LANES
C_Z = C_QKVD + 3 * N_HEADS_D * DK
C_SM = C_Z + N_HEADS_D * DV
C_GA = C_SM + LANES
C_GD = C_GA + D_MODEL
C_END = C_GD + D_MODEL


def _cparams(sem):
    return pltpu.CompilerParams(dimension_semantics=sem, vmem_limit_bytes=VMEM_LIMIT)


def _dot(a, b):
    return jnp.dot(a, b, preferred_element_type=F32)


def _dot_nt(a, b):
    return lax.dot_general(a, b, (((1,), (1,)), ((), ())), preferred_element_type=F32)


def _split_dot(t, w_bf16):
    hi = t.astype(BF16)
    lo = (t - hi.astype(F32)).astype(BF16)
    return _dot(hi, w_bf16) + _dot(lo, w_bf16)


def _inproj_kernel(x_ref, g_ref, w_ref, wvt_ref, cos_ref, sin_ref, qn_ref, kn_ref,
                   q_ref, k_ref, vt_ref, qkvd_ref, z_ref, sm_ref, ga_ref, gd_ref):
    x = x_ref[...]
    ms = jnp.mean(x * x, axis=-1, keepdims=True)
    u = (x * lax.rsqrt(ms + EPS) * g_ref[...]).astype(BF16)
    cos = cos_ref[...]
    sin = sin_ref[...]
    lane = lax.broadcasted_iota(jnp.int32, (1, LANES), 1)
    first = (lane % 32) < 16

    def rope(t):
        rot = jnp.where(first, pltpu.roll(t, LANES - 16, 1), pltpu.roll(t, 16, 1))
        return t * cos + rot * sin

    def mm(c0, n):
        return _dot(u, w_ref[:, c0:c0 + n])

    def head(t, gain):
        msq = jnp.sum(t * t, axis=-1, keepdims=True) * (1.0 / HEAD_DIM)
        return rope(t * lax.rsqrt(msq + EPS) * gain)

    for hp in range(N_HEADS_A // 2):
        t2 = mm(C_Q + 2 * LANES * hp, 2 * LANES)
        for i in range(2):
            h = 2 * hp + i
            q_ref[:, LANES * h:LANES * (h + 1)] = (
                head(t2[:, LANES * i:LANES * (i + 1)], qn_ref[...]) * Q_SCALE).astype(BF16)
    t2 = mm(C_K, N_KV_HEADS * LANES)
    for j in range(N_KV_HEADS):
        k_ref[:, LANES * j:LANES * (j + 1)] = head(t2[:, LANES * j:LANES * (j + 1)], kn_ref[...]).astype(BF16)

    vt = _dot_nt(wvt_ref[...], u)
    vrow = lax.broadcasted_iota(jnp.int32, (N_KV_HEADS * V_ROWS, 1), 0) % V_ROWS
    vt_ref[...] = jnp.where(vrow < HEAD_DIM, vt, 1.0).astype(BF16)
    qkvd_ref[...] = mm(C_QKVD, C_Z - C_QKVD)
    z_ref[...] = mm(C_Z, C_SM - C_Z).astype(BF16)
    sm_ref[...] = mm(C_SM, LANES)
    ga_ref[...] = mm(C_GA, D_MODEL).astype(BF16)
    gd_ref[...] = mm(C_GD, D_MODEL).astype(BF16)


def _inproj(h2, g, w_all, wvt, cos, sin, qn, kn, lp):
    n = h2.shape[0]
    tm = ROW_TILE
    nt = lp // tm
    row = lambda i: (i, 0)
    const = lambda i: (0, 0)
    tab = lambda i: (i % nt, 0)
    out_shapes = (
        jax.ShapeDtypeStruct((n, N_HEADS_A * LANES), BF16),
        jax.ShapeDtypeStruct((n, N_KV_HEADS * LANES), BF16),
        jax.ShapeDtypeStruct((N_KV_HEADS * V_ROWS, n), BF16),
        jax.ShapeDtypeStruct((n, C_Z - C_QKVD), F32),
        jax.ShapeDtypeStruct((n, C_SM - C_Z), BF16),
        jax.ShapeDtypeStruct((n, LANES), F32),
        jax.ShapeDtypeStruct((n, D_MODEL), BF16),
        jax.ShapeDtypeStruct((n, D_MODEL), BF16),
    )
    out_specs = [pl.BlockSpec((tm, s.shape[1]), row) for s in out_shapes]
    out_specs[2] = pl.BlockSpec((N_KV_HEADS * V_ROWS, tm), lambda i: (0, i))
    return pl.pallas_call(
        _inproj_kernel,
        grid=(n // tm,),
        in_specs=[
            pl.BlockSpec((tm, D_MODEL), row),
            pl.BlockSpec((1, D_MODEL), const),
            pl.BlockSpec((D_MODEL, C_END), const, pipeline_mode=pl.Buffered(1)),
            pl.BlockSpec((N_KV_HEADS * V_ROWS, D_MODEL), const, pipeline_mode=pl.Buffered(1)),
            pl.BlockSpec((tm, LANES), tab),
            pl.BlockSpec((tm, LANES), tab),
            pl.BlockSpec((1, LANES), const),
            pl.BlockSpec((1, LANES), const),
        ],
        out_specs=tuple(out_specs),
        out_shape=out_shapes,
        compiler_params=_cparams(("parallel",)),
        name="inproj",
    )(h2, g, w_all, wvt, cos, sin, qn, kn)


def _flash_kernel(q_ref, k_ref, vt_ref, o_ref, qs_ref, m_ref, acc_ref, sa_ref, sb_ref, ca_ref, cb_ref,
                  *, tq, tk, s_real):
    g = N_HEADS_A // N_KV_HEADS
    r = g * tq
    for i in range(g):
        qs_ref[i * tq:(i + 1) * tq, :] = q_ref[0, :, LANES * i:LANES * (i + 1)]
    m_ref[...] = jnp.full(m_ref.shape, -jnp.inf, F32)
    acc_ref[...] = jnp.zeros(acc_ref.shape, F32)

    def scores(koff, size):
        return _dot_nt(k_ref[0, pl.ds(koff, size), :], qs_ref[...])

    def absorb(s, cmax, koff, size):
        m_prev = m_ref[...]
        m_new = jnp.maximum(m_prev, cmax)
        alpha = jnp.exp2(m_prev - m_new)
        p = jnp.exp2(s - m_new).astype(BF16)
        acc_ref[...] = alpha * acc_ref[...] + _dot(vt_ref[:, pl.ds(koff, size)], p)
        m_ref[...] = m_new

    def produce(koff, s_ref, c_ref):
        s = scores(koff, tk)
        s_ref[...] = s
        c_ref[...] = jnp.max(s, axis=0, keepdims=True)

    def main_off(c):
        off = FRONT + c * tk
        return off if isinstance(c, int) else pl.multiple_of(off, LANES)

    s = scores(0, FRONT)
    s = jnp.where(lax.broadcasted_iota(jnp.int32, (FRONT, 1), 0) >= N_NULL, s, -jnp.inf)

    n_main = s_real // tk
    if n_main:
        produce(main_off(0), sa_ref, ca_ref)
    absorb(s, jnp.max(s, axis=0, keepdims=True), 0, FRONT)
    if n_main:
        n_pairs = (n_main - 1) // 2

        hw = r // LANE_SLICES

        def step(koff_next, nxt, koff_cur, cur):
            (sn_ref, cn_ref), (sc_ref, cc_ref) = nxt, cur
            for hh in range(LANE_SLICES):
                sl = slice(hh * hw, (hh + 1) * hw)
                s_new = _dot_nt(k_ref[0, pl.ds(koff_next, tk), :], qs_ref[sl, :])
                sn_ref[:, sl] = s_new
                cn_ref[:, sl] = jnp.max(s_new, axis=0, keepdims=True)
                m_prev = m_ref[:, sl]
                m_new = jnp.maximum(m_prev, cc_ref[:, sl])
                alpha = jnp.exp2(m_prev - m_new)
                p = jnp.exp2(sc_ref[:, sl] - m_new).astype(BF16)
                acc_ref[:, sl] = alpha * acc_ref[:, sl] + _dot(vt_ref[:, pl.ds(koff_cur, tk)], p)
                m_ref[:, sl] = m_new

        def pair(pi):
            buf_a, buf_b = (sa_ref, ca_ref), (sb_ref, cb_ref)
            step(main_off(2 * pi + 1), buf_b, main_off(2 * pi), buf_a)
            step(main_off(2 * pi + 2), buf_a, main_off(2 * pi + 1), buf_b)

        def body(t, carry):
            for u in range(PAIRS_PER_TRIP):
                pair(t * PAIRS_PER_TRIP + u)
            return carry

        lax.fori_loop(0, n_pairs // PAIRS_PER_TRIP, body, 0)
        for pi in range(n_pairs - n_pairs % PAIRS_PER_TRIP, n_pairs):
            pair(pi)
        if n_main - 2 * n_pairs == 2:
            step(main_off(2 * n_pairs + 1), (sb_ref, cb_ref), main_off(2 * n_pairs), (sa_ref, ca_ref))
            absorb(sb_ref[...], cb_ref[...], main_off(2 * n_pairs + 1), tk)
        else:
            absorb(sa_ref[...], ca_ref[...], main_off(2 * n_pairs), tk)
    if s_real % tk:
        s = scores(FRONT + n_main * tk, s_real % tk)
        absorb(s, jnp.max(s, axis=0, keepdims=True), FRONT + n_main * tk, s_real % tk)

    acc = acc_ref[...]
    out_t = acc[:HEAD_DIM] / acc[HEAD_DIM:HEAD_DIM + 1]
    for p in range(g // 2):
        pair = jnp.concatenate([out_t[:, (2 * p) * tq:(2 * p + 1) * tq],
                                out_t[:, (2 * p + 1) * tq:(2 * p + 2) * tq]], axis=0)
        o_ref[0, :, 2 * HEAD_DIM * p:2 * HEAD_DIM * (p + 1)] = pair.T.astype(BF16)


def _attention(q, k, vt, s_real):
    b, lp, _ = q.shape
    tq = ROW_TILE
    tk = 512
    g = N_HEADS_A // N_KV_HEADS
    return pl.pallas_call(
        functools.partial(_flash_kernel, tq=tq, tk=tk, s_real=s_real),
        grid=(b, N_KV_HEADS, lp // tq),
        in_specs=[
            pl.BlockSpec((1, tq, g * LANES), lambda bi, j, qi: (bi, qi, j)),
            pl.BlockSpec((1, lp, LANES), lambda bi, j, qi: (bi, 0, j)),
            pl.BlockSpec((V_ROWS, lp), lambda bi, j, qi: (j, bi)),
        ],
        out_specs=pl.BlockSpec((1, tq, g * HEAD_DIM), lambda bi, j, qi: (bi, qi, j)),
        out_shape=jax.ShapeDtypeStruct((b, lp, N_HEADS_A * HEAD_DIM), BF16),
        scratch_shapes=[
            pltpu.VMEM((g * tq, LANES), BF16),
            pltpu.VMEM((1, g * tq), F32),
            pltpu.VMEM((V_ROWS, g * tq), F32),
            pltpu.VMEM((tk, g * tq), F32),
            pltpu.VMEM((tk, g * tq), F32),
            pltpu.VMEM((1, g * tq), F32),
            pltpu.VMEM((1, g * tq), F32),
        ],
        compiler_params=_cparams(("parallel", "parallel", "arbitrary")),
        name="attention",
    )(q, k, vt)


def _dprep_kernel(prev_ref, cur_ref, next_ref, sm_ref, cw_ref, rate_ref, dtb_ref, seg_ref,
                  trif_ref, trib_ref, q_ref, k_ref, v_ref, gcf_ref, gcb_ref, bf_ref, bb_ref,
                  *, tm, n_tiles):
    i = pl.program_id(1)
    row = lax.broadcasted_iota(jnp.int32, (tm, 1), 0)
    valid = (row + i * tm) >= N_NULL
    x = jnp.where(valid, cur_ref[0], 0.0)
    prow = jnp.where(i > 0, prev_ref[0, 7:8, :], 0.0)
    nrow = jnp.where(i < n_tiles - 1, next_ref[0, 0:1, :], 0.0)
    xp = jnp.where(row == 0, prow, pltpu.roll(x, 1, 0))
    xn = jnp.where(row == tm - 1, nrow, pltpu.roll(x, tm - 1, 0))
    cw = cw_ref[...]
    y = cw[0:1] * xp + cw[1:2] * x + cw[2:3] * xn
    y = y * jax.nn.sigmoid(y)

    w = N_HEADS_D * DK
    seg = seg_ref[...]
    q = y[:, 0:w]
    q = q * lax.rsqrt(_split_dot(q * q, seg) + EPS) * (DK ** -0.5)
    k = y[:, w:2 * w]
    k = k * lax.rsqrt(_split_dot(k * k, seg) + EPS)
    q_ref[0] = jnp.where(valid, q, 0.0)
    k_ref[0] = jnp.where(valid, k, 0.0)
    v_ref[0] = jnp.where(valid, y[:, 2 * w:3 * w], 0.0)

    sm = sm_ref[0]
    t = sm + dtb_ref[...]
    softplus = jnp.maximum(t, 0.0) + jnp.log1p(jnp.exp(-jnp.abs(t)))
    g_all = jnp.where(valid, -rate_ref[...] * softplus, 0.0)
    beta_all = jnp.where(valid, jax.nn.sigmoid(sm), 0.0)
    gc_f = jnp.dot(trif_ref[...], g_all, preferred_element_type=F32, precision=lax.Precision.HIGHEST)
    gc_b = jnp.dot(trib_ref[...], g_all, preferred_element_type=F32, precision=lax.Precision.HIGHEST)

    lane = lax.broadcasted_iota(jnp.int32, (1, LANES), 1)
    low = lane < DK

    def expand(a, c0, out_ref):
        for p in range(N_HEADS_D // 2):
            e0 = jnp.broadcast_to(a[:, c0 + 2 * p:c0 + 2 * p + 1], (tm, LANES))
            e1 = jnp.broadcast_to(a[:, c0 + 2 * p + 1:c0 + 2 * p + 2], (tm, LANES))
            out_ref[0, :, LANES * p:LANES * (p + 1)] = jnp.where(low, e0, e1)

    expand(gc_f, 0, gcf_ref)
    expand(gc_b, N_HEADS_D, gcb_ref)
    expand(beta_all, 2 * N_HEADS_D, bf_ref)
    expand(beta_all, 3 * N_HEADS_D, bb_ref)


def _delta_prep(qkvd, sm, cw, rate, dtb, seg, trif, trib):
    b, lp, c = qkvd.shape
    tm = ROW_TILE
    nt = lp // tm
    w = N_HEADS_D * DK
    halo = 8
    nb8 = lp // halo
    cur = lambda bi, i: (bi, i, 0)
    const = lambda bi, i: (0, 0)
    out = jax.ShapeDtypeStruct((b, lp, w), F32)
    return pl.pallas_call(
        functools.partial(_dprep_kernel, tm=tm, n_tiles=nt),
        grid=(b, nt),
        in_specs=[
            pl.BlockSpec((1, halo, c), lambda bi, i: (bi, jnp.maximum(i * (tm // halo) - 1, 0), 0)),
            pl.BlockSpec((1, tm, c), cur),
            pl.BlockSpec((1, halo, c), lambda bi, i: (bi, jnp.minimum((i + 1) * (tm // halo), nb8 - 1), 0)),
            pl.BlockSpec((1, tm, LANES), cur),
            pl.BlockSpec((3, c), const),
            pl.BlockSpec((1, LANES), const),
            pl.BlockSpec((1, LANES), const),
            pl.BlockSpec((w, w), const),
            pl.BlockSpec((tm, tm), const),
            pl.BlockSpec((tm, tm), const),
        ],
        out_specs=tuple(pl.BlockSpec((1, tm, w), cur) for _ in range(7)),
        out_shape=(out,) * 7,
        compiler_params=_cparams(("parallel", "parallel")),
        name="delta_prep",
    )(qkvd, qkvd, qkvd, sm, cw, rate, dtb, seg, trif, trib)


def _dscan_kernel(qf_ref, kf_ref, vf_ref, gf_ref, bf_ref, qb_ref, kb_ref, vb_ref, gb_ref, bb_ref,
                  of_ref, ob_ref, s_ref):
    i = pl.program_id(0)

    @pl.when(i == 0)
    def _():
        s_ref[...] = jnp.zeros(s_ref.shape, F32)

    c = CHUNK
    r = lax.broadcasted_iota(jnp.int32, (c, GROUP_W), 0)
    cj = lax.broadcasted_iota(jnp.int32, (c, GROUP_W), 1) % c
    eye = r == cj
    eyef = eye.astype(F32)
    bdmask = (lax.broadcasted_iota(jnp.int32, (GROUP_W, GROUP_W), 0) // c ==
              lax.broadcasted_iota(jnp.int32, (GROUP_W, GROUP_W), 1) // c)

    lane_low = lax.broadcasted_iota(jnp.int32, (c, LANES), 1) < DK
    zero_tile = jnp.zeros((c, LANES), BF16)

    def bd(t):
        rows = []
        for hd in range(HEAD_GROUP):
            tile = t[:, LANES * (hd // 2):LANES * (hd // 2 + 1)]
            tile = jnp.where(lane_low == (hd % 2 == 0), tile, 0.0).astype(BF16)
            rows.append(jnp.concatenate([tile, zero_tile] if hd < 2 else [zero_tile, tile], axis=1))
        return jnp.concatenate(rows, axis=0)

    n_grp = N_HEADS_D // HEAD_GROUP
    fwd_refs = (qf_ref, kf_ref, vf_ref, gf_ref, bf_ref, of_ref)
    bwd_refs = (qb_ref, kb_ref, vb_ref, gb_ref, bb_ref, ob_ref)
    chains = []
    for bi in range(of_ref.shape[0]):
        for grp in range(n_grp):
            for reverse in (False, True):
                sl = slice(GROUP_W * grp, GROUP_W * (grp + 1))
                chains.append((fwd_refs if not reverse else bwd_refs, bi, sl, reverse, len(chains)))
    n_chains = len(chains)
    items = []
    for step in range(CHUNKS_PER_STEP):
        for ch in chains:
            ci = CHUNKS_PER_STEP - 1 - step if ch[3] else step
            items.append(ch + (slice(ci * c, (ci + 1) * c),))

    def each(fn, *cols):
        return [fn(*args) for args in zip(*cols)]

    rev = [it[3] for it in items]
    q = [it[0][0][it[1], it[5], it[2]] for it in items]
    k = [it[0][1][it[1], it[5], it[2]] for it in items]
    v = [it[0][2][it[1], it[5], it[2]] for it in items]
    gc = [it[0][3][it[1], it[5], it[2]] for it in items]
    beta = [it[0][4][it[1], it[5], it[2]] for it in items]

    def decay_of(g, reverse):
        incl = (r <= cj) if reverse else (r >= cj)
        gcol = jnp.sum(jnp.where(eye, g, 0.0), axis=0, keepdims=True)
        return jnp.where(incl, jnp.exp(jnp.where(incl, g - gcol, 0.0)), 0.0)

    decay = each(decay_of, gc, rev)
    eg = each(jnp.exp, gc)
    glast = each(lambda g, reverse: g[0:1] if reverse else g[c - 1:c], gc, rev)
    kbeta = each(lambda a, b: a * b, k, beta)
    gq = each(lambda kb_, q_, k_: _dot_nt(jnp.concatenate([kb_, q_], axis=0).astype(BF16), bd(k_)),
              kbeta, q, k)
    a_intra = each(lambda g, d: g[c:] * d, gq, decay)
    m = each(lambda g, d, reverse: -jnp.where((r < cj) if reverse else (r > cj), g[:c] * d, 0.0),
             gq, decay, rev)
    p = each(lambda m_: eyef + m_, m)
    m = each(lambda m_: _dot(m_.astype(BF16), bd(m_)), m)
    for _ in range(4):
        rr = each(lambda m_, p_: _dot(jnp.concatenate([m_, p_], axis=0).astype(BF16), bd(m_)), m, p)
        p = each(lambda p_, rr_: p_ + rr_[c:], p, rr)
        m = each(lambda rr_: rr_[:c], rr)
    tmat = each(lambda p_, m_: (p_ + _dot(p_.astype(BF16), bd(m_))).astype(BF16), p, m)
    u = each(lambda t, v_, b: _dot(t, bd(v_ * b)), tmat, v, beta)
    w = each(lambda t, kb_, e: _dot(t, bd(kb_ * e)), tmat, kbeta, eg)
    state = [s_ref[ch[4]] for ch in chains]
    for step in range(CHUNKS_PER_STEP):
        part = slice(step * n_chains, (step + 1) * n_chains)
        wq = each(lambda w_, q_, e, s: _dot(jnp.concatenate([w_, q_ * e], axis=0).astype(BF16), s.astype(BF16)),
                  w[part], q[part], eg[part], state)
        v_new = each(lambda u_, wq_: u_ - wq_[:c], u[part], wq)
        o = each(lambda wq_, a, vn: wq_[c:] + _dot(a.astype(BF16), bd(vn)), wq, a_intra[part], v_new)
        upd = each(lambda k_, gl, g, vn: _dot((k_ * jnp.exp(gl - g)).T.astype(BF16), vn.astype(BF16)),
                   k[part], glast[part], gc[part], v_new)
        state = each(lambda s, gl, up: s * jnp.exp(gl) + jnp.where(bdmask, up, 0.0), state, glast[part], upd)
        for it, o_ in zip(items[part], o):
            it[0][5][it[1], it[5], it[2]] = o_
    for ch, s in zip(chains, state):
        s_ref[ch[4]] = s


def _delta_scan(q, k, v, gcf, gcb, bf, bb):
    b, lp, w = q.shape
    n = lp // (CHUNKS_PER_STEP * CHUNK)
    fwd = lambda i: (0, i, 0)
    bwd = lambda i: (0, n - 1 - i, 0)
    blk = (b, CHUNKS_PER_STEP * CHUNK, w)
    out = jax.ShapeDtypeStruct((b, lp, w), F32)
    return pl.pallas_call(
        _dscan_kernel,
        grid=(n,),
        in_specs=[pl.BlockSpec(blk, fwd)] * 5 + [pl.BlockSpec(blk, bwd)] * 5,
        out_specs=(pl.BlockSpec(blk, fwd), pl.BlockSpec(blk, bwd)),
        out_shape=(out, out),
        scratch_shapes=[pltpu.VMEM((b * 2 * N_HEADS_D // HEAD_GROUP, GROUP_W, GROUP_W), F32)],
        compiler_params=_cparams(("arbitrary",)),
        name="delta_scan",
    )(q, k, v, gcf, bf, q, k, v, gcb, bb)


def _post_kernel(h_ref, at_ref, of_ref, ob_ref, z_ref, ga_ref, gd_ref, on_ref, seg_ref,
                 wa_ref, wd_ref, wo_ref, gf_ref, wrh_ref, wrl_ref,
                 h_out, xn_out, aff_out, *, tm, n_tiles):
    od = of_ref[...] + ob_ref[...]
    msd = _split_dot(od * od, seg_ref[...]) * (1.0 / DV)
    z = z_ref[...].astype(F32)
    od = od * lax.rsqrt(msd + EPS) * on_ref[...] * (z * jax.nn.sigmoid(z))
    y_d = _dot(od.astype(BF16), wd_ref[...])
    y_a = _dot(at_ref[...], wa_ref[...])
    merged = (jax.nn.sigmoid(ga_ref[...].astype(F32)) * y_a +
              jax.nn.sigmoid(gd_ref[...].astype(F32)) * y_d)
    h = h_ref[...] + _dot(merged.astype(BF16), wo_ref[...])
    h_out[...] = h

    ms = jnp.mean(h * h, axis=-1, keepdims=True)
    xn = h * lax.rsqrt(ms + EPS) * gf_ref[...]
    xh = xn.astype(BF16)
    xn_out[...] = xh
    xl = (xn - xh.astype(F32)).astype(BF16)
    logits = _dot(xh, wrh_ref[...]) + _dot(xl, wrh_ref[...]) + _dot(xh, wrl_ref[...])
    lane = lax.broadcasted_iota(jnp.int32, (1, LANES), 1)
    logits = jnp.where(lane < N_EXPERTS, logits, -jnp.inf)
    e = jnp.exp(logits - jnp.max(logits, axis=-1, keepdims=True))
    aff = e / jnp.sum(e, axis=-1, keepdims=True)
    row = lax.broadcasted_iota(jnp.int32, (tm, 1), 0) + (pl.program_id(0) % n_tiles) * tm
    aff_out[...] = jnp.where(row >= N_NULL, aff, -1.0)


def _post(h2, at, of, ob, z, ga, gd, on, seg, wa, wd, wo, gf, wrh, wrl, lp):
    n = h2.shape[0]
    tm = ROW_TILE
    nt = lp // tm
    row = lambda i: (i, 0)
    const = lambda i: (0, 0)
    hw = N_HEADS_D * DV

    def wspec(shape):
        return pl.BlockSpec(shape, const, pipeline_mode=pl.Buffered(1))

    return pl.pallas_call(
        functools.partial(_post_kernel, tm=tm, n_tiles=nt),
        grid=(n // tm,),
        in_specs=[
            pl.BlockSpec((tm, D_MODEL), row),
            pl.BlockSpec((tm, hw), row),
            pl.BlockSpec((tm, hw), row),
            pl.BlockSpec((tm, hw), row),
            pl.BlockSpec((tm, hw), row),
            pl.BlockSpec((tm, D_MODEL), row),
            pl.BlockSpec((tm, D_MODEL), row),
            pl.BlockSpec((1, hw), const),
            wspec((hw, hw)),
            wspec((hw, D_MODEL)),
            wspec((hw, D_MODEL)),
            wspec((D_MODEL, D_MODEL)),
            pl.BlockSpec((1, D_MODEL), const),
            wspec((D_MODEL, LANES)),
            wspec((D_MODEL, LANES)),
        ],
        out_specs=(pl.BlockSpec((tm, D_MODEL), row), pl.BlockSpec((tm, D_MODEL), row),
                   pl.BlockSpec((tm, LANES), row)),
        out_shape=(jax.ShapeDtypeStruct((n, D_MODEL), F32), jax.ShapeDtypeStruct((n, D_MODEL), BF16),
                   jax.ShapeDtypeStruct((n, LANES), F32)),
        compiler_params=_cparams(("parallel",)),
        name="merge_out",
    )(h2, at, of, ob, z, ga, gd, on, seg, wa, wd, wo, gf, wrh, wrl)


def _expert_kernel(x_ref, gate_ref, wg_ref, wu_ref, wd_ref, y_ref, wgb_ref, wub_ref, wdb_ref):
    @pl.when(pl.program_id(1) == 0)
    def _():
        wgb_ref[...] = wg_ref[0].astype(BF16)
        wub_ref[...] = wu_ref[0].astype(BF16)
        wdb_ref[...] = wd_ref[0].astype(BF16)

    x = x_ref[0]
    a = _dot(x, wgb_ref[...])
    b = _dot(x, wub_ref[...])
    hh = (a * jax.nn.sigmoid(a) * b).astype(BF16)
    y_ref[0] = _dot(hh, wdb_ref[...]) * gate_ref[0]


def _experts(xe, gates, wg, wu, wd, layer, tm):
    e, cp, d = xe.shape
    tile = lambda ei, i: (ei, i, 0)
    wsp = lambda ei, i: (layer, ei, 0, 0)
    return pl.pallas_call(
        _expert_kernel,
        grid=(e, cp // tm),
        in_specs=[
            pl.BlockSpec((1, tm, d), tile),
            pl.BlockSpec((1, tm, 1), tile),
            pl.BlockSpec((None, 1, d, D_EXPERT), wsp),
            pl.BlockSpec((None, 1, d, D_EXPERT), wsp),
            pl.BlockSpec((None, 1, D_EXPERT, d), wsp),
        ],
        out_specs=pl.BlockSpec((1, tm, d), tile),
        out_shape=jax.ShapeDtypeStruct((e, cp, d), F32),
        scratch_shapes=[pltpu.VMEM((d, D_EXPERT), BF16), pltpu.VMEM((d, D_EXPERT), BF16),
                        pltpu.VMEM((D_EXPERT, d), BF16)],
        compiler_params=_cparams(("parallel", "arbitrary")),
        name="experts",
    )(xe, gates, wg, wu, wd)


ROUTE_CHUNK = 128
ROUTE_WINDOW = 2 * ROUTE_CHUNK


def _route_select_kernel(aff_ref, tri_ref, pos_ref, off_ref, *, cap):
    n = aff_ref.shape[0]
    rc = ROUTE_CHUNK
    n_chunks = n // rc

    def bits_of(c):
        return pltpu.bitcast(aff_ref[pl.ds(pl.multiple_of(c * rc, rc), rc), :], jnp.int32)

    def count(pred):
        cb = ROW_TILE

        def body(c, acc):
            blk = pltpu.bitcast(aff_ref[pl.ds(pl.multiple_of(c * cb, cb), cb), :], jnp.int32)
            return acc + jnp.sum(pred(blk).astype(jnp.int32), axis=0, keepdims=True)
        return lax.fori_loop(0, n // cb, body, jnp.zeros((1, LANES), jnp.int32))

    def search(i, thr):
        cand = thr | jnp.left_shift(jnp.int32(1), 30 - i)
        return jnp.where(count(lambda b: b >= cand) >= cap, cand, thr)

    thr = lax.fori_loop(0, 31, search, jnp.zeros((1, LANES), jnp.int32))
    need = (cap - count(lambda b: b > thr)).astype(F32)
    tri = tri_ref[...]

    def emit(c, carry):
        ties_before, picks_before = carry
        b = bits_of(c)
        tie = b == thr
        tie_incl = _dot(tri, jnp.where(tie, 1.0, 0.0).astype(BF16)) + ties_before
        pick = (b > thr) | (tie & (tie_incl - 1.0 < need))
        pick_incl = _dot(tri, jnp.where(pick, 1.0, 0.0).astype(BF16)) + picks_before
        pos_ref[pl.ds(pl.multiple_of(c * rc, rc), rc), :] = jnp.where(pick, pick_incl - 1.0, -1.0)
        off_ref[c] = picks_before.astype(jnp.int32)
        return tie_incl[rc - 1:rc], pick_incl[rc - 1:rc]

    zero = jnp.zeros((1, LANES), F32)
    lax.fori_loop(0, n_chunks, emit, (zero, zero))


def _route_compact_kernel(off_ref, pos_ref, aff_ref, idx_ref, gate_ref):
    c = pl.program_id(0)
    rc = ROUTE_CHUNK

    @pl.when(c == 0)
    def _():
        idx_ref[...] = jnp.zeros(idx_ref.shape, F32)
        gate_ref[...] = jnp.zeros(gate_ref.shape, F32)

    token = (lax.broadcasted_iota(jnp.int32, (rc, 1), 0) + c * rc).astype(F32)
    slot = lax.broadcasted_iota(jnp.int32, (1, ROUTE_WINDOW), 1).astype(F32)
    pos = pos_ref[...]
    aff = aff_ref[...]
    for e in range(N_EXPERTS):
        start = pl.multiple_of((off_ref[c * N_EXPERTS + e] // LANES) * LANES, LANES)
        hit = (pos[:, e:e + 1] - start.astype(F32)) == slot
        win = (slice(e, e + 1), pl.ds(start, ROUTE_WINDOW))
        idx_ref[win] += jnp.sum(jnp.where(hit, token, 0.0), axis=0, keepdims=True)
        gate_ref[win] += jnp.sum(jnp.where(hit, aff[:, e:e + 1], 0.0), axis=0, keepdims=True)


def _route(aff, cap, slots):
    n = aff.shape[0]
    rc = ROUTE_CHUNK
    n_chunks = n // rc
    tri = jnp.asarray(np.tril(np.ones((rc, rc), np.float32)), BF16)
    whole = pl.BlockSpec(memory_space=pltpu.VMEM)
    pos, off = pl.pallas_call(
        functools.partial(_route_select_kernel, cap=cap),
        in_specs=[whole, whole],
        out_specs=(whole, whole),
        out_shape=(jax.ShapeDtypeStruct((n, LANES), F32),
                   jax.ShapeDtypeStruct((n_chunks, 1, LANES), jnp.int32)),
        compiler_params=pltpu.CompilerParams(vmem_limit_bytes=VMEM_LIMIT),
        name="route_select",
    )(aff, tri)
    off = off[:, 0, :N_EXPERTS].reshape(-1)
    chunk = lambda c, off_ref: (c, 0)
    fixed = lambda c, off_ref: (0, 0)
    out = jax.ShapeDtypeStruct((N_EXPERTS, slots), F32)
    idx, gates = pl.pallas_call(
        _route_compact_kernel,
        grid_spec=pltpu.PrefetchScalarGridSpec(
            num_scalar_prefetch=1,
            grid=(n_chunks,),
            in_specs=[pl.BlockSpec((rc, LANES), chunk), pl.BlockSpec((rc, LANES), chunk)],
            out_specs=(pl.BlockSpec((N_EXPERTS, slots), fixed), pl.BlockSpec((N_EXPERTS, slots), fixed)),
        ),
        out_shape=(out, out),
        compiler_params=_cparams(("arbitrary",)),
        name="route_compact",
    )(off, pos, aff)
    return idx.astype(jnp.int32), gates


def _final_kernel(*refs):
    *x_refs, g_ref, o_ref = refs
    tm = x_refs[0].shape[1]
    for i, x_ref in enumerate(x_refs):
        x = x_ref[0]
        ms = jnp.mean(x * x, axis=-1, keepdims=True)
        o_ref[0, i * tm:(i + 1) * tm, :] = x * lax.rsqrt(ms + EPS) * g_ref[...]


def _final_norm(h3, g, s):
    b = h3.shape[0]
    tm = FRONT
    per_step = 4 if s % (4 * tm) == 0 else 1

    def in_spec(k):
        return pl.BlockSpec((1, tm, D_MODEL), lambda bi, i: (bi, per_step * i + k + 1, 0))

    return pl.pallas_call(
        _final_kernel,
        grid=(b, s // (per_step * tm)),
        in_specs=[in_spec(k) for k in range(per_step)] + [pl.BlockSpec((1, D_MODEL), lambda bi, i: (0, 0))],
        out_specs=pl.BlockSpec((1, per_step * tm, D_MODEL), lambda bi, i: (bi, i, 0)),
        out_shape=jax.ShapeDtypeStruct((b, s, D_MODEL), F32),
        compiler_params=_cparams(("parallel", "parallel")),
        name="final_norm",
    )(*([h3] * per_step), g)


def _rope_tables(s):
    lp = FRONT + s
    t = np.arange(lp) - FRONT
    real = t >= 0
    pos = np.stack([np.where(real, t // GRID_W, 0), np.where(real, t % GRID_W, 0)], axis=-1)
    n_freq = HEAD_DIM // 4
    inv_freq = jnp.asarray(ROPE_THETA, F32) ** (-jnp.arange(n_freq, dtype=F32) / n_freq)
    lane = np.arange(LANES) % HEAD_DIM
    axis = lane // (HEAD_DIM // 2)
    freq = lane % n_freq
    ang = jnp.asarray(pos, F32)[:, axis] * inv_freq[freq][None, :]
    sign = np.where((lane % (HEAD_DIM // 2)) < n_freq, -1.0, 1.0).astype(np.float32)
    return jnp.cos(ang), jnp.sin(ang) * sign[None, :]


def _pack_w_in(w):
    o = 0

    def take(n):
        nonlocal o
        part = w[:, o:o + n]
        o += n
        return part

    def head_blocks(part, n_heads):
        part = part.reshape(D_MODEL, n_heads, HEAD_DIM)
        part = jnp.pad(part, ((0, 0), (0, 0), (0, LANES - HEAD_DIM)))
        return part.reshape(D_MODEL, n_heads * LANES)

    q_a = head_blocks(take(N_HEADS_A * HEAD_DIM), N_HEADS_A)
    k_a = head_blocks(take(N_KV_HEADS * HEAD_DIM), N_KV_HEADS)
    v_a = take(N_KV_HEADS * HEAD_DIM).reshape(D_MODEL, N_KV_HEADS, HEAD_DIM)
    v_a = jnp.pad(v_a, ((0, 0), (0, 0), (0, V_ROWS - HEAD_DIM))).reshape(D_MODEL, N_KV_HEADS * V_ROWS)
    qkv_d = take(3 * N_HEADS_D * DK)
    z_d = take(N_HEADS_D * DV)
    small = take(4 * N_HEADS_D)
    gate_a = take(D_MODEL)
    gate_d = take(D_MODEL)
    small = jnp.pad(small, ((0, 0), (0, LANES - small.shape[1])))
    w_all = jnp.concatenate([q_a, k_a, qkv_d, z_d, small, gate_a, gate_d], axis=1).astype(BF16)
    return w_all, v_a.T.astype(BF16)


def _chunk_tri(tm, reverse):
    i = np.arange(tm)
    same = (i[:, None] // CHUNK) == (i[None, :] // CHUNK)
    tri = (i[None, :] >= i[:, None]) if reverse else (i[None, :] <= i[:, None])
    return jnp.asarray((same & tri).astype(np.float32))


def _expert_tile(cap):
    n_tiles = -(-cap // 512)
    tm = -(-cap // n_tiles)
    tm = -(-tm // 16) * 16
    return tm, n_tiles


def _trunk(x, meta_tokens, layers, ffn_w, norm_final):
    b, s, d = x.shape
    lp = FRONT + s
    n = b * lp
    n_tok = b * (N_META + s)
    cap = EC_CAPACITY * n_tok // N_EXPERTS
    etm, ent = _expert_tile(cap)
    cap_pad = etm * ent
    slots = -(-max(cap_pad, cap + ROUTE_WINDOW) // LANES) * LANES
    cos, sin = _rope_tables(s)
    seg = jnp.asarray(np.kron(np.eye(N_HEADS_D), np.ones((DK, DK))), BF16)
    trif = _chunk_tri(ROW_TILE, False)
    trib = _chunk_tri(ROW_TILE, True)

    front = jnp.concatenate([jnp.zeros((N_NULL, d), F32), meta_tokens.astype(F32)], axis=0)
    h = jnp.concatenate([jnp.broadcast_to(front[None], (b, FRONT, d)), x], axis=1).reshape(n, d)

    for li, lw in enumerate(layers):
        q, k, vt, qkvd, z, sm, ga, gd = _inproj(h, lw["norm_mix"], lw["w_all"], lw["wvt"], cos, sin,
                                                lw["qn"], lw["kn"], lp)
        at = _attention(q.reshape(b, lp, -1), k.reshape(b, lp, -1), vt, s)
        qd, kd, vd, gcf, gcb, bf, bb = _delta_prep(qkvd.reshape(b, lp, -1), sm.reshape(b, lp, -1),
                                                   lw["conv_w"], lw["rate"], lw["dtb"], seg, trif, trib)
        of, ob = _delta_scan(qd, kd, vd, gcf, gcb, bf, bb)
        hw = N_HEADS_D * DV
        h, xn, aff = _post(h, at.reshape(n, hw), of.reshape(n, hw), ob.reshape(n, hw), z, ga, gd,
                           lw["on"], seg, lw["w_attn_proj"], lw["w_delta_proj"], lw["w_out"],
                           lw["norm_ffn"], lw["wr_hi"], lw["wr_lo"], lp)
        idx, gates = _route(aff, cap, slots)
        idx = idx[:, :cap_pad]
        gates = gates[:, :cap_pad]
        xe = xn.at[idx.reshape(-1)].get(mode="promise_in_bounds").reshape(N_EXPERTS, cap_pad, d)
        ye = _experts(xe, gates[..., None], ffn_w[0], ffn_w[1], ffn_w[2], li, etm)
        h = h.at[idx.reshape(-1)].add(ye.reshape(-1, d), mode="promise_in_bounds")

    return _final_norm(h.reshape(b, lp, d), norm_final, s)


def kernel(x_prompt, x_sample, meta_tokens, norm_mix, w_in, q_norm, k_norm, conv_w, a_log, dt_bias,
           o_norm, w_attn_proj, w_delta_proj, w_out, norm_ffn, w_router, w_gate, w_up, w_down,
           norm_final):
    depth = w_in.shape[0]
    layers = []
    for l in range(depth):
        rate = jnp.exp(a_log[l].astype(F32)).reshape(1, -1)
        dtb = dt_bias[l].astype(F32).reshape(1, -1)
        pad = LANES - rate.shape[1]
        wr = jnp.pad(w_router[l].astype(F32), ((0, 0), (0, LANES - N_EXPERTS)))
        wr_hi = wr.astype(BF16)
        w_all, wvt = _pack_w_in(w_in[l])
        layers.append(dict(
            norm_mix=norm_mix[l].astype(F32).reshape(1, -1),
            w_all=w_all,
            wvt=wvt,
            qn=jnp.tile(q_norm[l].astype(F32), LANES // HEAD_DIM).reshape(1, -1),
            kn=jnp.tile(k_norm[l].astype(F32), LANES // HEAD_DIM).reshape(1, -1),
            conv_w=conv_w[l].astype(F32),
            rate=jnp.pad(rate, ((0, 0), (0, pad))),
            dtb=jnp.pad(dtb, ((0, 0), (0, pad))),
            on=jnp.tile(o_norm[l].astype(F32), N_HEADS_D).reshape(1, -1),
            w_attn_proj=w_attn_proj[l].astype(BF16),
            w_delta_proj=w_delta_proj[l].astype(BF16),
            w_out=w_out[l].astype(BF16),
            norm_ffn=norm_ffn[l].astype(F32).reshape(1, -1),
            wr_hi=wr_hi,
            wr_lo=(wr - wr_hi.astype(F32)).astype(BF16),
        ))
    ffn_w = (w_gate.astype(F32), w_up.astype(F32), w_down.astype(F32))
    nf = norm_final.astype(F32).reshape(1, -1)
    y_prompt = _trunk(x_prompt, meta_tokens, layers, ffn_w, nf)
    y_sample = _trunk(x_sample, meta_tokens, layers, ffn_w, nf)
    return (y_prompt, y_sample)
```

```python
import functools

import numpy as np
import jax
import jax.numpy as jnp
from jax import lax
from jax.experimental import pallas as pl
from jax.experimental.pallas import tpu as pltpu

F32 = jnp.float32
BF16 = jnp.bfloat16

D_MODEL = 1024
N_META = 16
GRID_W = 64
N_HEADS_A = 8
N_KV_HEADS = 2
HEAD_DIM = 64
ROPE_THETA = 10000.0
N_HEADS_D = 8
DK = 64
DV = 64
CHUNK = 64
N_EXPERTS = 16
EC_CAPACITY = 2
D_EXPERT = 1024
EPS = 1e-6

FRONT = 128
N_NULL = FRONT - N_META
LANES = 128
ROW_TILE = 384
HEAD_GROUP = 4
GROUP_W = HEAD_GROUP * DK
CHUNKS_PER_STEP = 2
VMEM_LIMIT = 48 * 1024 * 1024
Q_SCALE = HEAD_DIM ** -0.5 * float(np.log2(np.e))
PAIRS_PER_TRIP = 3
LANE_SLICES = 2
V_ROWS = HEAD_DIM + 16

C_Q = 0
C_K = C_Q + N_HEADS_A * LANES
C_QKVD = C_K + N_KV_HEADS * LANES
C_Z = C_QKVD + 3 * N_HEADS_D * DK
C_SM = C_Z + N_HEADS_D * DV
C_GA = C_SM + LANES
C_GD = C_GA + D_MODEL
C_END = C_GD + D_MODEL


def _cparams(sem):
    return pltpu.CompilerParams(dimension_semantics=sem, vmem_limit_bytes=VMEM_LIMIT)


def _dot(a, b):
    return jnp.dot(a, b, preferred_element_type=F32)


def _dot_nt(a, b):
    return lax.dot_general(a, b, (((1,), (1,)), ((), ())), preferred_element_type=F32)


def _split_dot(t, w_bf16):
    hi = t.astype(BF16)
    lo = (t - hi.astype(F32)).astype(BF16)
    return _dot(hi, w_bf16) + _dot(lo, w_bf16)


def _inproj_kernel(x_ref, g_ref, w_ref, wvt_ref, cos_ref, sin_ref, qn_ref, kn_ref,
                   q_ref, k_ref, vt_ref, qkvd_ref, z_ref, sm_ref, ga_ref, gd_ref):
    x = x_ref[...]
    ms = jnp.mean(x * x, axis=-1, keepdims=True)
    u = (x * lax.rsqrt(ms + EPS) * g_ref[...]).astype(BF16)
    cos = cos_ref[...]
    sin = sin_ref[...]
    lane = lax.broadcasted_iota(jnp.int32, (1, LANES), 1)
    first = (lane % 32) < 16

    def rope(t):
        rot = jnp.where(first, pltpu.roll(t, LANES - 16, 1), pltpu.roll(t, 16, 1))
        return t * cos + rot * sin

    def mm(c0, n):
        return _dot(u, w_ref[:, c0:c0 + n])

    def head(t, gain):
        msq = jnp.sum(t * t, axis=-1, keepdims=True) * (1.0 / HEAD_DIM)
        return rope(t * lax.rsqrt(msq + EPS) * gain)

    for hp in range(N_HEADS_A // 2):
        t2 = mm(C_Q + 2 * LANES * hp, 2 * LANES)
        for i in range(2):
            h = 2 * hp + i
            q_ref[:, LANES * h:LANES * (h + 1)] = (
                head(t2[:, LANES * i:LANES * (i + 1)], qn_ref[...]) * Q_SCALE).astype(BF16)
    t2 = mm(C_K, N_KV_HEADS * LANES)
    for j in range(N_KV_HEADS):
        k_ref[:, LANES * j:LANES * (j + 1)] = head(t2[:, LANES * j:LANES * (j + 1)], kn_ref[...]).astype(BF16)

    vt = _dot_nt(wvt_ref[...], u)
    vrow = lax.broadcasted_iota(jnp.int32, (N_KV_HEADS * V_ROWS, 1), 0) % V_ROWS
    vt_ref[...] = jnp.where(vrow < HEAD_DIM, vt, 1.0).astype(BF16)
    qkvd_ref[...] = mm(C_QKVD, C_Z - C_QKVD)
    z_ref[...] = mm(C_Z, C_SM - C_Z).astype(BF16)
    sm_ref[...] = mm(C_SM, LANES)
    ga_ref[...] = mm(C_GA, D_MODEL).astype(BF16)
    gd_ref[...] = mm(C_GD, D_MODEL).astype(BF16)


def _inproj(h2, g, w_all, wvt, cos, sin, qn, kn, lp):
    n = h2.shape[0]
    tm = ROW_TILE
    nt = lp // tm
    row = lambda i: (i, 0)
    const = lambda i: (0, 0)
    tab = lambda i: (i % nt, 0)
    out_shapes = (
        jax.ShapeDtypeStruct((n, N_HEADS_A * LANES), BF16),
        jax.ShapeDtypeStruct((n, N_KV_HEADS * LANES), BF16),
        jax.ShapeDtypeStruct((N_KV_HEADS * V_ROWS, n), BF16),
        jax.ShapeDtypeStruct((n, C_Z - C_QKVD), F32),
        jax.ShapeDtypeStruct((n, C_SM - C_Z), BF16),
        jax.ShapeDtypeStruct((n, LANES), F32),
        jax.ShapeDtypeStruct((n, D_MODEL), BF16),
        jax.ShapeDtypeStruct((n, D_MODEL), BF16),
    )
    out_specs = [pl.BlockSpec((tm, s.shape[1]), row) for s in out_shapes]
    out_specs[2] = pl.BlockSpec((N_KV_HEADS * V_ROWS, tm), lambda i: (0, i))
    return pl.pallas_call(
        _inproj_kernel,
        grid=(n // tm,),
        in_specs=[
            pl.BlockSpec((tm, D_MODEL), row),
            pl.BlockSpec((1, D_MODEL), const),
            pl.BlockSpec((D_MODEL, C_END), const, pipeline_mode=pl.Buffered(1)),
            pl.BlockSpec((N_KV_HEADS * V_ROWS, D_MODEL), const, pipeline_mode=pl.Buffered(1)),
            pl.BlockSpec((tm, LANES), tab),
            pl.BlockSpec((tm, LANES), tab),
            pl.BlockSpec((1, LANES), const),
            pl.BlockSpec((1, LANES), const),
        ],
        out_specs=tuple(out_specs),
        out_shape=out_shapes,
        compiler_params=_cparams(("parallel",)),
        name="inproj",
    )(h2, g, w_all, wvt, cos, sin, qn, kn)


def _flash_kernel(q_ref, k_ref, vt_ref, o_ref, qs_ref, m_ref, acc_ref, sa_ref, sb_ref, ca_ref, cb_ref,
                  *, tq, tk, s_real):
    g = N_HEADS_A // N_KV_HEADS
    r = g * tq
    for i in range(g):
        qs_ref[i * tq:(i + 1) * tq, :] = q_ref[0, :, LANES * i:LANES * (i + 1)]
    m_ref[...] = jnp.full(m_ref.shape, -jnp.inf, F32)
    acc_ref[...] = jnp.zeros(acc_ref.shape, F32)

    def scores(koff, size):
        return _dot_nt(k_ref[0, pl.ds(koff, size), :], qs_ref[...])

    def absorb(s, cmax, koff, size):
        m_prev = m_ref[...]
        m_new = jnp.maximum(m_prev, cmax)
        alpha = jnp.exp2(m_prev - m_new)
        p = jnp.exp2(s - m_new).astype(BF16)
        acc_ref[...] = alpha * acc_ref[...] + _dot(vt_ref[:, pl.ds(koff, size)], p)
        m_ref[...] = m_new

    def produce(koff, s_ref, c_ref):
        s = scores(koff, tk)
        s_ref[...] = s
        c_ref[...] = jnp.max(s, axis=0, keepdims=True)

    def main_off(c):
        off = FRONT + c * tk
        return off if isinstance(c, int) else pl.multiple_of(off, LANES)

    s = scores(0, FRONT)
    s = jnp.where(lax.broadcasted_iota(jnp.int32, (FRONT, 1), 0) >= N_NULL, s, -jnp.inf)

    n_main = s_real // tk
    if n_main:
        produce(main_off(0), sa_ref, ca_ref)
    absorb(s, jnp.max(s, axis=0, keepdims=True), 0, FRONT)
    if n_main:
        n_pairs = (n_main - 1) // 2

        hw = r // LANE_SLICES

        def step(koff_next, nxt, koff_cur, cur):
            (sn_ref, cn_ref), (sc_ref, cc_ref) = nxt, cur
            for hh in range(LANE_SLICES):
                sl = slice(hh * hw, (hh + 1) * hw)
                s_new = _dot_nt(k_ref[0, pl.ds(koff_next, tk), :], qs_ref[sl, :])
                sn_ref[:, sl] = s_new
                cn_ref[:, sl] = jnp.max(s_new, axis=0, keepdims=True)
                m_prev = m_ref[:, sl]
                m_new = jnp.maximum(m_prev, cc_ref[:, sl])
                alpha = jnp.exp2(m_prev - m_new)
                p = jnp.exp2(sc_ref[:, sl] - m_new).astype(BF16)
                acc_ref[:, sl] = alpha * acc_ref[:, sl] + _dot(vt_ref[:, pl.ds(koff_cur, tk)], p)
                m_ref[:, sl] = m_new

        def pair(pi):
            buf_a, buf_b = (sa_ref, ca_ref), (sb_ref, cb_ref)
            step(main_off(2 * pi + 1), buf_b, main_off(2 * pi), buf_a)
            step(main_off(2 * pi + 2), buf_a, main_off(2 * pi + 1), buf_b)

        def body(t, carry):
            for u in range(PAIRS_PER_TRIP):
                pair(t * PAIRS_PER_TRIP + u)
            return carry

        lax.fori_loop(0, n_pairs // PAIRS_PER_TRIP, body, 0)
        for pi in range(n_pairs - n_pairs % PAIRS_PER_TRIP, n_pairs):
            pair(pi)
        if n_main - 2 * n_pairs == 2:
            step(main_off(2 * n_pairs + 1), (sb_ref, cb_ref), main_off(2 * n_pairs), (sa_ref, ca_ref))
            absorb(sb_ref[...], cb_ref[...], main_off(2 * n_pairs + 1), tk)
        else:
            absorb(sa_ref[...], ca_ref[...], main_off(2 * n_pairs), tk)
    if s_real % tk:
        s = scores(FRONT + n_main * tk, s_real % tk)
        absorb(s, jnp.max(s, axis=0, keepdims=True), FRONT + n_main * tk, s_real % tk)

    acc = acc_ref[...]
    out_t = acc[:HEAD_DIM] / acc[HEAD_DIM:HEAD_DIM + 1]
    for p in range(g // 2):
        pair = jnp.concatenate([out_t[:, (2 * p) * tq:(2 * p + 1) * tq],
                                out_t[:, (2 * p + 1) * tq:(2 * p + 2) * tq]], axis=0)
        o_ref[0, :, 2 * HEAD_DIM * p:2 * HEAD_DIM * (p + 1)] = pair.T.astype(BF16)


def _attention(q, k, vt, s_real):
    b, lp, _ = q.shape
    tq = ROW_TILE
    tk = 512
    g = N_HEADS_A // N_KV_HEADS
    return pl.pallas_call(
        functools.partial(_flash_kernel, tq=tq, tk=tk, s_real=s_real),
        grid=(b, N_KV_HEADS, lp // tq),
        in_specs=[
            pl.BlockSpec((1, tq, g * LANES), lambda bi, j, qi: (bi, qi, j)),
            pl.BlockSpec((1, lp, LANES), lambda bi, j, qi: (bi, 0, j)),
            pl.BlockSpec((V_ROWS, lp), lambda bi, j, qi: (j, bi)),
        ],
        out_specs=pl.BlockSpec((1, tq, g * HEAD_DIM), lambda bi, j, qi: (bi, qi, j)),
        out_shape=jax.ShapeDtypeStruct((b, lp, N_HEADS_A * HEAD_DIM), BF16),
        scratch_shapes=[
            pltpu.VMEM((g * tq, LANES), BF16),
            pltpu.VMEM((1, g * tq), F32),
            pltpu.VMEM((V_ROWS, g * tq), F32),
            pltpu.VMEM((tk, g * tq), F32),
            pltpu.VMEM((tk, g * tq), F32),
            pltpu.VMEM((1, g * tq), F32),
            pltpu.VMEM((1, g * tq), F32),
        ],
        compiler_params=_cparams(("parallel", "parallel", "arbitrary")),
        name="attention",
    )(q, k, vt)


def _dprep_kernel(prev_ref, cur_ref, next_ref, sm_ref, cw_ref, rate_ref, dtb_ref, seg_ref,
                  trif_ref, trib_ref, q_ref, k_ref, v_ref, gcf_ref, gcb_ref, bf_ref, bb_ref,
                  *, tm, n_tiles):
    i = pl.program_id(1)
    row = lax.broadcasted_iota(jnp.int32, (tm, 1), 0)
    valid = (row + i * tm) >= N_NULL
    x = jnp.where(valid, cur_ref[0], 0.0)
    prow = jnp.where(i > 0, prev_ref[0, 7:8, :], 0.0)
    nrow = jnp.where(i < n_tiles - 1, next_ref[0, 0:1, :], 0.0)
    xp = jnp.where(row == 0, prow, pltpu.roll(x, 1, 0))
    xn = jnp.where(row == tm - 1, nrow, pltpu.roll(x, tm - 1, 0))
    cw = cw_ref[...]
    y = cw[0:1] * xp + cw[1:2] * x + cw[2:3] * xn
    y = y * jax.nn.sigmoid(y)

    w = N_HEADS_D * DK
    seg = seg_ref[...]
    q = y[:, 0:w]
    q = q * lax.rsqrt(_split_dot(q * q, seg) + EPS) * (DK ** -0.5)
    k = y[:, w:2 * w]
    k = k * lax.rsqrt(_split_dot(k * k, seg) + EPS)
    q_ref[0] = jnp.where(valid, q, 0.0)
    k_ref[0] = jnp.where(valid, k, 0.0)
    v_ref[0] = jnp.where(valid, y[:, 2 * w:3 * w], 0.0)

    sm = sm_ref[0]
    t = sm + dtb_ref[...]
    softplus = jnp.maximum(t, 0.0) + jnp.log1p(jnp.exp(-jnp.abs(t)))
    g_all = jnp.where(valid, -rate_ref[...] * softplus, 0.0)
    beta_all = jnp.where(valid, jax.nn.sigmoid(sm), 0.0)
    g1 = g_all.astype(BF16)
    r1 = g_all - g1.astype(F32)
    g2 = r1.astype(BF16)
    g3 = (r1 - g2.astype(F32)).astype(BF16)
    trif = trif_ref[...]
    trib = trib_ref[...]
    gc_f = _dot(trif, g1) + _dot(trif, g2) + _dot(trif, g3)
    gc_b = _dot(trib, g1) + _dot(trib, g2) + _dot(trib, g3)

    lane = lax.broadcasted_iota(jnp.int32, (1, LANES), 1)
    low = lane < DK

    def expand(a, c0, out_ref):
        for p in range(N_HEADS_D // 2):
            e0 = jnp.broadcast_to(a[:, c0 + 2 * p:c0 + 2 * p + 1], (tm, LANES))
            e1 = jnp.broadcast_to(a[:, c0 + 2 * p + 1:c0 + 2 * p + 2], (tm, LANES))
            out_ref[0, :, LANES * p:LANES * (p + 1)] = jnp.where(low, e0, e1)

    expand(gc_f, 0, gcf_ref)
    expand(gc_b, N_HEADS_D, gcb_ref)
    expand(beta_all, 2 * N_HEADS_D, bf_ref)
    expand(beta_all, 3 * N_HEADS_D, bb_ref)


def _delta_prep(qkvd, sm, cw, rate, dtb, seg, trif, trib):
    b, lp, c = qkvd.shape
    tm = ROW_TILE
    nt = lp // tm
    w = N_HEADS_D * DK
    halo = 8
    nb8 = lp // halo
    cur = lambda bi, i: (bi, i, 0)
    const = lambda bi, i: (0, 0)
    out = jax.ShapeDtypeStruct((b, lp, w), F32)
    return pl.pallas_call(
        functools.partial(_dprep_kernel, tm=tm, n_tiles=nt),
        grid=(b, nt),
        in_specs=[
            pl.BlockSpec((1, halo, c), lambda bi, i: (bi, jnp.maximum(i * (tm // halo) - 1, 0), 0)),
            pl.BlockSpec((1, tm, c), cur),
            pl.BlockSpec((1, halo, c), lambda bi, i: (bi, jnp.minimum((i + 1) * (tm // halo), nb8 - 1), 0)),
            pl.BlockSpec((1, tm, LANES), cur),
            pl.BlockSpec((3, c), const),
            pl.BlockSpec((1, LANES), const),
            pl.BlockSpec((1, LANES), const),
            pl.BlockSpec((w, w), const),
            pl.BlockSpec((tm, tm), const),
            pl.BlockSpec((tm, tm), const),
        ],
        out_specs=tuple(pl.BlockSpec((1, tm, w), cur) for _ in range(7)),
        out_shape=(out,) * 7,
        compiler_params=_cparams(("parallel", "parallel")),
        name="delta_prep",
    )(qkvd, qkvd, qkvd, sm, cw, rate, dtb, seg, trif, trib)


def _dscan_kernel(qf_ref, kf_ref, vf_ref, gf_ref, bf_ref, qb_ref, kb_ref, vb_ref, gb_ref, bb_ref,
                  of_ref, ob_ref, s_ref):
    i = pl.program_id(0)

    @pl.when(i == 0)
    def _():
        s_ref[...] = jnp.zeros(s_ref.shape, F32)

    c = CHUNK
    r = lax.broadcasted_iota(jnp.int32, (c, GROUP_W), 0)
    cj = lax.broadcasted_iota(jnp.int32, (c, GROUP_W), 1) % c
    eye = r == cj
    eyef = eye.astype(F32)
    bdmask = (lax.broadcasted_iota(jnp.int32, (GROUP_W, GROUP_W), 0) // c ==
              lax.broadcasted_iota(jnp.int32, (GROUP_W, GROUP_W), 1) // c)

    lane_low = lax.broadcasted_iota(jnp.int32, (c, LANES), 1) < DK
    zero_tile = jnp.zeros((c, LANES), BF16)

    def bd(t):
        rows = []
        for hd in range(HEAD_GROUP):
            tile = t[:, LANES * (hd // 2):LANES * (hd // 2 + 1)]
            tile = jnp.where(lane_low == (hd % 2 == 0), tile, 0.0).astype(BF16)
            rows.append(jnp.concatenate([tile, zero_tile] if hd < 2 else [zero_tile, tile], axis=1))
        return jnp.concatenate(rows, axis=0)

    n_grp = N_HEADS_D // HEAD_GROUP
    fwd_refs = (qf_ref, kf_ref, vf_ref, gf_ref, bf_ref, of_ref)
    bwd_refs = (qb_ref, kb_ref, vb_ref, gb_ref, bb_ref, ob_ref)
    chains = []
    for bi in range(of_ref.shape[0]):
        for grp in range(n_grp):
            for reverse in (False, True):
                sl = slice(GROUP_W * grp, GROUP_W * (grp + 1))
                chains.append((fwd_refs if not reverse else bwd_refs, bi, sl, reverse, len(chains)))
    n_chains = len(chains)
    items = []
    for step in range(CHUNKS_PER_STEP):
        for ch in chains:
            ci = CHUNKS_PER_STEP - 1 - step if ch[3] else step
            items.append(ch + (slice(ci * c, (ci + 1) * c),))

    def each(fn, *cols):
        return [fn(*args) for args in zip(*cols)]

    rev = [it[3] for it in items]
    q = [it[0][0][it[1], it[5], it[2]] for it in items]
    k = [it[0][1][it[1], it[5], it[2]] for it in items]
    v = [it[0][2][it[1], it[5], it[2]] for it in items]
    gc = [it[0][3][it[1], it[5], it[2]] for it in items]
    beta = [it[0][4][it[1], it[5], it[2]] for it in items]

    def decay_of(g, reverse):
        incl = (r <= cj) if reverse else (r >= cj)
        gcol = jnp.sum(jnp.where(eye, g, 0.0), axis=0, keepdims=True)
        return jnp.where(incl, jnp.exp(jnp.where(incl, g - gcol, 0.0)), 0.0)

    decay = each(decay_of, gc, rev)
    eg = each(jnp.exp, gc)
    glast = each(lambda g, reverse: g[0:1] if reverse else g[c - 1:c], gc, rev)
    kbeta = each(lambda a, b: a * b, k, beta)
    gq = each(lambda kb_, q_, k_: _dot_nt(jnp.concatenate([kb_, q_], axis=0).astype(BF16), bd(k_)),
              kbeta, q, k)
    a_intra = each(lambda g, d: g[c:] * d, gq, decay)
    m = each(lambda g, d, reverse: -jnp.where((r < cj) if reverse else (r > cj), g[:c] * d, 0.0),
             gq, decay, rev)
    p = each(lambda m_: eyef + m_, m)
    m = each(lambda m_: _dot(m_.astype(BF16), bd(m_)), m)
    for _ in range(4):
        rr = each(lambda m_, p_: _dot(jnp.concatenate([m_, p_], axis=0).astype(BF16), bd(m_)), m, p)
        p = each(lambda p_, rr_: p_ + rr_[c:], p, rr)
        m = each(lambda rr_: rr_[:c], rr)
    tmat = each(lambda p_, m_: (p_ + _dot(p_.astype(BF16), bd(m_))).astype(BF16), p, m)
    u = each(lambda t, v_, b: _dot(t, bd(v_ * b)), tmat, v, beta)
    w = each(lambda t, kb_, e: _dot(t, bd(kb_ * e)), tmat, kbeta, eg)
    state = [s_ref[ch[4]] for ch in chains]
    for step in range(CHUNKS_PER_STEP):
        part = slice(step * n_chains, (step + 1) * n_chains)
        wq = each(lambda w_, q_, e, s: _dot(jnp.concatenate([w_, q_ * e], axis=0).astype(BF16), s.astype(BF16)),
                  w[part], q[part], eg[part], state)
        v_new = each(lambda u_, wq_: u_ - wq_[:c], u[part], wq)
        o = each(lambda wq_, a, vn: wq_[c:] + _dot(a.astype(BF16), bd(vn)), wq, a_intra[part], v_new)
        upd = each(lambda k_, gl, g, vn: _dot((k_ * jnp.exp(gl - g)).T.astype(BF16), vn.astype(BF16)),
                   k[part], glast[part], gc[part], v_new)
        state = each(lambda s, gl, up: s * jnp.exp(gl) + jnp.where(bdmask, up, 0.0), state, glast[part], upd)
        for it, o_ in zip(items[part], o):
            it[0][5][it[1], it[5], it[2]] = o_
    for ch, s in zip(chains, state):
        s_ref[ch[4]] = s


def _delta_scan(q, k, v, gcf, gcb, bf, bb):
    b, lp, w = q.shape
    n = lp // (CHUNKS_PER_STEP * CHUNK)
    fwd = lambda i: (0, i, 0)
    bwd = lambda i: (0, n - 1 - i, 0)
    blk = (b, CHUNKS_PER_STEP * CHUNK, w)
    out = jax.ShapeDtypeStruct((b, lp, w), F32)
    return pl.pallas_call(
        _dscan_kernel,
        grid=(n,),
        in_specs=[pl.BlockSpec(blk, fwd)] * 5 + [pl.BlockSpec(blk, bwd)] * 5,
        out_specs=(pl.BlockSpec(blk, fwd), pl.BlockSpec(blk, bwd)),
        out_shape=(out, out),
        scratch_shapes=[pltpu.VMEM((b * 2 * N_HEADS_D // HEAD_GROUP, GROUP_W, GROUP_W), F32)],
        compiler_params=_cparams(("arbitrary",)),
        name="delta_scan",
    )(q, k, v, gcf, bf, q, k, v, gcb, bb)


def _post_kernel(h_ref, at_ref, of_ref, ob_ref, z_ref, ga_ref, gd_ref, on_ref, seg_ref,
                 wa_ref, wd_ref, wo_ref, gf_ref, wrh_ref, wrl_ref,
                 h_out, xn_out, aff_out, *, tm, n_tiles):
    od = of_ref[...] + ob_ref[...]
    msd = _split_dot(od * od, seg_ref[...]) * (1.0 / DV)
    z = z_ref[...].astype(F32)
    od = od * lax.rsqrt(msd + EPS) * on_ref[...] * (z * jax.nn.sigmoid(z))
    y_d = _dot(od.astype(BF16), wd_ref[...])
    y_a = _dot(at_ref[...], wa_ref[...])
    merged = (jax.nn.sigmoid(ga_ref[...].astype(F32)) * y_a +
              jax.nn.sigmoid(gd_ref[...].astype(F32)) * y_d)
    h = h_ref[...] + _dot(merged.astype(BF16), wo_ref[...])
    h_out[...] = h

    ms = jnp.mean(h * h, axis=-1, keepdims=True)
    xn = h * lax.rsqrt(ms + EPS) * gf_ref[...]
    xh = xn.astype(BF16)
    xn_out[...] = xh
    xl = (xn - xh.astype(F32)).astype(BF16)
    logits = _dot(xh, wrh_ref[...]) + _dot(xl, wrh_ref[...]) + _dot(xh, wrl_ref[...])
    lane = lax.broadcasted_iota(jnp.int32, (1, LANES), 1)
    logits = jnp.where(lane < N_EXPERTS, logits, -jnp.inf)
    e = jnp.exp(logits - jnp.max(logits, axis=-1, keepdims=True))
    aff = e / jnp.sum(e, axis=-1, keepdims=True)
    row = lax.broadcasted_iota(jnp.int32, (tm, 1), 0) + (pl.program_id(0) % n_tiles) * tm
    aff_out[...] = jnp.where(row >= N_NULL, aff, -1.0)


def _post(h2, at, of, ob, z, ga, gd, on, seg, wa, wd, wo, gf, wrh, wrl, lp):
    n = h2.shape[0]
    tm = ROW_TILE
    nt = lp // tm
    row = lambda i: (i, 0)
    const = lambda i: (0, 0)
    hw = N_HEADS_D * DV

    def wspec(shape):
        return pl.BlockSpec(shape, const, pipeline_mode=pl.Buffered(1))

    return pl.pallas_call(
        functools.partial(_post_kernel, tm=tm, n_tiles=nt),
        grid=(n // tm,),
        in_specs=[
            pl.BlockSpec((tm, D_MODEL), row),
            pl.BlockSpec((tm, hw), row),
            pl.BlockSpec((tm, hw), row),
            pl.BlockSpec((tm, hw), row),
            pl.BlockSpec((tm, hw), row),
            pl.BlockSpec((tm, D_MODEL), row),
            pl.BlockSpec((tm, D_MODEL), row),
            pl.BlockSpec((1, hw), const),
            wspec((hw, hw)),
            wspec((hw, D_MODEL)),
            wspec((hw, D_MODEL)),
            wspec((D_MODEL, D_MODEL)),
            pl.BlockSpec((1, D_MODEL), const),
            wspec((D_MODEL, LANES)),
            wspec((D_MODEL, LANES)),
        ],
        out_specs=(pl.BlockSpec((tm, D_MODEL), row), pl.BlockSpec((tm, D_MODEL), row),
                   pl.BlockSpec((tm, LANES), row)),
        out_shape=(jax.ShapeDtypeStruct((n, D_MODEL), F32), jax.ShapeDtypeStruct((n, D_MODEL), BF16),
                   jax.ShapeDtypeStruct((n, LANES), F32)),
        compiler_params=_cparams(("parallel",)),
        name="merge_out",
    )(h2, at, of, ob, z, ga, gd, on, seg, wa, wd, wo, gf, wrh, wrl)


def _expert_kernel(x_ref, gate_ref, wg_ref, wu_ref, wd_ref, y_ref, wgb_ref, wub_ref, wdb_ref):
    @pl.when(pl.program_id(1) == 0)
    def _():
        wgb_ref[...] = wg_ref[0].astype(BF16)
        wub_ref[...] = wu_ref[0].astype(BF16)
        wdb_ref[...] = wd_ref[0].astype(BF16)

    x = x_ref[0]
    a = _dot(x, wgb_ref[...])
    b = _dot(x, wub_ref[...])
    hh = (a * jax.nn.sigmoid(a) * b).astype(BF16)
    y_ref[0] = _dot(hh, wdb_ref[...]) * gate_ref[0]


def _experts(xe, gates, wg, wu, wd, layer, tm):
    e, cp, d = xe.shape
    tile = lambda ei, i: (ei, i, 0)
    wsp = lambda ei, i: (layer, ei, 0, 0)
    return pl.pallas_call(
        _expert_kernel,
        grid=(e, cp // tm),
        in_specs=[
            pl.BlockSpec((1, tm, d), tile),
            pl.BlockSpec((1, tm, 1), tile),
            pl.BlockSpec((None, 1, d, D_EXPERT), wsp),
            pl.BlockSpec((None, 1, d, D_EXPERT), wsp),
            pl.BlockSpec((None, 1, D_EXPERT, d), wsp),
        ],
        out_specs=pl.BlockSpec((1, tm, d), tile),
        out_shape=jax.ShapeDtypeStruct((e, cp, d), F32),
        scratch_shapes=[pltpu.VMEM((d, D_EXPERT), BF16), pltpu.VMEM((d, D_EXPERT), BF16),
                        pltpu.VMEM((D_EXPERT, d), BF16)],
        compiler_params=_cparams(("parallel", "arbitrary")),
        name="experts",
    )(xe, gates, wg, wu, wd)


ROUTE_CHUNK = 128
ROUTE_WINDOW = 2 * ROUTE_CHUNK


def _route_select_kernel(aff_ref, tri_ref, pos_ref, off_ref, *, cap):
    n = aff_ref.shape[0]
    rc = ROUTE_CHUNK
    n_chunks = n // rc

    def bits_of(c):
        return pltpu.bitcast(aff_ref[pl.ds(pl.multiple_of(c * rc, rc), rc), :], jnp.int32)

    def count(pred):
        cb = ROW_TILE

        def body(c, acc):
            blk = pltpu.bitcast(aff_ref[pl.ds(pl.multiple_of(c * cb, cb), cb), :], jnp.int32)
            return acc + jnp.sum(pred(blk).astype(jnp.int32), axis=0, keepdims=True)
        return lax.fori_loop(0, n // cb, body, jnp.zeros((1, LANES), jnp.int32))

    def search(i, thr):
        cand = thr | jnp.left_shift(jnp.int32(1), 30 - i)
        return jnp.where(count(lambda b: b >= cand) >= cap, cand, thr)

    thr = lax.fori_loop(0, 31, search, jnp.zeros((1, LANES), jnp.int32))
    need = (cap - count(lambda b: b > thr)).astype(F32)
    tri = tri_ref[...]

    def emit(c, carry):
        ties_before, picks_before = carry
        b = bits_of(c)
        tie = b == thr
        tie_incl = _dot(tri, jnp.where(tie, 1.0, 0.0).astype(BF16)) + ties_before
        pick = (b > thr) | (tie & (tie_incl - 1.0 < need))
        pick_incl = _dot(tri, jnp.where(pick, 1.0, 0.0).astype(BF16)) + picks_before
        pos_ref[pl.ds(pl.multiple_of(c * rc, rc), rc), :] = jnp.where(pick, pick_incl - 1.0, -1.0)
        off_ref[c] = picks_before.astype(jnp.int32)
        return tie_incl[rc - 1:rc], pick_incl[rc - 1:rc]

    zero = jnp.zeros((1, LANES), F32)
    lax.fori_loop(0, n_chunks, emit, (zero, zero))


def _route_compact_kernel(off_ref, pos_ref, aff_ref, idx_ref, gate_ref):
    c = pl.program_id(0)
    rc = ROUTE_CHUNK

    @pl.when(c == 0)
    def _():
        idx_ref[...] = jnp.zeros(idx_ref.shape, F32)
        gate_ref[...] = jnp.zeros(gate_ref.shape, F32)

    token = (lax.broadcasted_iota(jnp.int32, (rc, 1), 0) + c * rc).astype(F32)
    slot = lax.broadcasted_iota(jnp.int32, (1, ROUTE_WINDOW), 1).astype(F32)
    pos = pos_ref[...]
    aff = aff_ref[...]
    for e in range(N_EXPERTS):
        start = pl.multiple_of((off_ref[c * N_EXPERTS + e] // LANES) * LANES, LANES)
        hit = (pos[:, e:e + 1] - start.astype(F32)) == slot
        win = (slice(e, e + 1), pl.ds(start, ROUTE_WINDOW))
        idx_ref[win] += jnp.sum(jnp.where(hit, token, 0.0), axis=0, keepdims=True)
        gate_ref[win] += jnp.sum(jnp.where(hit, aff[:, e:e + 1], 0.0), axis=0, keepdims=True)


def _route(aff, cap, slots):
    n = aff.shape[0]
    rc = ROUTE_CHUNK
    n_chunks = n // rc
    tri = jnp.asarray(np.tril(np.ones((rc, rc), np.float32)), BF16)
    whole = pl.BlockSpec(memory_space=pltpu.VMEM)
    pos, off = pl.pallas_call(
        functools.partial(_route_select_kernel, cap=cap),
        in_specs=[whole, whole],
        out_specs=(whole, whole),
        out_shape=(jax.ShapeDtypeStruct((n, LANES), F32),
                   jax.ShapeDtypeStruct((n_chunks, 1, LANES), jnp.int32)),
        compiler_params=pltpu.CompilerParams(vmem_limit_bytes=VMEM_LIMIT),
        name="route_select",
    )(aff, tri)
    off = off[:, 0, :N_EXPERTS].reshape(-1)
    chunk = lambda c, off_ref: (c, 0)
    fixed = lambda c, off_ref: (0, 0)
    out = jax.ShapeDtypeStruct((N_EXPERTS, slots), F32)
    idx, gates = pl.pallas_call(
        _route_compact_kernel,
        grid_spec=pltpu.PrefetchScalarGridSpec(
            num_scalar_prefetch=1,
            grid=(n_chunks,),
            in_specs=[pl.BlockSpec((rc, LANES), chunk), pl.BlockSpec((rc, LANES), chunk)],
            out_specs=(pl.BlockSpec((N_EXPERTS, slots), fixed), pl.BlockSpec((N_EXPERTS, slots), fixed)),
        ),
        out_shape=(out, out),
        compiler_params=_cparams(("arbitrary",)),
        name="route_compact",
    )(off, pos, aff)
    return idx.astype(jnp.int32), gates


def _final_kernel(*refs):
    *x_refs, g_ref, o_ref = refs
    tm = x_refs[0].shape[1]
    for i, x_ref in enumerate(x_refs):
        x = x_ref[0]
        ms = jnp.mean(x * x, axis=-1, keepdims=True)
        o_ref[0, i * tm:(i + 1) * tm, :] = x * lax.rsqrt(ms + EPS) * g_ref[...]


def _final_norm(h3, g, s):
    b = h3.shape[0]
    tm = FRONT
    per_step = 4 if s % (4 * tm) == 0 else 1

    def in_spec(k):
        return pl.BlockSpec((1, tm, D_MODEL), lambda bi, i: (bi, per_step * i + k + 1, 0))

    return pl.pallas_call(
        _final_kernel,
        grid=(b, s // (per_step * tm)),
        in_specs=[in_spec(k) for k in range(per_step)] + [pl.BlockSpec((1, D_MODEL), lambda bi, i: (0, 0))],
        out_specs=pl.BlockSpec((1, per_step * tm, D_MODEL), lambda bi, i: (bi, i, 0)),
        out_shape=jax.ShapeDtypeStruct((b, s, D_MODEL), F32),
        compiler_params=_cparams(("parallel", "parallel")),
        name="final_norm",
    )(*([h3] * per_step), g)


def _rope_tables(s):
    lp = FRONT + s
    t = np.arange(lp) - FRONT
    real = t >= 0
    pos = np.stack([np.where(real, t // GRID_W, 0), np.where(real, t % GRID_W, 0)], axis=-1)
    n_freq = HEAD_DIM // 4
    inv_freq = jnp.asarray(ROPE_THETA, F32) ** (-jnp.arange(n_freq, dtype=F32) / n_freq)
    lane = np.arange(LANES) % HEAD_DIM
    axis = lane // (HEAD_DIM // 2)
    freq = lane % n_freq
    ang = jnp.asarray(pos, F32)[:, axis] * inv_freq[freq][None, :]
    sign = np.where((lane % (HEAD_DIM // 2)) < n_freq, -1.0, 1.0).astype(np.float32)
    return jnp.cos(ang), jnp.sin(ang) * sign[None, :]


def _pack_w_in(w):
    o = 0

    def take(n):
        nonlocal o
        part = w[:, o:o + n]
        o += n
        return part

    def head_blocks(part, n_heads):
        part = part.reshape(D_MODEL, n_heads, HEAD_DIM)
        part = jnp.pad(part, ((0, 0), (0, 0), (0, LANES - HEAD_DIM)))
        return part.reshape(D_MODEL, n_heads * LANES)

    q_a = head_blocks(take(N_HEADS_A * HEAD_DIM), N_HEADS_A)
    k_a = head_blocks(take(N_KV_HEADS * HEAD_DIM), N_KV_HEADS)
    v_a = take(N_KV_HEADS * HEAD_DIM).reshape(D_MODEL, N_KV_HEADS, HEAD_DIM)
    v_a = jnp.pad(v_a, ((0, 0), (0, 0), (0, V_ROWS - HEAD_DIM))).reshape(D_MODEL, N_KV_HEADS * V_ROWS)
    qkv_d = take(3 * N_HEADS_D * DK)
    z_d = take(N_HEADS_D * DV)
    small = take(4 * N_HEADS_D)
    gate_a = take(D_MODEL)
    gate_d = take(D_MODEL)
    small = jnp.pad(small, ((0, 0), (0, LANES - small.shape[1])))
    w_all = jnp.concatenate([q_a, k_a, qkv_d, z_d, small, gate_a, gate_d], axis=1).astype(BF16)
    return w_all, v_a.T.astype(BF16)


def _chunk_tri(tm, reverse):
    i = np.arange(tm)
    same = (i[:, None] // CHUNK) == (i[None, :] // CHUNK)
    tri = (i[None, :] >= i[:, None]) if reverse else (i[None, :] <= i[:, None])
    return jnp.asarray((same & tri).astype(np.float32))


def _expert_tile(cap):
    n_tiles = -(-cap // 512)
    tm = -(-cap // n_tiles)
    tm = -(-tm // 16) * 16
    return tm, n_tiles


def _trunk(x, meta_tokens, layers, ffn_w, norm_final):
    b, s, d = x.shape
    lp = FRONT + s
    n = b * lp
    n_tok = b * (N_META + s)
    cap = EC_CAPACITY * n_tok // N_EXPERTS
    etm, ent = _expert_tile(cap)
    cap_pad = etm * ent
    slots = -(-max(cap_pad, cap + ROUTE_WINDOW) // LANES) * LANES
    cos, sin = _rope_tables(s)
    seg = jnp.asarray(np.kron(np.eye(N_HEADS_D), np.ones((DK, DK))), BF16)
    trif = _chunk_tri(ROW_TILE, False).astype(BF16)
    trib = _chunk_tri(ROW_TILE, True).astype(BF16)

    front = jnp.concatenate([jnp.zeros((N_NULL, d), F32), meta_tokens.astype(F32)], axis=0)
    h = jnp.concatenate([jnp.broadcast_to(front[None], (b, FRONT, d)), x], axis=1).reshape(n, d)

    for li, lw in enumerate(layers):
        q, k, vt, qkvd, z, sm, ga, gd = _inproj(h, lw["norm_mix"], lw["w_all"], lw["wvt"], cos, sin,
                                                lw["qn"], lw["kn"], lp)
        at = _attention(q.reshape(b, lp, -1), k.reshape(b, lp, -1), vt, s)
        qd, kd, vd, gcf, gcb, bf, bb = _delta_prep(qkvd.reshape(b, lp, -1), sm.reshape(b, lp, -1),
                                                   lw["conv_w"], lw["rate"], lw["dtb"], seg, trif, trib)
        of, ob = _delta_scan(qd, kd, vd, gcf, gcb, bf, bb)
        hw = N_HEADS_D * DV
        h, xn, aff = _post(h, at.reshape(n, hw), of.reshape(n, hw), ob.reshape(n, hw), z, ga, gd,
                           lw["on"], seg, lw["w_attn_proj"], lw["w_delta_proj"], lw["w_out"],
                           lw["norm_ffn"], lw["wr_hi"], lw["wr_lo"], lp)
        idx, gates = _route(aff, cap, slots)
        idx = idx[:, :cap_pad]
        gates = gates[:, :cap_pad]
        xe = xn.at[idx.reshape(-1)].get(mode="promise_in_bounds").reshape(N_EXPERTS, cap_pad, d)
        ye = _experts(xe, gates[..., None], ffn_w[0], ffn_w[1], ffn_w[2], li, etm)
        h = h.at[idx.reshape(-1)].add(ye.reshape(-1, d), mode="promise_in_bounds")

    return _final_norm(h.reshape(b, lp, d), norm_final, s)


def kernel(x_prompt, x_sample, meta_tokens, norm_mix, w_in, q_norm, k_norm, conv_w, a_log, dt_bias,
           o_norm, w_attn_proj, w_delta_proj, w_out, norm_ffn, w_router, w_gate, w_up, w_down,
           norm_final):
    depth = w_in.shape[0]
    layers = []
    for l in range(depth):
        rate = jnp.exp(a_log[l].astype(F32)).reshape(1, -1)
        dtb = dt_bias[l].astype(F32).reshape(1, -1)
        pad = LANES - rate.shape[1]
        wr = jnp.pad(w_router[l].astype(F32), ((0, 0), (0, LANES - N_EXPERTS)))
        wr_hi = wr.astype(BF16)
        w_all, wvt = _pack_w_in(w_in[l])
        layers.append(dict(
            norm_mix=norm_mix[l].astype(F32).reshape(1, -1),
            w_all=w_all,
            wvt=wvt,
            qn=jnp.tile(q_norm[l].astype(F32), LANES // HEAD_DIM).reshape(1, -1),
            kn=jnp.tile(k_norm[l].astype(F32), LANES // HEAD_DIM).reshape(1, -1),
            conv_w=conv_w[l].astype(F32),
            rate=jnp.pad(rate, ((0, 0), (0, pad))),
            dtb=jnp.pad(dtb, ((0, 0), (0, pad))),
            on=jnp.tile(o_norm[l].astype(F32), N_HEADS_D).reshape(1, -1),
            w_attn_proj=w_attn_proj[l].astype(BF16),
            w_delta_proj=w_delta_proj[l].astype(BF16),
            w_out=w_out[l].astype(BF16),
            norm_ffn=norm_ffn[l].astype(F32).reshape(1, -1),
            wr_hi=wr_hi,
            wr_lo=(wr - wr_hi.astype(F32)).astype(BF16),
        ))
    ffn_w = (w_gate.astype(F32), w_up.astype(F32), w_down.astype(F32))
    nf = norm_final.astype(F32).reshape(1, -1)
    y_prompt = _trunk(x_prompt, meta_tokens, layers, ffn_w, nf)
    y_sample = _trunk(x_sample, meta_tokens, layers, ffn_w, nf)
    return (y_prompt, y_sample)
```

```python
import functools

import numpy as np
import jax
import jax.numpy as jnp
from jax import lax
from jax.experimental import pallas as pl
from jax.experimental.pallas import tpu as pltpu

F32 = jnp.float32
BF16 = jnp.bfloat16

D_MODEL = 1024
N_META = 16
GRID_W = 64
N_HEADS_A = 8
N_KV_HEADS = 2
HEAD_DIM = 64
ROPE_THETA = 10000.0
N_HEADS_D = 8
DK = 64
DV = 64
CHUNK = 64
N_EXPERTS = 16
EC_CAPACITY = 2
D_EXPERT = 1024
EPS = 1e-6

FRONT = 128
N_NULL = FRONT - N_META
LANES = 128
ROW_TILE = 384
HEAD_GROUP = 4
GROUP_W = HEAD_GROUP * DK
CHUNKS_PER_STEP = 2
VMEM_LIMIT = 48 * 1024 * 1024
Q_SCALE = HEAD_DIM ** -0.5 * float(np.log2(np.e))
PAIRS_PER_TRIP = 3
LANE_SLICES = 2
V_ROWS = HEAD_DIM + 16

C_Q = 0
C_K = C_Q + N_HEADS_A * LANES
C_QKVD = C_K + N_KV_HEADS * LANES
C_Z = C_QKVD + 3 * N_HEADS_D * DK
C_SM = C_Z + N_HEADS_D * DV
C_GA = C_SM + LANES
C_GD = C_GA + D_MODEL
C_END = C_GD + D_MODEL


def _cparams(sem):
    return pltpu.CompilerParams(dimension_semantics=sem, vmem_limit_bytes=VMEM_LIMIT)


def _dot(a, b):
    return jnp.dot(a, b, preferred_element_type=F32)


def _dot_nt(a, b):
    return lax.dot_general(a, b, (((1,), (1,)), ((), ())), preferred_element_type=F32)


def _split_dot(t, w_bf16):
    hi = t.astype(BF16)
    lo = (t - hi.astype(F32)).astype(BF16)
    return _dot(hi, w_bf16) + _dot(lo, w_bf16)


def _inproj_kernel(x_ref, g_ref, w_ref, wvt_ref, cos_ref, sin_ref, qn_ref, kn_ref,
                   q_ref, k_ref, vt_ref, qkvd_ref, z_ref, sm_ref, ga_ref, gd_ref):
    x = x_ref[...]
    ms = jnp.mean(x * x, axis=-1, keepdims=True)
    u = (x * lax.rsqrt(ms + EPS) * g_ref[...]).astype(BF16)
    cos = cos_ref[...]
    sin = sin_ref[...]
    lane = lax.broadcasted_iota(jnp.int32, (1, LANES), 1)
    first = (lane % 32) < 16

    def rope(t):
        rot = jnp.where(first, pltpu.roll(t, LANES - 16, 1), pltpu.roll(t, 16, 1))
        return t * cos + rot * sin

    def mm(c0, n):
        return _dot(u, w_ref[:, c0:c0 + n])

    def head(t, gain):
        msq = jnp.sum(t * t, axis=-1, keepdims=True) * (1.0 / HEAD_DIM)
        return rope(t * lax.rsqrt(msq + EPS) * gain)

    for hp in range(N_HEADS_A // 2):
        t2 = mm(C_Q + 2 * LANES * hp, 2 * LANES)
        for i in range(2):
            h = 2 * hp + i
            q_ref[:, LANES * h:LANES * (h + 1)] = (
                head(t2[:, LANES * i:LANES * (i + 1)], qn_ref[...]) * Q_SCALE).astype(BF16)
    t2 = mm(C_K, N_KV_HEADS * LANES)
    for j in range(N_KV_HEADS):
        k_ref[:, LANES * j:LANES * (j + 1)] = head(t2[:, LANES * j:LANES * (j + 1)], kn_ref[...]).astype(BF16)

    vt = _dot_nt(wvt_ref[...], u)
    vrow = lax.broadcasted_iota(jnp.int32, (N_KV_HEADS * V_ROWS, 1), 0) % V_ROWS
    vt_ref[...] = jnp.where(vrow < HEAD_DIM, vt, 1.0).astype(BF16)
    qkvd_ref[...] = mm(C_QKVD, C_Z - C_QKVD)
    z_ref[...] = mm(C_Z, C_SM - C_Z).astype(BF16)
    sm_ref[...] = mm(C_SM, LANES)
    ga_ref[...] = mm(C_GA, D_MODEL).astype(BF16)
    gd_ref[...] = mm(C_GD, D_MODEL).astype(BF16)


def _inproj(h2, g, w_all, wvt, cos, sin, qn, kn, lp):
    n = h2.shape[0]
    tm = ROW_TILE
    nt = lp // tm
    row = lambda i: (i, 0)
    const = lambda i: (0, 0)
    tab = lambda i: (i % nt, 0)
    out_shapes = (
        jax.ShapeDtypeStruct((n, N_HEADS_A * LANES), BF16),
        jax.ShapeDtypeStruct((n, N_KV_HEADS * LANES), BF16),
        jax.ShapeDtypeStruct((N_KV_HEADS * V_ROWS, n), BF16),
        jax.ShapeDtypeStruct((n, C_Z - C_QKVD), F32),
        jax.ShapeDtypeStruct((n, C_SM - C_Z), BF16),
        jax.ShapeDtypeStruct((n, LANES), F32),
        jax.ShapeDtypeStruct((n, D_MODEL), BF16),
        jax.ShapeDtypeStruct((n, D_MODEL), BF16),
    )
    out_specs = [pl.BlockSpec((tm, s.shape[1]), row) for s in out_shapes]
    out_specs[2] = pl.BlockSpec((N_KV_HEADS * V_ROWS, tm), lambda i: (0, i))
    return pl.pallas_call(
        _inproj_kernel,
        grid=(n // tm,),
        in_specs=[
            pl.BlockSpec((tm, D_MODEL), row),
            pl.BlockSpec((1, D_MODEL), const),
            pl.BlockSpec((D_MODEL, C_END), const, pipeline_mode=pl.Buffered(1)),
            pl.BlockSpec((N_KV_HEADS * V_ROWS, D_MODEL), const, pipeline_mode=pl.Buffered(1)),
            pl.BlockSpec((tm, LANES), tab),
            pl.BlockSpec((tm, LANES), tab),
            pl.BlockSpec((1, LANES), const),
            pl.BlockSpec((1, LANES), const),
        ],
        out_specs=tuple(out_specs),
        out_shape=out_shapes,
        compiler_params=_cparams(("parallel",)),
        name="inproj",
    )(h2, g, w_all, wvt, cos, sin, qn, kn)


def _flash_kernel(q_ref, k_ref, vt_ref, o_ref, qs_ref, m_ref, acc_ref, sa_ref, sb_ref, ca_ref, cb_ref,
                  *, tq, tk, s_real):
    g = N_HEADS_A // N_KV_HEADS
    r = g * tq
    for i in range(g):
        qs_ref[i * tq:(i + 1) * tq, :] = q_ref[0, :, LANES * i:LANES * (i + 1)]
    m_ref[...] = jnp.full(m_ref.shape, -jnp.inf, F32)
    acc_ref[...] = jnp.zeros(acc_ref.shape, F32)

    def scores(koff, size):
        return _dot_nt(k_ref[0, pl.ds(koff, size), :], qs_ref[...])

    def absorb(s, cmax, koff, size):
        m_prev = m_ref[...]
        m_new = jnp.maximum(m_prev, cmax)
        alpha = jnp.exp2(m_prev - m_new)
        p = jnp.exp2(s - m_new).astype(BF16)
        acc_ref[...] = alpha * acc_ref[...] + _dot(vt_ref[:, pl.ds(koff, size)], p)
        m_ref[...] = m_new

    def produce(koff, s_ref, c_ref):
        s = scores(koff, tk)
        s_ref[...] = s
        c_ref[...] = jnp.max(s, axis=0, keepdims=True)

    def main_off(c):
        off = FRONT + c * tk
        return off if isinstance(c, int) else pl.multiple_of(off, LANES)

    s = scores(0, FRONT)
    s = jnp.where(lax.broadcasted_iota(jnp.int32, (FRONT, 1), 0) >= N_NULL, s, -jnp.inf)

    n_main = s_real // tk
    if n_main:
        produce(main_off(0), sa_ref, ca_ref)
    absorb(s, jnp.max(s, axis=0, keepdims=True), 0, FRONT)
    if n_main:
        n_pairs = (n_main - 1) // 2

        hw = r // LANE_SLICES

        def step(koff_next, nxt, koff_cur, cur):
            (sn_ref, cn_ref), (sc_ref, cc_ref) = nxt, cur
            for hh in range(LANE_SLICES):
                sl = slice(hh * hw, (hh + 1) * hw)
                s_new = _dot_nt(k_ref[0, pl.ds(koff_next, tk), :], qs_ref[sl, :])
                sn_ref[:, sl] = s_new
                cn_ref[:, sl] = jnp.max(s_new, axis=0, keepdims=True)
                m_prev = m_ref[:, sl]
                m_new = jnp.maximum(m_prev, cc_ref[:, sl])
                alpha = jnp.exp2(m_prev - m_new)
                p = jnp.exp2(sc_ref[:, sl] - m_new).astype(BF16)
                acc_ref[:, sl] = alpha * acc_ref[:, sl] + _dot(vt_ref[:, pl.ds(koff_cur, tk)], p)
                m_ref[:, sl] = m_new

        def pair(pi):
            buf_a, buf_b = (sa_ref, ca_ref), (sb_ref, cb_ref)
            step(main_off(2 * pi + 1), buf_b, main_off(2 * pi), buf_a)
            step(main_off(2 * pi + 2), buf_a, main_off(2 * pi + 1), buf_b)

        def body(t, carry):
            for u in range(PAIRS_PER_TRIP):
                pair(t * PAIRS_PER_TRIP + u)
            return carry

        lax.fori_loop(0, n_pairs // PAIRS_PER_TRIP, body, 0)
        for pi in range(n_pairs - n_pairs % PAIRS_PER_TRIP, n_pairs):
            pair(pi)
        if n_main - 2 * n_pairs == 2:
            step(main_off(2 * n_pairs + 1), (sb_ref, cb_ref), main_off(2 * n_pairs), (sa_ref, ca_ref))
            absorb(sb_ref[...], cb_ref[...], main_off(2 * n_pairs + 1), tk)
        else:
            absorb(sa_ref[...], ca_ref[...], main_off(2 * n_pairs), tk)
    if s_real % tk:
        s = scores(FRONT + n_main * tk, s_real % tk)
        absorb(s, jnp.max(s, axis=0, keepdims=True), FRONT + n_main * tk, s_real % tk)

    acc = acc_ref[...]
    out_t = acc[:HEAD_DIM] / acc[HEAD_DIM:HEAD_DIM + 1]
    for p in range(g // 2):
        pair = jnp.concatenate([out_t[:, (2 * p) * tq:(2 * p + 1) * tq],
                                out_t[:, (2 * p + 1) * tq:(2 * p + 2) * tq]], axis=0)
        o_ref[0, :, 2 * HEAD_DIM * p:2 * HEAD_DIM * (p + 1)] = pair.T.astype(BF16)


def _attention(q, k, vt, s_real):
    b, lp, _ = q.shape
    tq = ROW_TILE
    tk = 512
    g = N_HEADS_A // N_KV_HEADS
    return pl.pallas_call(
        functools.partial(_flash_kernel, tq=tq, tk=tk, s_real=s_real),
        grid=(b, N_KV_HEADS, lp // tq),
        in_specs=[
            pl.BlockSpec((1, tq, g * LANES), lambda bi, j, qi: (bi, qi, j)),
            pl.BlockSpec((1, lp, LANES), lambda bi, j, qi: (bi, 0, j)),
            pl.BlockSpec((V_ROWS, lp), lambda bi, j, qi: (j, bi)),
        ],
        out_specs=pl.BlockSpec((1, tq, g * HEAD_DIM), lambda bi, j, qi: (bi, qi, j)),
        out_shape=jax.ShapeDtypeStruct((b, lp, N_HEADS_A * HEAD_DIM), BF16),
        scratch_shapes=[
            pltpu.VMEM((g * tq, LANES), BF16),
            pltpu.VMEM((1, g * tq), F32),
            pltpu.VMEM((V_ROWS, g * tq), F32),
            pltpu.VMEM((tk, g * tq), F32),
            pltpu.VMEM((tk, g * tq), F32),
            pltpu.VMEM((1, g * tq), F32),
            pltpu.VMEM((1, g * tq), F32),
        ],
        compiler_params=_cparams(("parallel", "parallel", "arbitrary")),
        name="attention",
    )(q, k, vt)


def _dprep_kernel(prev_ref, cur_ref, next_ref, sm_ref, cw_ref, rate_ref, dtb_ref, seg_ref,
                  trif_ref, trib_ref, q_ref, k_ref, v_ref, gcf_ref, gcb_ref, bf_ref, bb_ref,
                  *, tm, n_tiles):
    i = pl.program_id(1)
    row = lax.broadcasted_iota(jnp.int32, (tm, 1), 0)
    valid = (row + i * tm) >= N_NULL
    x = jnp.where(valid, cur_ref[0], 0.0)
    prow = jnp.where(i > 0, prev_ref[0, 7:8, :], 0.0)
    nrow = jnp.where(i < n_tiles - 1, next_ref[0, 0:1, :], 0.0)
    xp = jnp.where(row == 0, prow, pltpu.roll(x, 1, 0))
    xn = jnp.where(row == tm - 1, nrow, pltpu.roll(x, tm - 1, 0))
    cw = cw_ref[...]
    y = cw[0:1] * xp + cw[1:2] * x + cw[2:3] * xn
    y = y * jax.nn.sigmoid(y)

    w = N_HEADS_D * DK
    seg = seg_ref[...]
    q = y[:, 0:w]
    q = q * lax.rsqrt(_split_dot(q * q, seg) + EPS) * (DK ** -0.5)
    k = y[:, w:2 * w]
    k = k * lax.rsqrt(_split_dot(k * k, seg) + EPS)
    q_ref[0] = jnp.where(valid, q, 0.0)
    k_ref[0] = jnp.where(valid, k, 0.0)
    v_ref[0] = jnp.where(valid, y[:, 2 * w:3 * w], 0.0)

    sm = sm_ref[0]
    t = sm + dtb_ref[...]
    softplus = jnp.maximum(t, 0.0) + jnp.log1p(jnp.exp(-jnp.abs(t)))
    g_all = jnp.where(valid, -rate_ref[...] * softplus, 0.0)
    beta_all = jnp.where(valid, jax.nn.sigmoid(sm), 0.0)
    g1 = g_all.astype(BF16)
    r1 = g_all - g1.astype(F32)
    g2 = r1.astype(BF16)
    g3 = (r1 - g2.astype(F32)).astype(BF16)
    trif = trif_ref[...]
    trib = trib_ref[...]
    gc_f = _dot(trif, g1) + _dot(trif, g2) + _dot(trif, g3)
    gc_b = _dot(trib, g1) + _dot(trib, g2) + _dot(trib, g3)

    lane = lax.broadcasted_iota(jnp.int32, (1, LANES), 1)
    low = lane < DK

    def expand(a, c0, out_ref):
        for p in range(N_HEADS_D // 2):
            e0 = jnp.broadcast_to(a[:, c0 + 2 * p:c0 + 2 * p + 1], (tm, LANES))
            e1 = jnp.broadcast_to(a[:, c0 + 2 * p + 1:c0 + 2 * p + 2], (tm, LANES))
            out_ref[0, :, LANES * p:LANES * (p + 1)] = jnp.where(low, e0, e1)

    expand(gc_f, 0, gcf_ref)
    expand(gc_b, N_HEADS_D, gcb_ref)
    expand(beta_all, 2 * N_HEADS_D, bf_ref)
    expand(beta_all, 3 * N_HEADS_D, bb_ref)


def _delta_prep(qkvd, sm, cw, rate, dtb, seg, trif, trib):
    b, lp, c = qkvd.shape
    tm = ROW_TILE
    nt = lp // tm
    w = N_HEADS_D * DK
    halo = 8
    nb8 = lp // halo
    cur = lambda bi, i: (bi, i, 0)
    const = lambda bi, i: (0, 0)
    out = jax.ShapeDtypeStruct((b, lp, w), F32)
    return pl.pallas_call(
        functools.partial(_dprep_kernel, tm=tm, n_tiles=nt),
        grid=(b, nt),
        in_specs=[
            pl.BlockSpec((1, halo, c), lambda bi, i: (bi, jnp.maximum(i * (tm // halo) - 1, 0), 0)),
            pl.BlockSpec((1, tm, c), cur),
            pl.BlockSpec((1, halo, c), lambda bi, i: (bi, jnp.minimum((i + 1) * (tm // halo), nb8 - 1), 0)),
            pl.BlockSpec((1, tm, LANES), cur),
            pl.BlockSpec((3, c), const),
            pl.BlockSpec((1, LANES), const),
            pl.BlockSpec((1, LANES), const),
            pl.BlockSpec((w, w), const),
            pl.BlockSpec((tm, tm), const),
            pl.BlockSpec((tm, tm), const),
        ],
        out_specs=tuple(pl.BlockSpec((1, tm, w), cur) for _ in range(7)),
        out_shape=(out,) * 7,
        compiler_params=_cparams(("parallel", "parallel")),
        name="delta_prep",
    )(qkvd, qkvd, qkvd, sm, cw, rate, dtb, seg, trif, trib)


def _dscan_kernel(qf_ref, kf_ref, vf_ref, gf_ref, bf_ref, qb_ref, kb_ref, vb_ref, gb_ref, bb_ref,
                  of_ref, ob_ref, s_ref):
    i = pl.program_id(0)

    @pl.when(i == 0)
    def _():
        s_ref[...] = jnp.zeros(s_ref.shape, F32)

    c = CHUNK
    r = lax.broadcasted_iota(jnp.int32, (c, GROUP_W), 0)
    cj = lax.broadcasted_iota(jnp.int32, (c, GROUP_W), 1) % c
    eye = r == cj
    eyef = eye.astype(F32)
    bdmask = (lax.broadcasted_iota(jnp.int32, (GROUP_W, GROUP_W), 0) // c ==
              lax.broadcasted_iota(jnp.int32, (GROUP_W, GROUP_W), 1) // c)

    lane_low = lax.broadcasted_iota(jnp.int32, (c, LANES), 1) < DK
    zero_tile = jnp.zeros((c, LANES), BF16)

    def bd(t):
        rows = []
        for hd in range(HEAD_GROUP):
            tile = t[:, LANES * (hd // 2):LANES * (hd // 2 + 1)]
            tile = jnp.where(lane_low == (hd % 2 == 0), tile, 0.0).astype(BF16)
            rows.append(jnp.concatenate([tile, zero_tile] if hd < 2 else [zero_tile, tile], axis=1))
        return jnp.concatenate(rows, axis=0)

    n_grp = N_HEADS_D // HEAD_GROUP
    fwd_refs = (qf_ref, kf_ref, vf_ref, gf_ref, bf_ref, of_ref)
    bwd_refs = (qb_ref, kb_ref, vb_ref, gb_ref, bb_ref, ob_ref)
    chains = []
    for bi in range(of_ref.shape[0]):
        for grp in range(n_grp):
            for reverse in (False, True):
                sl = slice(GROUP_W * grp, GROUP_W * (grp + 1))
                chains.append((fwd_refs if not reverse else bwd_refs, bi, sl, reverse, len(chains)))
    n_chains = len(chains)
    items = []
    for step in range(CHUNKS_PER_STEP):
        for ch in chains:
            ci = CHUNKS_PER_STEP - 1 - step if ch[3] else step
            items.append(ch + (slice(ci * c, (ci + 1) * c),))

    def each(fn, *cols):
        return [fn(*args) for args in zip(*cols)]

    rev = [it[3] for it in items]
    q = [it[0][0][it[1], it[5], it[2]] for it in items]
    k = [it[0][1][it[1], it[5], it[2]] for it in items]
    v = [it[0][2][it[1], it[5], it[2]] for it in items]
    gc = [it[0][3][it[1], it[5], it[2]] for it in items]
    beta = [it[0][4][it[1], it[5], it[2]] for it in items]

    def decay_of(g, reverse):
        incl = (r <= cj) if reverse else (r >= cj)
        gcol = jnp.sum(jnp.where(eye, g, 0.0), axis=0, keepdims=True)
        return jnp.where(incl, jnp.exp(jnp.where(incl, g - gcol, 0.0)), 0.0)

    decay = each(decay_of, gc, rev)
    eg = each(jnp.exp, gc)
    glast = each(lambda g, reverse: g[0:1] if reverse else g[c - 1:c], gc, rev)
    kbeta = each(lambda a, b: a * b, k, beta)
    gq = each(lambda kb_, q_, k_: _dot_nt(jnp.concatenate([kb_, q_], axis=0).astype(BF16), bd(k_)),
              kbeta, q, k)
    a_intra = each(lambda g, d: g[c:] * d, gq, decay)
    m = each(lambda g, d, reverse: -jnp.where((r < cj) if reverse else (r > cj), g[:c] * d, 0.0),
             gq, decay, rev)
    p = each(lambda m_: eyef + m_, m)
    m = each(lambda m_: _dot(m_.astype(BF16), bd(m_)), m)
    for _ in range(4):
        rr = each(lambda m_, p_: _dot(jnp.concatenate([m_, p_], axis=0).astype(BF16), bd(m_)), m, p)
        p = each(lambda p_, rr_: p_ + rr_[c:], p, rr)
        m = each(lambda rr_: rr_[:c], rr)
    tmat = each(lambda p_, m_: (p_ + _dot(p_.astype(BF16), bd(m_))).astype(BF16), p, m)
    u = each(lambda t, v_, b: _dot(t, bd(v_ * b)), tmat, v, beta)
    w = each(lambda t, kb_, e: _dot(t, bd(kb_ * e)), tmat, kbeta, eg)
    state = [s_ref[ch[4]] for ch in chains]
    for step in range(CHUNKS_PER_STEP):
        part = slice(step * n_chains, (step + 1) * n_chains)
        wq = each(lambda w_, q_, e, s: _dot(jnp.concatenate([w_, q_ * e], axis=0).astype(BF16), s.astype(BF16)),
                  w[part], q[part], eg[part], state)
        v_new = each(lambda u_, wq_: u_ - wq_[:c], u[part], wq)
        o = each(lambda wq_, a, vn: wq_[c:] + _dot(a.astype(BF16), bd(vn)), wq, a_intra[part], v_new)
        upd = each(lambda k_, gl, g, vn: _dot((k_ * jnp.exp(gl - g)).T.astype(BF16), vn.astype(BF16)),
                   k[part], glast[part], gc[part], v_new)
        state = each(lambda s, gl, up: s * jnp.exp(gl) + jnp.where(bdmask, up, 0.0), state, glast[part], upd)
        for it, o_ in zip(items[part], o):
            it[0][5][it[1], it[5], it[2]] = o_
    for ch, s in zip(chains, state):
        s_ref[ch[4]] = s


def _delta_scan(q, k, v, gcf, gcb, bf, bb):
    b, lp, w = q.shape
    n = lp // (CHUNKS_PER_STEP * CHUNK)
    fwd = lambda i: (0, i, 0)
    bwd = lambda i: (0, n - 1 - i, 0)
    blk = (b, CHUNKS_PER_STEP * CHUNK, w)
    out = jax.ShapeDtypeStruct((b, lp, w), F32)
    return pl.pallas_call(
        _dscan_kernel,
        grid=(n,),
        in_specs=[pl.BlockSpec(blk, fwd)] * 5 + [pl.BlockSpec(blk, bwd)] * 5,
        out_specs=(pl.BlockSpec(blk, fwd), pl.BlockSpec(blk, bwd)),
        out_shape=(out, out),
        scratch_shapes=[pltpu.VMEM((b * 2 * N_HEADS_D // HEAD_GROUP, GROUP_W, GROUP_W), F32)],
        compiler_params=_cparams(("arbitrary",)),
        name="delta_scan",
    )(q, k, v, gcf, bf, q, k, v, gcb, bb)


def _post_kernel(h_ref, at_ref, of_ref, ob_ref, z_ref, ga_ref, gd_ref, on_ref, seg_ref,
                 wa_ref, wd_ref, wo_ref, gf_ref, wr_ref,
                 h_out, xn_out, aff_out, *, tm, n_tiles):
    od = of_ref[...] + ob_ref[...]
    msd = _split_dot(od * od, seg_ref[...]) * (1.0 / DV)
    z = z_ref[...].astype(F32)
    od = od * lax.rsqrt(msd + EPS) * on_ref[...] * (z * jax.nn.sigmoid(z))
    y_d = _dot(od.astype(BF16), wd_ref[...])
    y_a = _dot(at_ref[...], wa_ref[...])
    merged = (jax.nn.sigmoid(ga_ref[...].astype(F32)) * y_a +
              jax.nn.sigmoid(gd_ref[...].astype(F32)) * y_d)
    h = h_ref[...] + _dot(merged.astype(BF16), wo_ref[...])
    h_out[...] = h

    ms = jnp.mean(h * h, axis=-1, keepdims=True)
    xn = h * lax.rsqrt(ms + EPS) * gf_ref[...]
    xh = xn.astype(BF16)
    xn_out[...] = xh
    xl = (xn - xh.astype(F32)).astype(BF16)
    both = _dot(xh, wr_ref[...])
    logits = both[:, :LANES] + both[:, LANES:] + _dot(xl, wr_ref[:, :LANES])
    lane = lax.broadcasted_iota(jnp.int32, (1, LANES), 1)
    logits = jnp.where(lane < N_EXPERTS, logits, -jnp.inf)
    e = jnp.exp(logits - jnp.max(logits, axis=-1, keepdims=True))
    aff = e / jnp.sum(e, axis=-1, keepdims=True)
    row = lax.broadcasted_iota(jnp.int32, (tm, 1), 0) + (pl.program_id(0) % n_tiles) * tm
    aff_out[...] = jnp.where(row >= N_NULL, aff, -1.0)


def _post(h2, at, of, ob, z, ga, gd, on, seg, wa, wd, wo, gf, wr, lp):
    n = h2.shape[0]
    tm = ROW_TILE
    nt = lp // tm
    row = lambda i: (i, 0)
    const = lambda i: (0, 0)
    hw = N_HEADS_D * DV

    def wspec(shape):
        return pl.BlockSpec(shape, const, pipeline_mode=pl.Buffered(1))

    return pl.pallas_call(
        functools.partial(_post_kernel, tm=tm, n_tiles=nt),
        grid=(n // tm,),
        in_specs=[
            pl.BlockSpec((tm, D_MODEL), row),
            pl.BlockSpec((tm, hw), row),
            pl.BlockSpec((tm, hw), row),
            pl.BlockSpec((tm, hw), row),
            pl.BlockSpec((tm, hw), row),
            pl.BlockSpec((tm, D_MODEL), row),
            pl.BlockSpec((tm, D_MODEL), row),
            pl.BlockSpec((1, hw), const),
            wspec((hw, hw)),
            wspec((hw, D_MODEL)),
            wspec((hw, D_MODEL)),
            wspec((D_MODEL, D_MODEL)),
            pl.BlockSpec((1, D_MODEL), const),
            wspec((D_MODEL, 2 * LANES)),
        ],
        out_specs=(pl.BlockSpec((tm, D_MODEL), row), pl.BlockSpec((tm, D_MODEL), row),
                   pl.BlockSpec((tm, LANES), row)),
        out_shape=(jax.ShapeDtypeStruct((n, D_MODEL), F32), jax.ShapeDtypeStruct((n, D_MODEL), BF16),
                   jax.ShapeDtypeStruct((n, LANES), F32)),
        compiler_params=_cparams(("parallel",)),
        name="merge_out",
    )(h2, at, of, ob, z, ga, gd, on, seg, wa, wd, wo, gf, wr)


def _expert_kernel(x_ref, gate_ref, wg_ref, wu_ref, wd_ref, y_ref, wgb_ref, wub_ref, wdb_ref):
    @pl.when(pl.program_id(1) == 0)
    def _():
        wgb_ref[...] = wg_ref[0].astype(BF16)
        wub_ref[...] = wu_ref[0].astype(BF16)
        wdb_ref[...] = wd_ref[0].astype(BF16)

    x = x_ref[0]
    a = _dot(x, wgb_ref[...])
    b = _dot(x, wub_ref[...])
    hh = (a * jax.nn.sigmoid(a) * b).astype(BF16)
    y_ref[0] = _dot(hh, wdb_ref[...]) * gate_ref[0]


def _experts(xe, gates, wg, wu, wd, layer, tm):
    e, cp, d = xe.shape
    tile = lambda ei, i: (ei, i, 0)
    wsp = lambda ei, i: (layer, ei, 0, 0)
    return pl.pallas_call(
        _expert_kernel,
        grid=(e, cp // tm),
        in_specs=[
            pl.BlockSpec((1, tm, d), tile),
            pl.BlockSpec((1, tm, 1), tile),
            pl.BlockSpec((None, 1, d, D_EXPERT), wsp),
            pl.BlockSpec((None, 1, d, D_EXPERT), wsp),
            pl.BlockSpec((None, 1, D_EXPERT, d), wsp),
        ],
        out_specs=pl.BlockSpec((1, tm, d), tile),
        out_shape=jax.ShapeDtypeStruct((e, cp, d), F32),
        scratch_shapes=[pltpu.VMEM((d, D_EXPERT), BF16), pltpu.VMEM((d, D_EXPERT), BF16),
                        pltpu.VMEM((D_EXPERT, d), BF16)],
        compiler_params=_cparams(("parallel", "arbitrary")),
        name="experts",
    )(xe, gates, wg, wu, wd)


ROUTE_CHUNK = 128
ROUTE_WINDOW = 2 * ROUTE_CHUNK


def _route_select_kernel(aff_ref, tri_ref, pos_ref, off_ref, *, cap):
    n = aff_ref.shape[0]
    rc = ROUTE_CHUNK
    n_chunks = n // rc

    def bits_of(c):
        return pltpu.bitcast(aff_ref[pl.ds(pl.multiple_of(c * rc, rc), rc), :], jnp.int32)

    def count(pred):
        cb = ROW_TILE

        def body(c, acc):
            blk = pltpu.bitcast(aff_ref[pl.ds(pl.multiple_of(c * cb, cb), cb), :], jnp.int32)
            return acc + jnp.sum(pred(blk).astype(jnp.int32), axis=0, keepdims=True)
        return lax.fori_loop(0, n // cb, body, jnp.zeros((1, LANES), jnp.int32))

    def search(i, thr):
        cand = thr | jnp.left_shift(jnp.int32(1), 30 - i)
        return jnp.where(count(lambda b: b >= cand) >= cap, cand, thr)

    thr = lax.fori_loop(0, 31, search, jnp.zeros((1, LANES), jnp.int32))
    need = (cap - count(lambda b: b > thr)).astype(F32)
    tri = tri_ref[...]

    def emit(c, carry):
        ties_before, picks_before = carry
        b = bits_of(c)
        tie = b == thr
        tie_incl = _dot(tri, jnp.where(tie, 1.0, 0.0).astype(BF16)) + ties_before
        pick = (b > thr) | (tie & (tie_incl - 1.0 < need))
        pick_incl = _dot(tri, jnp.where(pick, 1.0, 0.0).astype(BF16)) + picks_before
        pos_ref[pl.ds(pl.multiple_of(c * rc, rc), rc), :] = jnp.where(pick, pick_incl - 1.0, -1.0)
        off_ref[c] = picks_before.astype(jnp.int32)
        return tie_incl[rc - 1:rc], pick_incl[rc - 1:rc]

    zero = jnp.zeros((1, LANES), F32)
    lax.fori_loop(0, n_chunks, emit, (zero, zero))


def _route_compact_kernel(off_ref, pos_ref, aff_ref, idx_ref, gate_ref):
    c = pl.program_id(0)
    rc = ROUTE_CHUNK

    @pl.when(c == 0)
    def _():
        idx_ref[...] = jnp.zeros(idx_ref.shape, F32)
        gate_ref[...] = jnp.zeros(gate_ref.shape, F32)

    token = (lax.broadcasted_iota(jnp.int32, (rc, 1), 0) + c * rc).astype(F32)
    slot = lax.broadcasted_iota(jnp.int32, (1, ROUTE_WINDOW), 1).astype(F32)
    pos = pos_ref[...]
    aff = aff_ref[...]
    for e in range(N_EXPERTS):
        start = pl.multiple_of((off_ref[c * N_EXPERTS + e] // LANES) * LANES, LANES)
        hit = (pos[:, e:e + 1] - start.astype(F32)) == slot
        win = (slice(e, e + 1), pl.ds(start, ROUTE_WINDOW))
        idx_ref[win] += jnp.sum(jnp.where(hit, token, 0.0), axis=0, keepdims=True)
        gate_ref[win] += jnp.sum(jnp.where(hit, aff[:, e:e + 1], 0.0), axis=0, keepdims=True)


def _route(aff, cap, slots):
    n = aff.shape[0]
    rc = ROUTE_CHUNK
    n_chunks = n // rc
    tri = jnp.asarray(np.tril(np.ones((rc, rc), np.float32)), BF16)
    whole = pl.BlockSpec(memory_space=pltpu.VMEM)
    pos, off = pl.pallas_call(
        functools.partial(_route_select_kernel, cap=cap),
        in_specs=[whole, whole],
        out_specs=(whole, whole),
        out_shape=(jax.ShapeDtypeStruct((n, LANES), F32),
                   jax.ShapeDtypeStruct((n_chunks, 1, LANES), jnp.int32)),
        compiler_params=pltpu.CompilerParams(vmem_limit_bytes=VMEM_LIMIT),
        name="route_select",
    )(aff, tri)
    off = off[:, 0, :N_EXPERTS].reshape(-1)
    chunk = lambda c, off_ref: (c, 0)
    fixed = lambda c, off_ref: (0, 0)
    out = jax.ShapeDtypeStruct((N_EXPERTS, slots), F32)
    idx, gates = pl.pallas_call(
        _route_compact_kernel,
        grid_spec=pltpu.PrefetchScalarGridSpec(
            num_scalar_prefetch=1,
            grid=(n_chunks,),
            in_specs=[pl.BlockSpec((rc, LANES), chunk), pl.BlockSpec((rc, LANES), chunk)],
            out_specs=(pl.BlockSpec((N_EXPERTS, slots), fixed), pl.BlockSpec((N_EXPERTS, slots), fixed)),
        ),
        out_shape=(out, out),
        compiler_params=_cparams(("arbitrary",)),
        name="route_compact",
    )(off, pos, aff)
    return idx.astype(jnp.int32), gates


def _final_kernel(*refs):
    *x_refs, g_ref, o_ref = refs
    tm = x_refs[0].shape[1]
    for i, x_ref in enumerate(x_refs):
        x = x_ref[0]
        ms = jnp.mean(x * x, axis=-1, keepdims=True)
        o_ref[0, i * tm:(i + 1) * tm, :] = x * lax.rsqrt(ms + EPS) * g_ref[...]


def _final_norm(h3, g, s):
    b = h3.shape[0]
    tm = FRONT
    per_step = 4 if s % (4 * tm) == 0 else 1

    def in_spec(k):
        return pl.BlockSpec((1, tm, D_MODEL), lambda bi, i: (bi, per_step * i + k + 1, 0))

    return pl.pallas_call(
        _final_kernel,
        grid=(b, s // (per_step * tm)),
        in_specs=[in_spec(k) for k in range(per_step)] + [pl.BlockSpec((1, D_MODEL), lambda bi, i: (0, 0))],
        out_specs=pl.BlockSpec((1, per_step * tm, D_MODEL), lambda bi, i: (bi, i, 0)),
        out_shape=jax.ShapeDtypeStruct((b, s, D_MODEL), F32),
        compiler_params=_cparams(("parallel", "parallel")),
        name="final_norm",
    )(*([h3] * per_step), g)


def _rope_tables(s):
    lp = FRONT + s
    t = np.arange(lp) - FRONT
    real = t >= 0
    pos = np.stack([np.where(real, t // GRID_W, 0), np.where(real, t % GRID_W, 0)], axis=-1)
    n_freq = HEAD_DIM // 4
    inv_freq = jnp.asarray(ROPE_THETA, F32) ** (-jnp.arange(n_freq, dtype=F32) / n_freq)
    lane = np.arange(LANES) % HEAD_DIM
    axis = lane // (HEAD_DIM // 2)
    freq = lane % n_freq
    ang = jnp.asarray(pos, F32)[:, axis] * inv_freq[freq][None, :]
    sign = np.where((lane % (HEAD_DIM // 2)) < n_freq, -1.0, 1.0).astype(np.float32)
    return jnp.cos(ang), jnp.sin(ang) * sign[None, :]


def _pack_w_in(w):
    o = 0

    def take(n):
        nonlocal o
        part = w[:, o:o + n]
        o += n
        return part

    def head_blocks(part, n_heads):
        part = part.reshape(D_MODEL, n_heads, HEAD_DIM)
        part = jnp.pad(part, ((0, 0), (0, 0), (0, LANES - HEAD_DIM)))
        return part.reshape(D_MODEL, n_heads * LANES)

    q_a = head_blocks(take(N_HEADS_A * HEAD_DIM), N_HEADS_A)
    k_a = head_blocks(take(N_KV_HEADS * HEAD_DIM), N_KV_HEADS)
    v_a = take(N_KV_HEADS * HEAD_DIM).reshape(D_MODEL, N_KV_HEADS, HEAD_DIM)
    v_a = jnp.pad(v_a, ((0, 0), (0, 0), (0, V_ROWS - HEAD_DIM))).reshape(D_MODEL, N_KV_HEADS * V_ROWS)
    qkv_d = take(3 * N_HEADS_D * DK)
    z_d = take(N_HEADS_D * DV)
    small = take(4 * N_HEADS_D)
    gate_a = take(D_MODEL)
    gate_d = take(D_MODEL)
    small = jnp.pad(small, ((0, 0), (0, LANES - small.shape[1])))
    w_all = jnp.concatenate([q_a, k_a, qkv_d, z_d, small, gate_a, gate_d], axis=1).astype(BF16)
    return w_all, v_a.T.astype(BF16)


def _chunk_tri(tm, reverse):
    i = np.arange(tm)
    same = (i[:, None] // CHUNK) == (i[None, :] // CHUNK)
    tri = (i[None, :] >= i[:, None]) if reverse else (i[None, :] <= i[:, None])
    return jnp.asarray((same & tri).astype(np.float32))


def _expert_tile(cap):
    n_tiles = -(-cap // 512)
    tm = -(-cap // n_tiles)
    tm = -(-tm // 16) * 16
    return tm, n_tiles


def _trunk(x, meta_tokens, layers, ffn_w, norm_final):
    b, s, d = x.shape
    lp = FRONT + s
    n = b * lp
    n_tok = b * (N_META + s)
    cap = EC_CAPACITY * n_tok // N_EXPERTS
    etm, ent = _expert_tile(cap)
    cap_pad = etm * ent
    slots = -(-max(cap_pad, cap + ROUTE_WINDOW) // LANES) * LANES
    cos, sin = _rope_tables(s)
    seg = jnp.asarray(np.kron(np.eye(N_HEADS_D), np.ones((DK, DK))), BF16)
    trif = _chunk_tri(ROW_TILE, False).astype(BF16)
    trib = _chunk_tri(ROW_TILE, True).astype(BF16)

    front = jnp.concatenate([jnp.zeros((N_NULL, d), F32), meta_tokens.astype(F32)], axis=0)
    h = jnp.concatenate([jnp.broadcast_to(front[None], (b, FRONT, d)), x], axis=1).reshape(n, d)

    for li, lw in enumerate(layers):
        q, k, vt, qkvd, z, sm, ga, gd = _inproj(h, lw["norm_mix"], lw["w_all"], lw["wvt"], cos, sin,
                                                lw["qn"], lw["kn"], lp)
        at = _attention(q.reshape(b, lp, -1), k.reshape(b, lp, -1), vt, s)
        qd, kd, vd, gcf, gcb, bf, bb = _delta_prep(qkvd.reshape(b, lp, -1), sm.reshape(b, lp, -1),
                                                   lw["conv_w"], lw["rate"], lw["dtb"], seg, trif, trib)
        of, ob = _delta_scan(qd, kd, vd, gcf, gcb, bf, bb)
        hw = N_HEADS_D * DV
        h, xn, aff = _post(h, at.reshape(n, hw), of.reshape(n, hw), ob.reshape(n, hw), z, ga, gd,
                           lw["on"], seg, lw["w_attn_proj"], lw["w_delta_proj"], lw["w_out"],
                           lw["norm_ffn"], lw["wr"], lp)
        idx, gates = _route(aff, cap, slots)
        idx = idx[:, :cap_pad]
        gates = gates[:, :cap_pad]
        xe = xn.at[idx.reshape(-1)].get(mode="promise_in_bounds").reshape(N_EXPERTS, cap_pad, d)
        ye = _experts(xe, gates[..., None], ffn_w[0], ffn_w[1], ffn_w[2], li, etm)
        h = h.at[idx.reshape(-1)].add(ye.reshape(-1, d), mode="promise_in_bounds")

    return _final_norm(h.reshape(b, lp, d), norm_final, s)


def kernel(x_prompt, x_sample, meta_tokens, norm_mix, w_in, q_norm, k_norm, conv_w, a_log, dt_bias,
           o_norm, w_attn_proj, w_delta_proj, w_out, norm_ffn, w_router, w_gate, w_up, w_down,
           norm_final):
    depth = w_in.shape[0]
    layers = []
    for l in range(depth):
        rate = jnp.exp(a_log[l].astype(F32)).reshape(1, -1)
        dtb = dt_bias[l].astype(F32).reshape(1, -1)
        pad = LANES - rate.shape[1]
        wr = jnp.pad(w_router[l].astype(F32), ((0, 0), (0, LANES - N_EXPERTS)))
        wr_hi = wr.astype(BF16)
        w_all, wvt = _pack_w_in(w_in[l])
        layers.append(dict(
            norm_mix=norm_mix[l].astype(F32).reshape(1, -1),
            w_all=w_all,
            wvt=wvt,
            qn=jnp.tile(q_norm[l].astype(F32), LANES // HEAD_DIM).reshape(1, -1),
            kn=jnp.tile(k_norm[l].astype(F32), LANES // HEAD_DIM).reshape(1, -1),
            conv_w=conv_w[l].astype(F32),
            rate=jnp.pad(rate, ((0, 0), (0, pad))),
            dtb=jnp.pad(dtb, ((0, 0), (0, pad))),
            on=jnp.tile(o_norm[l].astype(F32), N_HEADS_D).reshape(1, -1),
            w_attn_proj=w_attn_proj[l].astype(BF16),
            w_delta_proj=w_delta_proj[l].astype(BF16),
            w_out=w_out[l].astype(BF16),
            norm_ffn=norm_ffn[l].astype(F32).reshape(1, -1),
            wr=jnp.concatenate([wr_hi, (wr - wr_hi.astype(F32)).astype(BF16)], axis=1),
        ))
    ffn_w = (w_gate.astype(F32), w_up.astype(F32), w_down.astype(F32))
    nf = norm_final.astype(F32).reshape(1, -1)
    y_prompt = _trunk(x_prompt, meta_tokens, layers, ffn_w, nf)
    y_sample = _trunk(x_sample, meta_tokens, layers, ffn_w, nf)
    return (y_prompt, y_sample)
```

```python
import functools

import numpy as np
import jax
import jax.numpy as jnp
from jax import lax
from jax.experimental import pallas as pl
from jax.experimental.pallas import tpu as pltpu

F32 = jnp.float32
BF16 = jnp.bfloat16

D_MODEL = 1024
N_META = 16
GRID_W = 64
N_HEADS_A = 8
N_KV_HEADS = 2
HEAD_DIM = 64
ROPE_THETA = 10000.0
N_HEADS_D = 8
DK = 64
DV = 64
CHUNK = 64
N_EXPERTS = 16
EC_CAPACITY = 2
D_EXPERT = 1024
EPS = 1e-6

FRONT = 128
N_NULL = FRONT - N_META
LANES = 128
ROW_TILE = 384
HEAD_GROUP = 4
GROUP_W = HEAD_GROUP * DK
CHUNKS_PER_STEP = 2
VMEM_LIMIT = 48 * 1024 * 1024
Q_SCALE = HEAD_DIM ** -0.5 * float(np.log2(np.e))
PAIRS_PER_TRIP = 3
LANE_SLICES = 2
V_ROWS = HEAD_DIM + 16

C_Q = 0
C_K = C_Q + N_HEADS_A * LANES
C_QKVD = C_K + N_KV_HEADS * LANES
C_Z = C_QKVD + 3 * N_HEADS_D * DK
C_SM = C_Z + N_HEADS_D * DV
C_GA = C_SM + LANES
C_GD = C_GA + D_MODEL
C_END = C_GD + D_MODEL


def _cparams(sem):
    return pltpu.CompilerParams(dimension_semantics=sem, vmem_limit_bytes=VMEM_LIMIT)


def _dot(a, b):
    return jnp.dot(a, b, preferred_element_type=F32)


def _dot_nt(a, b):
    return lax.dot_general(a, b, (((1,), (1,)), ((), ())), preferred_element_type=F32)


def _split_dot(t, w_bf16):
    hi = t.astype(BF16)
    lo = (t - hi.astype(F32)).astype(BF16)
    return _dot(hi, w_bf16) + _dot(lo, w_bf16)


def _inproj_kernel(x_ref, g_ref, w_ref, wvt_ref, cos_ref, sin_ref, qn_ref, kn_ref,
                   q_ref, k_ref, vt_ref, qkvd_ref, z_ref, sm_ref, ga_ref, gd_ref):
    x = x_ref[...]
    ms = jnp.mean(x * x, axis=-1, keepdims=True)
    u = (x * lax.rsqrt(ms + EPS) * g_ref[...]).astype(BF16)
    cos = cos_ref[...]
    sin = sin_ref[...]
    lane = lax.broadcasted_iota(jnp.int32, (1, LANES), 1)
    first = (lane % 32) < 16

    def rope(t):
        rot = jnp.where(first, pltpu.roll(t, LANES - 16, 1), pltpu.roll(t, 16, 1))
        return t * cos + rot * sin

    def mm(c0, n):
        return _dot(u, w_ref[:, c0:c0 + n])

    def head(t, gain):
        msq = jnp.sum(t * t, axis=-1, keepdims=True) * (1.0 / HEAD_DIM)
        return rope(t * lax.rsqrt(msq + EPS) * gain)

    for hp in range(N_HEADS_A // 2):
        t2 = mm(C_Q + 2 * LANES * hp, 2 * LANES)
        for i in range(2):
            h = 2 * hp + i
            q_ref[:, LANES * h:LANES * (h + 1)] = (
                head(t2[:, LANES * i:LANES * (i + 1)], qn_ref[...]) * Q_SCALE).astype(BF16)
    t2 = mm(C_K, N_KV_HEADS * LANES)
    for j in range(N_KV_HEADS):
        k_ref[:, LANES * j:LANES * (j + 1)] = head(t2[:, LANES * j:LANES * (j + 1)], kn_ref[...]).astype(BF16)

    vt = _dot_nt(wvt_ref[...], u)
    vrow = lax.broadcasted_iota(jnp.int32, (N_KV_HEADS * V_ROWS, 1), 0) % V_ROWS
    vt_ref[...] = jnp.where(vrow < HEAD_DIM, vt, 1.0).astype(BF16)
    qkvd_ref[...] = mm(C_QKVD, C_Z - C_QKVD)
    z_ref[...] = mm(C_Z, C_SM - C_Z).astype(BF16)
    sm_ref[...] = mm(C_SM, LANES)
    ga_ref[...] = mm(C_GA, D_MODEL).astype(BF16)
    gd_ref[...] = mm(C_GD, D_MODEL).astype(BF16)


def _inproj(h2, g, w_all, wvt, cos, sin, qn, kn, lp):
    n = h2.shape[0]
    tm = ROW_TILE
    nt = lp // tm
    row = lambda i: (i, 0)
    const = lambda i: (0, 0)
    tab = lambda i: (i % nt, 0)
    out_shapes = (
        jax.ShapeDtypeStruct((n, N_HEADS_A * LANES), BF16),
        jax.ShapeDtypeStruct((n, N_KV_HEADS * LANES), BF16),
        jax.ShapeDtypeStruct((N_KV_HEADS * V_ROWS, n), BF16),
        jax.ShapeDtypeStruct((n, C_Z - C_QKVD), F32),
        jax.ShapeDtypeStruct((n, C_SM - C_Z), BF16),
        jax.ShapeDtypeStruct((n, LANES), F32),
        jax.ShapeDtypeStruct((n, D_MODEL), BF16),
        jax.ShapeDtypeStruct((n, D_MODEL), BF16),
    )
    out_specs = [pl.BlockSpec((tm, s.shape[1]), row) for s in out_shapes]
    out_specs[2] = pl.BlockSpec((N_KV_HEADS * V_ROWS, tm), lambda i: (0, i))
    return pl.pallas_call(
        _inproj_kernel,
        grid=(n // tm,),
        in_specs=[
            pl.BlockSpec((tm, D_MODEL), row),
            pl.BlockSpec((1, D_MODEL), const),
            pl.BlockSpec((D_MODEL, C_END), const, pipeline_mode=pl.Buffered(1)),
            pl.BlockSpec((N_KV_HEADS * V_ROWS, D_MODEL), const, pipeline_mode=pl.Buffered(1)),
            pl.BlockSpec((tm, LANES), tab),
            pl.BlockSpec((tm, LANES), tab),
            pl.BlockSpec((1, LANES), const),
            pl.BlockSpec((1, LANES), const),
        ],
        out_specs=tuple(out_specs),
        out_shape=out_shapes,
        compiler_params=_cparams(("parallel",)),
        name="inproj",
    )(h2, g, w_all, wvt, cos, sin, qn, kn)


def _flash_kernel(q_ref, k_ref, vt_ref, o_ref, qs_ref, m_ref, acc_ref, sa_ref, sb_ref, ca_ref, cb_ref,
                  *, tq, tk, s_real):
    g = N_HEADS_A // N_KV_HEADS
    r = g * tq
    for i in range(g):
        qs_ref[i * tq:(i + 1) * tq, :] = q_ref[0, :, LANES * i:LANES * (i + 1)]
    m_ref[...] = jnp.full(m_ref.shape, -jnp.inf, F32)
    acc_ref[...] = jnp.zeros(acc_ref.shape, F32)

    def scores(koff, size):
        return _dot_nt(k_ref[0, pl.ds(koff, size), :], qs_ref[...])

    def absorb(s, cmax, koff, size):
        m_prev = m_ref[...]
        m_new = jnp.maximum(m_prev, cmax)
        alpha = jnp.exp2(m_prev - m_new)
        p = jnp.exp2(s - m_new).astype(BF16)
        acc_ref[...] = alpha * acc_ref[...] + _dot(vt_ref[:, pl.ds(koff, size)], p)
        m_ref[...] = m_new

    def produce(koff, s_ref, c_ref):
        s = scores(koff, tk)
        s_ref[...] = s
        c_ref[...] = jnp.max(s, axis=0, keepdims=True)

    def main_off(c):
        off = FRONT + c * tk
        return off if isinstance(c, int) else pl.multiple_of(off, LANES)

    s = scores(0, FRONT)
    s = jnp.where(lax.broadcasted_iota(jnp.int32, (FRONT, 1), 0) >= N_NULL, s, -jnp.inf)

    n_main = s_real // tk
    if n_main:
        produce(main_off(0), sa_ref, ca_ref)
    absorb(s, jnp.max(s, axis=0, keepdims=True), 0, FRONT)
    if n_main:
        n_pairs = (n_main - 1) // 2

        hw = r // LANE_SLICES

        def step(koff_next, nxt, koff_cur, cur):
            (sn_ref, cn_ref), (sc_ref, cc_ref) = nxt, cur
            for hh in range(LANE_SLICES):
                sl = slice(hh * hw, (hh + 1) * hw)
                s_new = _dot_nt(k_ref[0, pl.ds(koff_next, tk), :], qs_ref[sl, :])
                sn_ref[:, sl] = s_new
                cn_ref[:, sl] = jnp.max(s_new, axis=0, keepdims=True)
                m_prev = m_ref[:, sl]
                m_new = jnp.maximum(m_prev, cc_ref[:, sl])
                alpha = jnp.exp2(m_prev - m_new)
                p = jnp.exp2(sc_ref[:, sl] - m_new).astype(BF16)
                acc_ref[:, sl] = alpha * acc_ref[:, sl] + _dot(vt_ref[:, pl.ds(koff_cur, tk)], p)
                m_ref[:, sl] = m_new

        def pair(pi):
            buf_a, buf_b = (sa_ref, ca_ref), (sb_ref, cb_ref)
            step(main_off(2 * pi + 1), buf_b, main_off(2 * pi), buf_a)
            step(main_off(2 * pi + 2), buf_a, main_off(2 * pi + 1), buf_b)

        def body(t, carry):
            for u in range(PAIRS_PER_TRIP):
                pair(t * PAIRS_PER_TRIP + u)
            return carry

        lax.fori_loop(0, n_pairs // PAIRS_PER_TRIP, body, 0)
        for pi in range(n_pairs - n_pairs % PAIRS_PER_TRIP, n_pairs):
            pair(pi)
        if n_main - 2 * n_pairs == 2:
            step(main_off(2 * n_pairs + 1), (sb_ref, cb_ref), main_off(2 * n_pairs), (sa_ref, ca_ref))
            absorb(sb_ref[...], cb_ref[...], main_off(2 * n_pairs + 1), tk)
        else:
            absorb(sa_ref[...], ca_ref[...], main_off(2 * n_pairs), tk)
    if s_real % tk:
        s = scores(FRONT + n_main * tk, s_real % tk)
        absorb(s, jnp.max(s, axis=0, keepdims=True), FRONT + n_main * tk, s_real % tk)

    acc = acc_ref[...]
    out_t = acc[:HEAD_DIM] / acc[HEAD_DIM:HEAD_DIM + 1]
    for p in range(g // 2):
        pair = jnp.concatenate([out_t[:, (2 * p) * tq:(2 * p + 1) * tq],
                                out_t[:, (2 * p + 1) * tq:(2 * p + 2) * tq]], axis=0)
        o_ref[0, :, 2 * HEAD_DIM * p:2 * HEAD_DIM * (p + 1)] = pair.T.astype(BF16)


def _attention(q, k, vt, s_real):
    b, lp, _ = q.shape
    tq = ROW_TILE
    tk = 512
    g = N_HEADS_A // N_KV_HEADS
    return pl.pallas_call(
        functools.partial(_flash_kernel, tq=tq, tk=tk, s_real=s_real),
        grid=(b, N_KV_HEADS, lp // tq),
        in_specs=[
            pl.BlockSpec((1, tq, g * LANES), lambda bi, j, qi: (bi, qi, j)),
            pl.BlockSpec((1, lp, LANES), lambda bi, j, qi: (bi, 0, j)),
            pl.BlockSpec((V_ROWS, lp), lambda bi, j, qi: (j, bi)),
        ],
        out_specs=pl.BlockSpec((1, tq, g * HEAD_DIM), lambda bi, j, qi: (bi, qi, j)),
        out_shape=jax.ShapeDtypeStruct((b, lp, N_HEADS_A * HEAD_DIM), BF16),
        scratch_shapes=[
            pltpu.VMEM((g * tq, LANES), BF16),
            pltpu.VMEM((1, g * tq), F32),
            pltpu.VMEM((V_ROWS, g * tq), F32),
            pltpu.VMEM((tk, g * tq), F32),
            pltpu.VMEM((tk, g * tq), F32),
            pltpu.VMEM((1, g * tq), F32),
            pltpu.VMEM((1, g * tq), F32),
        ],
        compiler_params=_cparams(("parallel", "parallel", "arbitrary")),
        name="attention",
    )(q, k, vt)


def _dprep_kernel(prev_ref, cur_ref, next_ref, sm_ref, cw_ref, rate_ref, dtb_ref, seg_ref,
                  trif_ref, trib_ref, q_ref, k_ref, v_ref, gcf_ref, gcb_ref, bf_ref, bb_ref,
                  *, tm, n_tiles):
    i = pl.program_id(1)
    row = lax.broadcasted_iota(jnp.int32, (tm, 1), 0)
    valid = (row + i * tm) >= N_NULL
    x = jnp.where(valid, cur_ref[0], 0.0)
    prow = jnp.where(i > 0, prev_ref[0, 7:8, :], 0.0)
    nrow = jnp.where(i < n_tiles - 1, next_ref[0, 0:1, :], 0.0)
    xp = jnp.where(row == 0, prow, pltpu.roll(x, 1, 0))
    xn = jnp.where(row == tm - 1, nrow, pltpu.roll(x, tm - 1, 0))
    cw = cw_ref[...]
    y = cw[0:1] * xp + cw[1:2] * x + cw[2:3] * xn
    y = y * jax.nn.sigmoid(y)

    w = N_HEADS_D * DK
    seg = seg_ref[...]
    q = y[:, 0:w]
    q = q * lax.rsqrt(_split_dot(q * q, seg) + EPS) * (DK ** -0.5)
    k = y[:, w:2 * w]
    k = k * lax.rsqrt(_split_dot(k * k, seg) + EPS)
    q_ref[0] = jnp.where(valid, q, 0.0)
    k_ref[0] = jnp.where(valid, k, 0.0)
    v_ref[0] = jnp.where(valid, y[:, 2 * w:3 * w], 0.0)

    sm = sm_ref[0]
    t = sm + dtb_ref[...]
    softplus = jnp.maximum(t, 0.0) + jnp.log1p(jnp.exp(-jnp.abs(t)))
    g_all = jnp.where(valid, -rate_ref[...] * softplus, 0.0)
    beta_all = jnp.where(valid, jax.nn.sigmoid(sm), 0.0)
    g1 = g_all.astype(BF16)
    r1 = g_all - g1.astype(F32)
    g2 = r1.astype(BF16)
    g3 = (r1 - g2.astype(F32)).astype(BF16)
    trif = trif_ref[...]
    trib = trib_ref[...]
    gc_f = _dot(trif, g1) + _dot(trif, g2) + _dot(trif, g3)
    gc_b = _dot(trib, g1) + _dot(trib, g2) + _dot(trib, g3)

    lane = lax.broadcasted_iota(jnp.int32, (1, LANES), 1)
    low = lane < DK

    def expand(a, c0, out_ref):
        for p in range(N_HEADS_D // 2):
            e0 = jnp.broadcast_to(a[:, c0 + 2 * p:c0 + 2 * p + 1], (tm, LANES))
            e1 = jnp.broadcast_to(a[:, c0 + 2 * p + 1:c0 + 2 * p + 2], (tm, LANES))
            out_ref[0, :, LANES * p:LANES * (p + 1)] = jnp.where(low, e0, e1)

    expand(gc_f, 0, gcf_ref)
    expand(gc_b, N_HEADS_D, gcb_ref)
    expand(beta_all, 2 * N_HEADS_D, bf_ref)
    expand(beta_all, 3 * N_HEADS_D, bb_ref)


def _delta_prep(qkvd, sm, cw, rate, dtb, seg, trif, trib):
    b, lp, c = qkvd.shape
    tm = ROW_TILE
    nt = lp // tm
    w = N_HEADS_D * DK
    halo = 8
    nb8 = lp // halo
    cur = lambda bi, i: (bi, i, 0)
    const = lambda bi, i: (0, 0)
    out = jax.ShapeDtypeStruct((b, lp, w), F32)
    return pl.pallas_call(
        functools.partial(_dprep_kernel, tm=tm, n_tiles=nt),
        grid=(b, nt),
        in_specs=[
            pl.BlockSpec((1, halo, c), lambda bi, i: (bi, jnp.maximum(i * (tm // halo) - 1, 0), 0)),
            pl.BlockSpec((1, tm, c), cur),
            pl.BlockSpec((1, halo, c), lambda bi, i: (bi, jnp.minimum((i + 1) * (tm // halo), nb8 - 1), 0)),
            pl.BlockSpec((1, tm, LANES), cur),
            pl.BlockSpec((3, c), const),
            pl.BlockSpec((1, LANES), const),
            pl.BlockSpec((1, LANES), const),
            pl.BlockSpec((w, w), const),
            pl.BlockSpec((tm, tm), const),
            pl.BlockSpec((tm, tm), const),
        ],
        out_specs=tuple(pl.BlockSpec((1, tm, w), cur) for _ in range(7)),
        out_shape=(out,) * 7,
        compiler_params=_cparams(("parallel", "parallel")),
        name="delta_prep",
    )(qkvd, qkvd, qkvd, sm, cw, rate, dtb, seg, trif, trib)


def _dscan_kernel(qf_ref, kf_ref, vf_ref, gf_ref, bf_ref, qb_ref, kb_ref, vb_ref, gb_ref, bb_ref,
                  of_ref, ob_ref, s_ref):
    i = pl.program_id(0)

    @pl.when(i == 0)
    def _():
        s_ref[...] = jnp.zeros(s_ref.shape, F32)

    c = CHUNK
    r = lax.broadcasted_iota(jnp.int32, (c, GROUP_W), 0)
    cj = lax.broadcasted_iota(jnp.int32, (c, GROUP_W), 1) % c
    eye = r == cj
    eyef = eye.astype(F32)
    bdmask = (lax.broadcasted_iota(jnp.int32, (GROUP_W, GROUP_W), 0) // c ==
              lax.broadcasted_iota(jnp.int32, (GROUP_W, GROUP_W), 1) // c)

    lane_low = lax.broadcasted_iota(jnp.int32, (c, LANES), 1) < DK
    zero_tile = jnp.zeros((c, LANES), BF16)

    def bd(t):
        rows = []
        for hd in range(HEAD_GROUP):
            tile = t[:, LANES * (hd // 2):LANES * (hd // 2 + 1)]
            tile = jnp.where(lane_low == (hd % 2 == 0), tile, 0.0).astype(BF16)
            rows.append(jnp.concatenate([tile, zero_tile] if hd < 2 else [zero_tile, tile], axis=1))
        return jnp.concatenate(rows, axis=0)

    n_grp = N_HEADS_D // HEAD_GROUP
    fwd_refs = (qf_ref, kf_ref, vf_ref, gf_ref, bf_ref, of_ref)
    bwd_refs = (qb_ref, kb_ref, vb_ref, gb_ref, bb_ref, ob_ref)
    chains = []
    for bi in range(of_ref.shape[0]):
        for grp in range(n_grp):
            for reverse in (False, True):
                sl = slice(GROUP_W * grp, GROUP_W * (grp + 1))
                chains.append((fwd_refs if not reverse else bwd_refs, bi, sl, reverse, len(chains)))
    n_chains = len(chains)
    items = []
    for step in range(CHUNKS_PER_STEP):
        for ch in chains:
            ci = CHUNKS_PER_STEP - 1 - step if ch[3] else step
            items.append(ch + (slice(ci * c, (ci + 1) * c),))

    def each(fn, *cols):
        return [fn(*args) for args in zip(*cols)]

    rev = [it[3] for it in items]
    q = [it[0][0][it[1], it[5], it[2]] for it in items]
    k = [it[0][1][it[1], it[5], it[2]] for it in items]
    v = [it[0][2][it[1], it[5], it[2]] for it in items]
    gc = [it[0][3][it[1], it[5], it[2]] for it in items]
    beta = [it[0][4][it[1], it[5], it[2]] for it in items]

    def decay_of(g, reverse):
        incl = (r <= cj) if reverse else (r >= cj)
        gcol = jnp.sum(jnp.where(eye, g, 0.0), axis=0, keepdims=True)
        return jnp.where(incl, jnp.exp(jnp.where(incl, g - gcol, 0.0)), 0.0)

    decay = each(decay_of, gc, rev)
    eg = each(jnp.exp, gc)
    glast = each(lambda g, reverse: g[0:1] if reverse else g[c - 1:c], gc, rev)
    kbeta = each(lambda a, b: a * b, k, beta)
    gq = each(lambda kb_, q_, k_: _dot_nt(jnp.concatenate([kb_, q_], axis=0).astype(BF16), bd(k_)),
              kbeta, q, k)
    a_intra = each(lambda g, d: g[c:] * d, gq, decay)
    m = each(lambda g, d, reverse: -jnp.where((r < cj) if reverse else (r > cj), g[:c] * d, 0.0),
             gq, decay, rev)
    p = each(lambda m_: eyef + m_, m)
    m = each(lambda m_: _dot(m_.astype(BF16), bd(m_)), m)
    for _ in range(4):
        rr = each(lambda m_, p_: _dot(jnp.concatenate([m_, p_], axis=0).astype(BF16), bd(m_)), m, p)
        p = each(lambda p_, rr_: p_ + rr_[c:], p, rr)
        m = each(lambda rr_: rr_[:c], rr)
    tmat = each(lambda p_, m_: (p_ + _dot(p_.astype(BF16), bd(m_))).astype(BF16), p, m)
    u = each(lambda t, v_, b: _dot(t, bd(v_ * b)), tmat, v, beta)
    w = each(lambda t, kb_, e: _dot(t, bd(kb_ * e)), tmat, kbeta, eg)
    state = [s_ref[ch[4]] for ch in chains]
    for step in range(CHUNKS_PER_STEP):
        part = slice(step * n_chains, (step + 1) * n_chains)
        wq = each(lambda w_, q_, e, s: _dot(jnp.concatenate([w_, q_ * e], axis=0).astype(BF16), s.astype(BF16)),
                  w[part], q[part], eg[part], state)
        v_new = each(lambda u_, wq_: u_ - wq_[:c], u[part], wq)
        o = each(lambda wq_, a, vn: wq_[c:] + _dot(a.astype(BF16), bd(vn)), wq, a_intra[part], v_new)
        upd = each(lambda k_, gl, g, vn: _dot((k_ * jnp.exp(gl - g)).T.astype(BF16), vn.astype(BF16)),
                   k[part], glast[part], gc[part], v_new)
        state = each(lambda s, gl, up: s * jnp.exp(gl) + jnp.where(bdmask, up, 0.0), state, glast[part], upd)
        for it, o_ in zip(items[part], o):
            it[0][5][it[1], it[5], it[2]] = o_
    for ch, s in zip(chains, state):
        s_ref[ch[4]] = s


def _delta_scan(q, k, v, gcf, gcb, bf, bb):
    b, lp, w = q.shape
    n = lp // (CHUNKS_PER_STEP * CHUNK)
    fwd = lambda i: (0, i, 0)
    bwd = lambda i: (0, n - 1 - i, 0)
    blk = (b, CHUNKS_PER_STEP * CHUNK, w)
    out = jax.ShapeDtypeStruct((b, lp, w), F32)
    return pl.pallas_call(
        _dscan_kernel,
        grid=(n,),
        in_specs=[pl.BlockSpec(blk, fwd)] * 5 + [pl.BlockSpec(blk, bwd)] * 5,
        out_specs=(pl.BlockSpec(blk, fwd), pl.BlockSpec(blk, bwd)),
        out_shape=(out, out),
        scratch_shapes=[pltpu.VMEM((b * 2 * N_HEADS_D // HEAD_GROUP, GROUP_W, GROUP_W), F32)],
        compiler_params=_cparams(("arbitrary",)),
        name="delta_scan",
    )(q, k, v, gcf, bf, q, k, v, gcb, bb)


def _post_kernel(h_ref, at_ref, of_ref, ob_ref, z_ref, ga_ref, gd_ref, on_ref, seg_ref,
                 wa_ref, wd_ref, wo_ref, gf_ref, wr_ref,
                 h_out, xn_out, aff_out, *, tm, n_tiles):
    od = of_ref[...] + ob_ref[...]
    msd = _split_dot(od * od, seg_ref[...]) * (1.0 / DV)
    z = z_ref[...].astype(F32)
    od = od * lax.rsqrt(msd + EPS) * on_ref[...] * (z * jax.nn.sigmoid(z))
    y_d = _dot(od.astype(BF16), wd_ref[...])
    y_a = _dot(at_ref[...], wa_ref[...])
    merged = (jax.nn.sigmoid(ga_ref[...].astype(F32)) * y_a +
              jax.nn.sigmoid(gd_ref[...].astype(F32)) * y_d)
    h = h_ref[...] + _dot(merged.astype(BF16), wo_ref[...])
    h_out[...] = h

    ms = jnp.mean(h * h, axis=-1, keepdims=True)
    xn = h * lax.rsqrt(ms + EPS) * gf_ref[...]
    xh = xn.astype(BF16)
    xn_out[...] = xh
    xl = (xn - xh.astype(F32)).astype(BF16)
    both = _dot(xh, wr_ref[...])
    logits = both[:, :LANES] + both[:, LANES:] + _dot(xl, wr_ref[:, :LANES])
    lane = lax.broadcasted_iota(jnp.int32, (1, LANES), 1)
    logits = jnp.where(lane < N_EXPERTS, logits, -jnp.inf)
    e = jnp.exp(logits - jnp.max(logits, axis=-1, keepdims=True))
    aff = e / jnp.sum(e, axis=-1, keepdims=True)
    row = lax.broadcasted_iota(jnp.int32, (tm, 1), 0) + (pl.program_id(0) % n_tiles) * tm
    aff_out[...] = jnp.where(row >= N_NULL, aff, -1.0)


def _post(h2, at, of, ob, z, ga, gd, on, seg, wa, wd, wo, gf, wr, lp):
    n = h2.shape[0]
    tm = ROW_TILE
    nt = lp // tm
    row = lambda i: (i, 0)
    const = lambda i: (0, 0)
    hw = N_HEADS_D * DV

    def wspec(shape):
        return pl.BlockSpec(shape, const, pipeline_mode=pl.Buffered(1))

    return pl.pallas_call(
        functools.partial(_post_kernel, tm=tm, n_tiles=nt),
        grid=(n // tm,),
        in_specs=[
            pl.BlockSpec((tm, D_MODEL), row),
            pl.BlockSpec((tm, hw), row),
            pl.BlockSpec((tm, hw), row),
            pl.BlockSpec((tm, hw), row),
            pl.BlockSpec((tm, hw), row),
            pl.BlockSpec((tm, D_MODEL), row),
            pl.BlockSpec((tm, D_MODEL), row),
            pl.BlockSpec((1, hw), const),
            wspec((hw, hw)),
            wspec((hw, D_MODEL)),
            wspec((hw, D_MODEL)),
            wspec((D_MODEL, D_MODEL)),
            pl.BlockSpec((1, D_MODEL), const),
            wspec((D_MODEL, 2 * LANES)),
        ],
        out_specs=(pl.BlockSpec((tm, D_MODEL), row), pl.BlockSpec((tm, D_MODEL), row),
                   pl.BlockSpec((tm, LANES), row)),
        out_shape=(jax.ShapeDtypeStruct((n, D_MODEL), F32), jax.ShapeDtypeStruct((n, D_MODEL), BF16),
                   jax.ShapeDtypeStruct((n, LANES), F32)),
        compiler_params=_cparams(("parallel",)),
        name="merge_out",
    )(h2, at, of, ob, z, ga, gd, on, seg, wa, wd, wo, gf, wr)


def _expert_kernel(x_ref, gate_ref, wg_ref, wu_ref, wd_ref, y_ref, wgb_ref, wub_ref, wdb_ref):
    @pl.when(pl.program_id(1) == 0)
    def _():
        wgb_ref[...] = wg_ref[0].astype(BF16)
        wub_ref[...] = wu_ref[0].astype(BF16)
        wdb_ref[...] = wd_ref[0].astype(BF16)

    x = x_ref[0]
    a = _dot(x, wgb_ref[...])
    b = _dot(x, wub_ref[...])
    hh = (a * jax.nn.sigmoid(a) * b).astype(BF16)
    y_ref[0] = _dot(hh, wdb_ref[...]) * gate_ref[0]


def _experts(xe, gates, wg, wu, wd, layer, tm):
    e, cp, d = xe.shape
    tile = lambda ei, i: (ei, i, 0)
    wsp = lambda ei, i: (layer, ei, 0, 0)
    return pl.pallas_call(
        _expert_kernel,
        grid=(e, cp // tm),
        in_specs=[
            pl.BlockSpec((1, tm, d), tile),
            pl.BlockSpec((1, tm, 1), tile),
            pl.BlockSpec((None, 1, d, D_EXPERT), wsp),
            pl.BlockSpec((None, 1, d, D_EXPERT), wsp),
            pl.BlockSpec((None, 1, D_EXPERT, d), wsp),
        ],
        out_specs=pl.BlockSpec((1, tm, d), tile),
        out_shape=jax.ShapeDtypeStruct((e, cp, d), F32),
        scratch_shapes=[pltpu.VMEM((d, D_EXPERT), BF16), pltpu.VMEM((d, D_EXPERT), BF16),
                        pltpu.VMEM((D_EXPERT, d), BF16)],
        compiler_params=_cparams(("parallel", "arbitrary")),
        name="experts",
    )(xe, gates, wg, wu, wd)


ROUTE_CHUNK = 128
ROUTE_WINDOW = 2 * ROUTE_CHUNK


def _route_select_kernel(aff_ref, tri_ref, pos_ref, off_ref, *, cap):
    n = aff_ref.shape[0]
    rc = ROUTE_CHUNK
    n_chunks = n // rc

    def bits_of(c):
        return pltpu.bitcast(aff_ref[pl.ds(pl.multiple_of(c * rc, rc), rc), :], jnp.int32)

    def count(pred):
        cb = ROW_TILE

        def body(c, acc):
            blk = pltpu.bitcast(aff_ref[pl.ds(pl.multiple_of(c * cb, cb), cb), :], jnp.int32)
            return acc + jnp.sum(pred(blk).astype(jnp.int32), axis=0, keepdims=True)
        return lax.fori_loop(0, n // cb, body, jnp.zeros((1, LANES), jnp.int32))

    def search(i, thr):
        cand = thr | jnp.left_shift(jnp.int32(1), 30 - i)
        return jnp.where(count(lambda b: b >= cand) >= cap, cand, thr)

    thr = lax.fori_loop(0, 31, search, jnp.zeros((1, LANES), jnp.int32))
    need = (cap - count(lambda b: b > thr)).astype(F32)
    tri = tri_ref[...]

    def emit(c, carry):
        ties_before, picks_before = carry
        b = bits_of(c)
        tie = b == thr
        tie_incl = _dot(tri, jnp.where(tie, 1.0, 0.0).astype(BF16)) + ties_before
        pick = (b > thr) | (tie & (tie_incl - 1.0 < need))
        pick_incl = _dot(tri, jnp.where(pick, 1.0, 0.0).astype(BF16)) + picks_before
        pos_ref[pl.ds(pl.multiple_of(c * rc, rc), rc), :] = jnp.where(pick, pick_incl - 1.0, -1.0)
        off_ref[c] = picks_before.astype(jnp.int32)
        return tie_incl[rc - 1:rc], pick_incl[rc - 1:rc]

    zero = jnp.zeros((1, LANES), F32)
    lax.fori_loop(0, n_chunks, emit, (zero, zero))


def _route_compact_kernel(off_ref, pos_ref, aff_ref, idx_ref, gate_ref):
    c = pl.program_id(0)
    rc = ROUTE_CHUNK

    @pl.when(c == 0)
    def _():
        idx_ref[...] = jnp.zeros(idx_ref.shape, F32)
        gate_ref[...] = jnp.zeros(gate_ref.shape, F32)

    token = (lax.broadcasted_iota(jnp.int32, (rc, 1), 0) + c * rc).astype(F32)
    slot = lax.broadcasted_iota(jnp.int32, (1, ROUTE_WINDOW), 1).astype(F32)
    pos = pos_ref[...]
    aff = aff_ref[...]
    for e in range(N_EXPERTS):
        start = pl.multiple_of((off_ref[c * N_EXPERTS + e] // LANES) * LANES, LANES)
        hit = (pos[:, e:e + 1] - start.astype(F32)) == slot
        win = (slice(e, e + 1), pl.ds(start, ROUTE_WINDOW))
        idx_ref[win] += jnp.sum(jnp.where(hit, token, 0.0), axis=0, keepdims=True)
        gate_ref[win] += jnp.sum(jnp.where(hit, aff[:, e:e + 1], 0.0), axis=0, keepdims=True)


def _route(aff, cap, slots):
    n = aff.shape[0]
    rc = ROUTE_CHUNK
    n_chunks = n // rc
    tri = jnp.asarray(np.tril(np.ones((rc, rc), np.float32)), BF16)
    whole = pl.BlockSpec(memory_space=pltpu.VMEM)
    pos, off = pl.pallas_call(
        functools.partial(_route_select_kernel, cap=cap),
        in_specs=[whole, whole],
        out_specs=(whole, whole),
        out_shape=(jax.ShapeDtypeStruct((n, LANES), F32),
                   jax.ShapeDtypeStruct((n_chunks, 1, LANES), jnp.int32)),
        compiler_params=pltpu.CompilerParams(vmem_limit_bytes=VMEM_LIMIT),
        name="route_select",
    )(aff, tri)
    off = off[:, 0, :N_EXPERTS].reshape(-1)
    chunk = lambda c, off_ref: (c, 0)
    fixed = lambda c, off_ref: (0, 0)
    out = jax.ShapeDtypeStruct((N_EXPERTS, slots), F32)
    idx, gates = pl.pallas_call(
        _route_compact_kernel,
        grid_spec=pltpu.PrefetchScalarGridSpec(
            num_scalar_prefetch=1,
            grid=(n_chunks,),
            in_specs=[pl.BlockSpec((rc, LANES), chunk), pl.BlockSpec((rc, LANES), chunk)],
            out_specs=(pl.BlockSpec((N_EXPERTS, slots), fixed), pl.BlockSpec((N_EXPERTS, slots), fixed)),
        ),
        out_shape=(out, out),
        compiler_params=_cparams(("arbitrary",)),
        name="route_compact",
    )(off, pos, aff)
    return idx.astype(jnp.int32), gates


def _final_kernel(*refs):
    *x_refs, g_ref, o_ref = refs
    tm = x_refs[0].shape[1]
    for i, x_ref in enumerate(x_refs):
        x = x_ref[0]
        ms = jnp.mean(x * x, axis=-1, keepdims=True)
        o_ref[0, i * tm:(i + 1) * tm, :] = x * lax.rsqrt(ms + EPS) * g_ref[...]


def _final_norm(h3, g, s):
    b = h3.shape[0]
    tm = FRONT
    per_step = 4 if s % (4 * tm) == 0 else 1

    def in_spec(k):
        return pl.BlockSpec((1, tm, D_MODEL), lambda bi, i: (bi, per_step * i + k + 1, 0))

    return pl.pallas_call(
        _final_kernel,
        grid=(b, s // (per_step * tm)),
        in_specs=[in_spec(k) for k in range(per_step)] + [pl.BlockSpec((1, D_MODEL), lambda bi, i: (0, 0))],
        out_specs=pl.BlockSpec((1, per_step * tm, D_MODEL), lambda bi, i: (bi, i, 0)),
        out_shape=jax.ShapeDtypeStruct((b, s, D_MODEL), F32),
        compiler_params=_cparams(("parallel", "parallel")),
        name="final_norm",
    )(*([h3] * per_step), g)


def _rope_tables(s):
    lp = FRONT + s
    t = np.arange(lp) - FRONT
    real = t >= 0
    pos = np.stack([np.where(real, t // GRID_W, 0), np.where(real, t % GRID_W, 0)], axis=-1)
    n_freq = HEAD_DIM // 4
    inv_freq = jnp.asarray(ROPE_THETA, F32) ** (-jnp.arange(n_freq, dtype=F32) / n_freq)
    lane = np.arange(LANES) % HEAD_DIM
    axis = lane // (HEAD_DIM // 2)
    freq = lane % n_freq
    ang = jnp.asarray(pos, F32)[:, axis] * inv_freq[freq][None, :]
    sign = np.where((lane % (HEAD_DIM // 2)) < n_freq, -1.0, 1.0).astype(np.float32)
    return jnp.cos(ang), jnp.sin(ang) * sign[None, :]


def _pack_w_in(w):
    o = 0

    def take(n):
        nonlocal o
        part = w[:, o:o + n]
        o += n
        return part

    def head_blocks(part, n_heads):
        part = part.reshape(D_MODEL, n_heads, HEAD_DIM)
        part = jnp.pad(part, ((0, 0), (0, 0), (0, LANES - HEAD_DIM)))
        return part.reshape(D_MODEL, n_heads * LANES)

    q_a = head_blocks(take(N_HEADS_A * HEAD_DIM), N_HEADS_A)
    k_a = head_blocks(take(N_KV_HEADS * HEAD_DIM), N_KV_HEADS)
    v_a = take(N_KV_HEADS * HEAD_DIM).reshape(D_MODEL, N_KV_HEADS, HEAD_DIM)
    v_a = jnp.pad(v_a, ((0, 0), (0, 0), (0, V_ROWS - HEAD_DIM))).reshape(D_MODEL, N_KV_HEADS * V_ROWS)
    qkv_d = take(3 * N_HEADS_D * DK)
    z_d = take(N_HEADS_D * DV)
    small = take(4 * N_HEADS_D)
    gate_a = take(D_MODEL)
    gate_d = take(D_MODEL)
    small = jnp.pad(small, ((0, 0), (0, LANES - small.shape[1])))
    w_all = jnp.concatenate([q_a, k_a, qkv_d, z_d, small, gate_a, gate_d], axis=1).astype(BF16)
    return w_all, v_a.T.astype(BF16)


def _chunk_tri(tm, reverse):
    i = np.arange(tm)
    same = (i[:, None] // CHUNK) == (i[None, :] // CHUNK)
    tri = (i[None, :] >= i[:, None]) if reverse else (i[None, :] <= i[:, None])
    return jnp.asarray((same & tri).astype(np.float32))


def _expert_tile(cap):
    n_tiles = -(-cap // 768)
    tm = -(-cap // n_tiles)
    tm = -(-tm // 16) * 16
    return tm, n_tiles


def _trunk(x, meta_tokens, layers, ffn_w, norm_final):
    b, s, d = x.shape
    lp = FRONT + s
    n = b * lp
    n_tok = b * (N_META + s)
    cap = EC_CAPACITY * n_tok // N_EXPERTS
    etm, ent = _expert_tile(cap)
    cap_pad = etm * ent
    slots = -(-max(cap_pad, cap + ROUTE_WINDOW) // LANES) * LANES
    cos, sin = _rope_tables(s)
    seg = jnp.asarray(np.kron(np.eye(N_HEADS_D), np.ones((DK, DK))), BF16)
    trif = _chunk_tri(ROW_TILE, False).astype(BF16)
    trib = _chunk_tri(ROW_TILE, True).astype(BF16)

    front = jnp.concatenate([jnp.zeros((N_NULL, d), F32), meta_tokens.astype(F32)], axis=0)
    h = jnp.concatenate([jnp.broadcast_to(front[None], (b, FRONT, d)), x], axis=1).reshape(n, d)

    for li, lw in enumerate(layers):
        q, k, vt, qkvd, z, sm, ga, gd = _inproj(h, lw["norm_mix"], lw["w_all"], lw["wvt"], cos, sin,
                                                lw["qn"], lw["kn"], lp)
        at = _attention(q.reshape(b, lp, -1), k.reshape(b, lp, -1), vt, s)
        qd, kd, vd, gcf, gcb, bf, bb = _delta_prep(qkvd.reshape(b, lp, -1), sm.reshape(b, lp, -1),
                                                   lw["conv_w"], lw["rate"], lw["dtb"], seg, trif, trib)
        of, ob = _delta_scan(qd, kd, vd, gcf, gcb, bf, bb)
        hw = N_HEADS_D * DV
        h, xn, aff = _post(h, at.reshape(n, hw), of.reshape(n, hw), ob.reshape(n, hw), z, ga, gd,
                           lw["on"], seg, lw["w_attn_proj"], lw["w_delta_proj"], lw["w_out"],
                           lw["norm_ffn"], lw["wr"], lp)
        idx, gates = _route(aff, cap, slots)
        idx = idx[:, :cap_pad]
        gates = gates[:, :cap_pad]
        xe = xn.at[idx.reshape(-1)].get(mode="promise_in_bounds").reshape(N_EXPERTS, cap_pad, d)
        ye = _experts(xe, gates[..., None], ffn_w[0], ffn_w[1], ffn_w[2], li, etm)
        h = h.at[idx.reshape(-1)].add(ye.reshape(-1, d), mode="promise_in_bounds")

    return _final_norm(h.reshape(b, lp, d), norm_final, s)


def kernel(x_prompt, x_sample, meta_tokens, norm_mix, w_in, q_norm, k_norm, conv_w, a_log, dt_bias,
           o_norm, w_attn_proj, w_delta_proj, w_out, norm_ffn, w_router, w_gate, w_up, w_down,
           norm_final):
    depth = w_in.shape[0]
    layers = []
    for l in range(depth):
        rate = jnp.exp(a_log[l].astype(F32)).reshape(1, -1)
        dtb = dt_bias[l].astype(F32).reshape(1, -1)
        pad = LANES - rate.shape[1]
        wr = jnp.pad(w_router[l].astype(F32), ((0, 0), (0, LANES - N_EXPERTS)))
        wr_hi = wr.astype(BF16)
        w_all, wvt = _pack_w_in(w_in[l])
        layers.append(dict(
            norm_mix=norm_mix[l].astype(F32).reshape(1, -1),
            w_all=w_all,
            wvt=wvt,
            qn=jnp.tile(q_norm[l].astype(F32), LANES // HEAD_DIM).reshape(1, -1),
            kn=jnp.tile(k_norm[l].astype(F32), LANES // HEAD_DIM).reshape(1, -1),
            conv_w=conv_w[l].astype(F32),
            rate=jnp.pad(rate, ((0, 0), (0, pad))),
            dtb=jnp.pad(dtb, ((0, 0), (0, pad))),
            on=jnp.tile(o_norm[l].astype(F32), N_HEADS_D).reshape(1, -1),
            w_attn_proj=w_attn_proj[l].astype(BF16),
            w_delta_proj=w_delta_proj[l].astype(BF16),
            w_out=w_out[l].astype(BF16),
            norm_ffn=norm_ffn[l].astype(F32).reshape(1, -1),
            wr=jnp.concatenate([wr_hi, (wr - wr_hi.astype(F32)).astype(BF16)], axis=1),
        ))
    ffn_w = (w_gate.astype(F32), w_up.astype(F32), w_down.astype(F32))
    nf = norm_final.astype(F32).reshape(1, -1)
    y_prompt = _trunk(x_prompt, meta_tokens, layers, ffn_w, nf)
    y_sample = _trunk(x_sample, meta_tokens, layers, ffn_w, nf)
    return (y_prompt, y_sample)
```

```python
import functools

import numpy as np
import jax
import jax.numpy as jnp
from jax import lax
from jax.experimental import pallas as pl
from jax.experimental.pallas import tpu as pltpu

F32 = jnp.float32
BF16 = jnp.bfloat16

D_MODEL = 1024
N_META = 16
GRID_W = 64
N_HEADS_A = 8
N_KV_HEADS = 2
HEAD_DIM = 64
ROPE_THETA = 10000.0
N_HEADS_D = 8
DK = 64
DV = 64
CHUNK = 64
N_EXPERTS = 16
EC_CAPACITY = 2
D_EXPERT = 1024
EPS = 1e-6

FRONT = 128
N_NULL = FRONT - N_META
LANES = 128
ROW_TILE = 384
HEAD_GROUP = 4
GROUP_W = HEAD_GROUP * DK
CHUNKS_PER_STEP = 2
VMEM_LIMIT = 48 * 1024 * 1024
Q_SCALE = HEAD_DIM ** -0.5 * float(np.log2(np.e))
PAIRS_PER_TRIP = 5
LANE_SLICES = 2
V_ROWS = HEAD_DIM + 16

C_Q = 0
C_K = C_Q + N_HEADS_A * LANES
C_QKVD = C_K + N_KV_HEADS * LANES
C_Z = C_QKVD + 3 * N_HEADS_D * DK
C_SM = C_Z + N_HEADS_D * DV
C_GA = C_SM + LANES
C_GD = C_GA + D_MODEL
C_END = C_GD + D_MODEL


def _cparams(sem):
    return pltpu.CompilerParams(dimension_semantics=sem, vmem_limit_bytes=VMEM_LIMIT)


def _dot(a, b):
    return jnp.dot(a, b, preferred_element_type=F32)


def _dot_nt(a, b):
    return lax.dot_general(a, b, (((1,), (1,)), ((), ())), preferred_element_type=F32)


def _split_dot(t, w_bf16):
    hi = t.astype(BF16)
    lo = (t - hi.astype(F32)).astype(BF16)
    return _dot(hi, w_bf16) + _dot(lo, w_bf16)


def _inproj_kernel(x_ref, g_ref, w_ref, wvt_ref, cos_ref, sin_ref, qn_ref, kn_ref,
                   q_ref, k_ref, vt_ref, qkvd_ref, z_ref, sm_ref, ga_ref, gd_ref):
    x = x_ref[...]
    ms = jnp.mean(x * x, axis=-1, keepdims=True)
    u = (x * lax.rsqrt(ms + EPS) * g_ref[...]).astype(BF16)
    cos = cos_ref[...]
    sin = sin_ref[...]
    lane = lax.broadcasted_iota(jnp.int32, (1, LANES), 1)
    first = (lane % 32) < 16

    def rope(t):
        rot = jnp.where(first, pltpu.roll(t, LANES - 16, 1), pltpu.roll(t, 16, 1))
        return t * cos + rot * sin

    def mm(c0, n):
        return _dot(u, w_ref[:, c0:c0 + n])

    def head(t, gain):
        msq = jnp.sum(t * t, axis=-1, keepdims=True) * (1.0 / HEAD_DIM)
        return rope(t * lax.rsqrt(msq + EPS) * gain)

    for hp in range(N_HEADS_A // 2):
        t2 = mm(C_Q + 2 * LANES * hp, 2 * LANES)
        for i in range(2):
            h = 2 * hp + i
            q_ref[:, LANES * h:LANES * (h + 1)] = (
                head(t2[:, LANES * i:LANES * (i + 1)], qn_ref[...]) * Q_SCALE).astype(BF16)
    t2 = mm(C_K, N_KV_HEADS * LANES)
    for j in range(N_KV_HEADS):
        k_ref[:, LANES * j:LANES * (j + 1)] = head(t2[:, LANES * j:LANES * (j + 1)], kn_ref[...]).astype(BF16)

    vt = _dot_nt(wvt_ref[...], u)
    vrow = lax.broadcasted_iota(jnp.int32, (N_KV_HEADS * V_ROWS, 1), 0) % V_ROWS
    vt_ref[...] = jnp.where(vrow < HEAD_DIM, vt, 1.0).astype(BF16)
    qkvd_ref[...] = mm(C_QKVD, C_Z - C_QKVD)
    z_ref[...] = mm(C_Z, C_SM - C_Z).astype(BF16)
    sm_ref[...] = mm(C_SM, LANES)
    ga_ref[...] = mm(C_GA, D_MODEL).astype(BF16)
    gd_ref[...] = mm(C_GD, D_MODEL).astype(BF16)


def _inproj(h2, g, w_all, wvt, cos, sin, qn, kn, lp):
    n = h2.shape[0]
    tm = ROW_TILE
    nt = lp // tm
    row = lambda i: (i, 0)
    const = lambda i: (0, 0)
    tab = lambda i: (i % nt, 0)
    out_shapes = (
        jax.ShapeDtypeStruct((n, N_HEADS_A * LANES), BF16),
        jax.ShapeDtypeStruct((n, N_KV_HEADS * LANES), BF16),
        jax.ShapeDtypeStruct((N_KV_HEADS * V_ROWS, n), BF16),
        jax.ShapeDtypeStruct((n, C_Z - C_QKVD), F32),
        jax.ShapeDtypeStruct((n, C_SM - C_Z), BF16),
        jax.ShapeDtypeStruct((n, LANES), F32),
        jax.ShapeDtypeStruct((n, D_MODEL), BF16),
        jax.ShapeDtypeStruct((n, D_MODEL), BF16),
    )
    out_specs = [pl.BlockSpec((tm, s.shape[1]), row) for s in out_shapes]
    out_specs[2] = pl.BlockSpec((N_KV_HEADS * V_ROWS, tm), lambda i: (0, i))
    return pl.pallas_call(
        _inproj_kernel,
        grid=(n // tm,),
        in_specs=[
            pl.BlockSpec((tm, D_MODEL), row),
            pl.BlockSpec((1, D_MODEL), const),
            pl.BlockSpec((D_MODEL, C_END), const, pipeline_mode=pl.Buffered(1)),
            pl.BlockSpec((N_KV_HEADS * V_ROWS, D_MODEL), const, pipeline_mode=pl.Buffered(1)),
            pl.BlockSpec((tm, LANES), tab),
            pl.BlockSpec((tm, LANES), tab),
            pl.BlockSpec((1, LANES), const),
            pl.BlockSpec((1, LANES), const),
        ],
        out_specs=tuple(out_specs),
        out_shape=out_shapes,
        compiler_params=_cparams(("parallel",)),
        name="inproj",
    )(h2, g, w_all, wvt, cos, sin, qn, kn)


def _flash_kernel(q_ref, k_ref, vt_ref, o_ref, qs_ref, m_ref, acc_ref, sa_ref, sb_ref, ca_ref, cb_ref,
                  *, tq, tk, s_real):
    g = N_HEADS_A // N_KV_HEADS
    r = g * tq
    for i in range(g):
        qs_ref[i * tq:(i + 1) * tq, :] = q_ref[0, :, LANES * i:LANES * (i + 1)]
    m_ref[...] = jnp.full(m_ref.shape, -jnp.inf, F32)
    acc_ref[...] = jnp.zeros(acc_ref.shape, F32)

    def scores(koff, size):
        return _dot_nt(k_ref[0, pl.ds(koff, size), :], qs_ref[...])

    def absorb(s, cmax, koff, size):
        m_prev = m_ref[...]
        m_new = jnp.maximum(m_prev, cmax)
        alpha = jnp.exp2(m_prev - m_new)
        p = jnp.exp2(s - m_new).astype(BF16)
        acc_ref[...] = alpha * acc_ref[...] + _dot(vt_ref[:, pl.ds(koff, size)], p)
        m_ref[...] = m_new

    def produce(koff, s_ref, c_ref):
        s = scores(koff, tk)
        s_ref[...] = s
        c_ref[...] = jnp.max(s, axis=0, keepdims=True)

    def main_off(c):
        off = FRONT + c * tk
        return off if isinstance(c, int) else pl.multiple_of(off, LANES)

    s = scores(0, FRONT)
    s = jnp.where(lax.broadcasted_iota(jnp.int32, (FRONT, 1), 0) >= N_NULL, s, -jnp.inf)

    n_main = s_real // tk
    if n_main:
        produce(main_off(0), sa_ref, ca_ref)
    absorb(s, jnp.max(s, axis=0, keepdims=True), 0, FRONT)
    if n_main:
        n_pairs = (n_main - 1) // 2

        hw = r // LANE_SLICES

        def step(koff_next, nxt, koff_cur, cur):
            (sn_ref, cn_ref), (sc_ref, cc_ref) = nxt, cur
            for hh in range(LANE_SLICES):
                sl = slice(hh * hw, (hh + 1) * hw)
                s_new = _dot_nt(k_ref[0, pl.ds(koff_next, tk), :], qs_ref[sl, :])
                sn_ref[:, sl] = s_new
                cn_ref[:, sl] = jnp.max(s_new, axis=0, keepdims=True)
                m_prev = m_ref[:, sl]
                m_new = jnp.maximum(m_prev, cc_ref[:, sl])
                alpha = jnp.exp2(m_prev - m_new)
                p = jnp.exp2(sc_ref[:, sl] - m_new).astype(BF16)
                acc_ref[:, sl] = alpha * acc_ref[:, sl] + _dot(vt_ref[:, pl.ds(koff_cur, tk)], p)
                m_ref[:, sl] = m_new

        def pair(pi):
            buf_a, buf_b = (sa_ref, ca_ref), (sb_ref, cb_ref)
            step(main_off(2 * pi + 1), buf_b, main_off(2 * pi), buf_a)
            step(main_off(2 * pi + 2), buf_a, main_off(2 * pi + 1), buf_b)

        def body(t, carry):
            for u in range(PAIRS_PER_TRIP):
                pair(t * PAIRS_PER_TRIP + u)
            return carry

        lax.fori_loop(0, n_pairs // PAIRS_PER_TRIP, body, 0)
        for pi in range(n_pairs - n_pairs % PAIRS_PER_TRIP, n_pairs):
            pair(pi)
        if n_main - 2 * n_pairs == 2:
            step(main_off(2 * n_pairs + 1), (sb_ref, cb_ref), main_off(2 * n_pairs), (sa_ref, ca_ref))
            absorb(sb_ref[...], cb_ref[...], main_off(2 * n_pairs + 1), tk)
        else:
            absorb(sa_ref[...], ca_ref[...], main_off(2 * n_pairs), tk)
    if s_real % tk:
        s = scores(FRONT + n_main * tk, s_real % tk)
        absorb(s, jnp.max(s, axis=0, keepdims=True), FRONT + n_main * tk, s_real % tk)

    acc = acc_ref[...]
    out_t = acc[:HEAD_DIM] / acc[HEAD_DIM:HEAD_DIM + 1]
    for p in range(g // 2):
        pair = jnp.concatenate([out_t[:, (2 * p) * tq:(2 * p + 1) * tq],
                                out_t[:, (2 * p + 1) * tq:(2 * p + 2) * tq]], axis=0)
        o_ref[0, :, 2 * HEAD_DIM * p:2 * HEAD_DIM * (p + 1)] = pair.T.astype(BF16)


def _attention(q, k, vt, s_real):
    b, lp, _ = q.shape
    tq = ROW_TILE
    tk = 512
    g = N_HEADS_A // N_KV_HEADS
    return pl.pallas_call(
        functools.partial(_flash_kernel, tq=tq, tk=tk, s_real=s_real),
        grid=(b, N_KV_HEADS, lp // tq),
        in_specs=[
            pl.BlockSpec((1, tq, g * LANES), lambda bi, j, qi: (bi, qi, j)),
            pl.BlockSpec((1, lp, LANES), lambda bi, j, qi: (bi, 0, j)),
            pl.BlockSpec((V_ROWS, lp), lambda bi, j, qi: (j, bi)),
        ],
        out_specs=pl.BlockSpec((1, tq, g * HEAD_DIM), lambda bi, j, qi: (bi, qi, j)),
        out_shape=jax.ShapeDtypeStruct((b, lp, N_HEADS_A * HEAD_DIM), BF16),
        scratch_shapes=[
            pltpu.VMEM((g * tq, LANES), BF16),
            pltpu.VMEM((1, g * tq), F32),
            pltpu.VMEM((V_ROWS, g * tq), F32),
            pltpu.VMEM((tk, g * tq), F32),
            pltpu.VMEM((tk, g * tq), F32),
            pltpu.VMEM((1, g * tq), F32),
            pltpu.VMEM((1, g * tq), F32),
        ],
        compiler_params=_cparams(("parallel", "parallel", "arbitrary")),
        name="attention",
    )(q, k, vt)


def _dprep_kernel(prev_ref, cur_ref, next_ref, sm_ref, cw_ref, rate_ref, dtb_ref, seg_ref,
                  trif_ref, trib_ref, q_ref, k_ref, v_ref, gcf_ref, gcb_ref, bf_ref, bb_ref,
                  *, tm, n_tiles):
    i = pl.program_id(1)
    row = lax.broadcasted_iota(jnp.int32, (tm, 1), 0)
    valid = (row + i * tm) >= N_NULL
    x = jnp.where(valid, cur_ref[0], 0.0)
    prow = jnp.where(i > 0, prev_ref[0, 7:8, :], 0.0)
    nrow = jnp.where(i < n_tiles - 1, next_ref[0, 0:1, :], 0.0)
    xp = jnp.where(row == 0, prow, pltpu.roll(x, 1, 0))
    xn = jnp.where(row == tm - 1, nrow, pltpu.roll(x, tm - 1, 0))
    cw = cw_ref[...]
    y = cw[0:1] * xp + cw[1:2] * x + cw[2:3] * xn
    y = y * jax.nn.sigmoid(y)

    w = N_HEADS_D * DK
    seg = seg_ref[...]
    q = y[:, 0:w]
    q = q * lax.rsqrt(_split_dot(q * q, seg) + EPS) * (DK ** -0.5)
    k = y[:, w:2 * w]
    k = k * lax.rsqrt(_split_dot(k * k, seg) + EPS)
    q_ref[0] = jnp.where(valid, q, 0.0)
    k_ref[0] = jnp.where(valid, k, 0.0)
    v_ref[0] = jnp.where(valid, y[:, 2 * w:3 * w], 0.0)

    sm = sm_ref[0]
    t = sm + dtb_ref[...]
    softplus = jnp.maximum(t, 0.0) + jnp.log1p(jnp.exp(-jnp.abs(t)))
    g_all = jnp.where(valid, -rate_ref[...] * softplus, 0.0)
    beta_all = jnp.where(valid, jax.nn.sigmoid(sm), 0.0)
    g1 = g_all.astype(BF16)
    r1 = g_all - g1.astype(F32)
    g2 = r1.astype(BF16)
    g3 = (r1 - g2.astype(F32)).astype(BF16)
    trif = trif_ref[...]
    trib = trib_ref[...]
    gc_f = _dot(trif, g1) + _dot(trif, g2) + _dot(trif, g3)
    gc_b = _dot(trib, g1) + _dot(trib, g2) + _dot(trib, g3)

    lane = lax.broadcasted_iota(jnp.int32, (1, LANES), 1)
    low = lane < DK

    def expand(a, c0, out_ref):
        for p in range(N_HEADS_D // 2):
            e0 = jnp.broadcast_to(a[:, c0 + 2 * p:c0 + 2 * p + 1], (tm, LANES))
            e1 = jnp.broadcast_to(a[:, c0 + 2 * p + 1:c0 + 2 * p + 2], (tm, LANES))
            out_ref[0, :, LANES * p:LANES * (p + 1)] = jnp.where(low, e0, e1)

    expand(gc_f, 0, gcf_ref)
    expand(gc_b, N_HEADS_D, gcb_ref)
    expand(beta_all, 2 * N_HEADS_D, bf_ref)
    expand(beta_all, 3 * N_HEADS_D, bb_ref)


def _delta_prep(qkvd, sm, cw, rate, dtb, seg, trif, trib):
    b, lp, c = qkvd.shape
    tm = ROW_TILE
    nt = lp // tm
    w = N_HEADS_D * DK
    halo = 8
    nb8 = lp // halo
    cur = lambda bi, i: (bi, i, 0)
    const = lambda bi, i: (0, 0)
    out = jax.ShapeDtypeStruct((b, lp, w), F32)
    return pl.pallas_call(
        functools.partial(_dprep_kernel, tm=tm, n_tiles=nt),
        grid=(b, nt),
        in_specs=[
            pl.BlockSpec((1, halo, c), lambda bi, i: (bi, jnp.maximum(i * (tm // halo) - 1, 0), 0)),
            pl.BlockSpec((1, tm, c), cur),
            pl.BlockSpec((1, halo, c), lambda bi, i: (bi, jnp.minimum((i + 1) * (tm // halo), nb8 - 1), 0)),
            pl.BlockSpec((1, tm, LANES), cur),
            pl.BlockSpec((3, c), const),
            pl.BlockSpec((1, LANES), const),
            pl.BlockSpec((1, LANES), const),
            pl.BlockSpec((w, w), const),
            pl.BlockSpec((tm, tm), const),
            pl.BlockSpec((tm, tm), const),
        ],
        out_specs=tuple(pl.BlockSpec((1, tm, w), cur) for _ in range(7)),
        out_shape=(out,) * 7,
        compiler_params=_cparams(("parallel", "parallel")),
        name="delta_prep",
    )(qkvd, qkvd, qkvd, sm, cw, rate, dtb, seg, trif, trib)


def _dscan_kernel(qf_ref, kf_ref, vf_ref, gf_ref, bf_ref, qb_ref, kb_ref, vb_ref, gb_ref, bb_ref,
                  of_ref, ob_ref, s_ref):
    i = pl.program_id(0)

    @pl.when(i == 0)
    def _():
        s_ref[...] = jnp.zeros(s_ref.shape, F32)

    c = CHUNK
    r = lax.broadcasted_iota(jnp.int32, (c, GROUP_W), 0)
    cj = lax.broadcasted_iota(jnp.int32, (c, GROUP_W), 1) % c
    eye = r == cj
    eyef = eye.astype(F32)
    bdmask = (lax.broadcasted_iota(jnp.int32, (GROUP_W, GROUP_W), 0) // c ==
              lax.broadcasted_iota(jnp.int32, (GROUP_W, GROUP_W), 1) // c)

    lane_low = lax.broadcasted_iota(jnp.int32, (c, LANES), 1) < DK
    zero_tile = jnp.zeros((c, LANES), BF16)

    def bd(t):
        rows = []
        for hd in range(HEAD_GROUP):
            tile = t[:, LANES * (hd // 2):LANES * (hd // 2 + 1)]
            tile = jnp.where(lane_low == (hd % 2 == 0), tile, 0.0).astype(BF16)
            rows.append(jnp.concatenate([tile, zero_tile] if hd < 2 else [zero_tile, tile], axis=1))
        return jnp.concatenate(rows, axis=0)

    n_grp = N_HEADS_D // HEAD_GROUP
    fwd_refs = (qf_ref, kf_ref, vf_ref, gf_ref, bf_ref, of_ref)
    bwd_refs = (qb_ref, kb_ref, vb_ref, gb_ref, bb_ref, ob_ref)
    chains = []
    for bi in range(of_ref.shape[0]):
        for grp in range(n_grp):
            for reverse in (False, True):
                sl = slice(GROUP_W * grp, GROUP_W * (grp + 1))
                chains.append((fwd_refs if not reverse else bwd_refs, bi, sl, reverse, len(chains)))
    n_chains = len(chains)
    items = []
    for step in range(CHUNKS_PER_STEP):
        for ch in chains:
            ci = CHUNKS_PER_STEP - 1 - step if ch[3] else step
            items.append(ch + (slice(ci * c, (ci + 1) * c),))

    def each(fn, *cols):
        return [fn(*args) for args in zip(*cols)]

    rev = [it[3] for it in items]
    q = [it[0][0][it[1], it[5], it[2]] for it in items]
    k = [it[0][1][it[1], it[5], it[2]] for it in items]
    v = [it[0][2][it[1], it[5], it[2]] for it in items]
    gc = [it[0][3][it[1], it[5], it[2]] for it in items]
    beta = [it[0][4][it[1], it[5], it[2]] for it in items]

    def decay_of(g, reverse):
        incl = (r <= cj) if reverse else (r >= cj)
        gcol = jnp.sum(jnp.where(eye, g, 0.0), axis=0, keepdims=True)
        return jnp.where(incl, jnp.exp(jnp.where(incl, g - gcol, 0.0)), 0.0)

    decay = each(decay_of, gc, rev)
    eg = each(jnp.exp, gc)
    glast = each(lambda g, reverse: g[0:1] if reverse else g[c - 1:c], gc, rev)
    kbeta = each(lambda a, b: a * b, k, beta)
    gq = each(lambda kb_, q_, k_: _dot_nt(jnp.concatenate([kb_, q_], axis=0).astype(BF16), bd(k_)),
              kbeta, q, k)
    a_intra = each(lambda g, d: g[c:] * d, gq, decay)
    m = each(lambda g, d, reverse: -jnp.where((r < cj) if reverse else (r > cj), g[:c] * d, 0.0),
             gq, decay, rev)
    p = each(lambda m_: eyef + m_, m)
    m = each(lambda m_: _dot(m_.astype(BF16), bd(m_)), m)
    for _ in range(4):
        rr = each(lambda m_, p_: _dot(jnp.concatenate([m_, p_], axis=0).astype(BF16), bd(m_)), m, p)
        p = each(lambda p_, rr_: p_ + rr_[c:], p, rr)
        m = each(lambda rr_: rr_[:c], rr)
    tmat = each(lambda p_, m_: (p_ + _dot(p_.astype(BF16), bd(m_))).astype(BF16), p, m)
    u = each(lambda t, v_, b: _dot(t, bd(v_ * b)), tmat, v, beta)
    w = each(lambda t, kb_, e: _dot(t, bd(kb_ * e)), tmat, kbeta, eg)
    state = [s_ref[ch[4]] for ch in chains]
    for step in range(CHUNKS_PER_STEP):
        part = slice(step * n_chains, (step + 1) * n_chains)
        wq = each(lambda w_, q_, e, s: _dot(jnp.concatenate([w_, q_ * e], axis=0).astype(BF16), s.astype(BF16)),
                  w[part], q[part], eg[part], state)
        v_new = each(lambda u_, wq_: u_ - wq_[:c], u[part], wq)
        o = each(lambda wq_, a, vn: wq_[c:] + _dot(a.astype(BF16), bd(vn)), wq, a_intra[part], v_new)
        upd = each(lambda k_, gl, g, vn: _dot((k_ * jnp.exp(gl - g)).T.astype(BF16), vn.astype(BF16)),
                   k[part], glast[part], gc[part], v_new)
        state = each(lambda s, gl, up: s * jnp.exp(gl) + jnp.where(bdmask, up, 0.0), state, glast[part], upd)
        for it, o_ in zip(items[part], o):
            it[0][5][it[1], it[5], it[2]] = o_
    for ch, s in zip(chains, state):
        s_ref[ch[4]] = s


def _delta_scan(q, k, v, gcf, gcb, bf, bb):
    b, lp, w = q.shape
    n = lp // (CHUNKS_PER_STEP * CHUNK)
    fwd = lambda i: (0, i, 0)
    bwd = lambda i: (0, n - 1 - i, 0)
    blk = (b, CHUNKS_PER_STEP * CHUNK, w)
    out = jax.ShapeDtypeStruct((b, lp, w), F32)
    return pl.pallas_call(
        _dscan_kernel,
        grid=(n,),
        in_specs=[pl.BlockSpec(blk, fwd)] * 5 + [pl.BlockSpec(blk, bwd)] * 5,
        out_specs=(pl.BlockSpec(blk, fwd), pl.BlockSpec(blk, bwd)),
        out_shape=(out, out),
        scratch_shapes=[pltpu.VMEM((b * 2 * N_HEADS_D // HEAD_GROUP, GROUP_W, GROUP_W), F32)],
        compiler_params=_cparams(("arbitrary",)),
        name="delta_scan",
    )(q, k, v, gcf, bf, q, k, v, gcb, bb)


def _post_kernel(h_ref, at_ref, of_ref, ob_ref, z_ref, ga_ref, gd_ref, on_ref, seg_ref,
                 wa_ref, wd_ref, wo_ref, gf_ref, wr_ref,
                 h_out, xn_out, aff_out, *, tm, n_tiles):
    od = of_ref[...] + ob_ref[...]
    msd = _split_dot(od * od, seg_ref[...]) * (1.0 / DV)
    z = z_ref[...].astype(F32)
    od = od * lax.rsqrt(msd + EPS) * on_ref[...] * (z * jax.nn.sigmoid(z))
    y_d = _dot(od.astype(BF16), wd_ref[...])
    y_a = _dot(at_ref[...], wa_ref[...])
    merged = (jax.nn.sigmoid(ga_ref[...].astype(F32)) * y_a +
              jax.nn.sigmoid(gd_ref[...].astype(F32)) * y_d)
    h = h_ref[...] + _dot(merged.astype(BF16), wo_ref[...])
    h_out[...] = h

    ms = jnp.mean(h * h, axis=-1, keepdims=True)
    xn = h * lax.rsqrt(ms + EPS) * gf_ref[...]
    xh = xn.astype(BF16)
    xn_out[...] = xh
    xl = (xn - xh.astype(F32)).astype(BF16)
    both = _dot(xh, wr_ref[...])
    logits = both[:, :LANES] + both[:, LANES:] + _dot(xl, wr_ref[:, :LANES])
    lane = lax.broadcasted_iota(jnp.int32, (1, LANES), 1)
    logits = jnp.where(lane < N_EXPERTS, logits, -jnp.inf)
    e = jnp.exp(logits - jnp.max(logits, axis=-1, keepdims=True))
    aff = e / jnp.sum(e, axis=-1, keepdims=True)
    row = lax.broadcasted_iota(jnp.int32, (tm, 1), 0) + (pl.program_id(0) % n_tiles) * tm
    aff_out[...] = jnp.where(row >= N_NULL, aff, -1.0)


def _post(h2, at, of, ob, z, ga, gd, on, seg, wa, wd, wo, gf, wr, lp):
    n = h2.shape[0]
    tm = ROW_TILE
    nt = lp // tm
    row = lambda i: (i, 0)
    const = lambda i: (0, 0)
    hw = N_HEADS_D * DV

    def wspec(shape):
        return pl.BlockSpec(shape, const, pipeline_mode=pl.Buffered(1))

    return pl.pallas_call(
        functools.partial(_post_kernel, tm=tm, n_tiles=nt),
        grid=(n // tm,),
        in_specs=[
            pl.BlockSpec((tm, D_MODEL), row),
            pl.BlockSpec((tm, hw), row),
            pl.BlockSpec((tm, hw), row),
            pl.BlockSpec((tm, hw), row),
            pl.BlockSpec((tm, hw), row),
            pl.BlockSpec((tm, D_MODEL), row),
            pl.BlockSpec((tm, D_MODEL), row),
            pl.BlockSpec((1, hw), const),
            wspec((hw, hw)),
            wspec((hw, D_MODEL)),
            wspec((hw, D_MODEL)),
            wspec((D_MODEL, D_MODEL)),
            pl.BlockSpec((1, D_MODEL), const),
            wspec((D_MODEL, 2 * LANES)),
        ],
        out_specs=(pl.BlockSpec((tm, D_MODEL), row), pl.BlockSpec((tm, D_MODEL), row),
                   pl.BlockSpec((tm, LANES), row)),
        out_shape=(jax.ShapeDtypeStruct((n, D_MODEL), F32), jax.ShapeDtypeStruct((n, D_MODEL), BF16),
                   jax.ShapeDtypeStruct((n, LANES), F32)),
        compiler_params=_cparams(("parallel",)),
        name="merge_out",
    )(h2, at, of, ob, z, ga, gd, on, seg, wa, wd, wo, gf, wr)


def _expert_kernel(x_ref, gate_ref, wg_ref, wu_ref, wd_ref, y_ref, wgb_ref, wub_ref, wdb_ref):
    @pl.when(pl.program_id(1) == 0)
    def _():
        wgb_ref[...] = wg_ref[0].astype(BF16)
        wub_ref[...] = wu_ref[0].astype(BF16)
        wdb_ref[...] = wd_ref[0].astype(BF16)

    x = x_ref[0]
    a = _dot(x, wgb_ref[...])
    b = _dot(x, wub_ref[...])
    hh = (a * jax.nn.sigmoid(a) * b).astype(BF16)
    y_ref[0] = _dot(hh, wdb_ref[...]) * gate_ref[0]


def _experts(xe, gates, wg, wu, wd, layer, tm):
    e, cp, d = xe.shape
    tile = lambda ei, i: (ei, i, 0)
    wsp = lambda ei, i: (layer, ei, 0, 0)
    return pl.pallas_call(
        _expert_kernel,
        grid=(e, cp // tm),
        in_specs=[
            pl.BlockSpec((1, tm, d), tile),
            pl.BlockSpec((1, tm, 1), tile),
            pl.BlockSpec((None, 1, d, D_EXPERT), wsp),
            pl.BlockSpec((None, 1, d, D_EXPERT), wsp),
            pl.BlockSpec((None, 1, D_EXPERT, d), wsp),
        ],
        out_specs=pl.BlockSpec((1, tm, d), tile),
        out_shape=jax.ShapeDtypeStruct((e, cp, d), F32),
        scratch_shapes=[pltpu.VMEM((d, D_EXPERT), BF16), pltpu.VMEM((d, D_EXPERT), BF16),
                        pltpu.VMEM((D_EXPERT, d), BF16)],
        compiler_params=_cparams(("parallel", "arbitrary")),
        name="experts",
    )(xe, gates, wg, wu, wd)


ROUTE_CHUNK = 128
ROUTE_WINDOW = 2 * ROUTE_CHUNK


def _route_select_kernel(aff_ref, tri_ref, pos_ref, off_ref, *, cap):
    n = aff_ref.shape[0]
    rc = ROUTE_CHUNK
    n_chunks = n // rc

    def bits_of(c):
        return pltpu.bitcast(aff_ref[pl.ds(pl.multiple_of(c * rc, rc), rc), :], jnp.int32)

    def count(pred):
        cb = ROW_TILE

        def body(c, acc):
            blk = pltpu.bitcast(aff_ref[pl.ds(pl.multiple_of(c * cb, cb), cb), :], jnp.int32)
            return acc + jnp.sum(pred(blk).astype(jnp.int32), axis=0, keepdims=True)
        return lax.fori_loop(0, n // cb, body, jnp.zeros((1, LANES), jnp.int32))

    def search(i, thr):
        cand = thr | jnp.left_shift(jnp.int32(1), 30 - i)
        return jnp.where(count(lambda b: b >= cand) >= cap, cand, thr)

    thr = lax.fori_loop(0, 31, search, jnp.zeros((1, LANES), jnp.int32))
    need = (cap - count(lambda b: b > thr)).astype(F32)
    tri = tri_ref[...]

    def emit(c, carry):
        ties_before, picks_before = carry
        b = bits_of(c)
        tie = b == thr
        tie_incl = _dot(tri, jnp.where(tie, 1.0, 0.0).astype(BF16)) + ties_before
        pick = (b > thr) | (tie & (tie_incl - 1.0 < need))
        pick_incl = _dot(tri, jnp.where(pick, 1.0, 0.0).astype(BF16)) + picks_before
        pos_ref[pl.ds(pl.multiple_of(c * rc, rc), rc), :] = jnp.where(pick, pick_incl - 1.0, -1.0)
        off_ref[c] = picks_before.astype(jnp.int32)
        return tie_incl[rc - 1:rc], pick_incl[rc - 1:rc]

    zero = jnp.zeros((1, LANES), F32)
    lax.fori_loop(0, n_chunks, emit, (zero, zero))


def _route_compact_kernel(off_ref, pos_ref, aff_ref, idx_ref, gate_ref):
    c = pl.program_id(0)
    rc = ROUTE_CHUNK

    @pl.when(c == 0)
    def _():
        idx_ref[...] = jnp.zeros(idx_ref.shape, F32)
        gate_ref[...] = jnp.zeros(gate_ref.shape, F32)

    token = (lax.broadcasted_iota(jnp.int32, (rc, 1), 0) + c * rc).astype(F32)
    slot = lax.broadcasted_iota(jnp.int32, (1, ROUTE_WINDOW), 1).astype(F32)
    pos = pos_ref[...]
    aff = aff_ref[...]
    for e in range(N_EXPERTS):
        start = pl.multiple_of((off_ref[c * N_EXPERTS + e] // LANES) * LANES, LANES)
        hit = (pos[:, e:e + 1] - start.astype(F32)) == slot
        win = (slice(e, e + 1), pl.ds(start, ROUTE_WINDOW))
        idx_ref[win] += jnp.sum(jnp.where(hit, token, 0.0), axis=0, keepdims=True)
        gate_ref[win] += jnp.sum(jnp.where(hit, aff[:, e:e + 1], 0.0), axis=0, keepdims=True)


def _route(aff, cap, slots):
    n = aff.shape[0]
    rc = ROUTE_CHUNK
    n_chunks = n // rc
    tri = jnp.asarray(np.tril(np.ones((rc, rc), np.float32)), BF16)
    whole = pl.BlockSpec(memory_space=pltpu.VMEM)
    pos, off = pl.pallas_call(
        functools.partial(_route_select_kernel, cap=cap),
        in_specs=[whole, whole],
        out_specs=(whole, whole),
        out_shape=(jax.ShapeDtypeStruct((n, LANES), F32),
                   jax.ShapeDtypeStruct((n_chunks, 1, LANES), jnp.int32)),
        compiler_params=pltpu.CompilerParams(vmem_limit_bytes=VMEM_LIMIT),
        name="route_select",
    )(aff, tri)
    off = off[:, 0, :N_EXPERTS].reshape(-1)
    chunk = lambda c, off_ref: (c, 0)
    fixed = lambda c, off_ref: (0, 0)
    out = jax.ShapeDtypeStruct((N_EXPERTS, slots), F32)
    idx, gates = pl.pallas_call(
        _route_compact_kernel,
        grid_spec=pltpu.PrefetchScalarGridSpec(
            num_scalar_prefetch=1,
            grid=(n_chunks,),
            in_specs=[pl.BlockSpec((rc, LANES), chunk), pl.BlockSpec((rc, LANES), chunk)],
            out_specs=(pl.BlockSpec((N_EXPERTS, slots), fixed), pl.BlockSpec((N_EXPERTS, slots), fixed)),
        ),
        out_shape=(out, out),
        compiler_params=_cparams(("arbitrary",)),
        name="route_compact",
    )(off, pos, aff)
    return idx.astype(jnp.int32), gates


def _final_kernel(*refs):
    *x_refs, g_ref, o_ref = refs
    tm = x_refs[0].shape[1]
    for i, x_ref in enumerate(x_refs):
        x = x_ref[0]
        ms = jnp.mean(x * x, axis=-1, keepdims=True)
        o_ref[0, i * tm:(i + 1) * tm, :] = x * lax.rsqrt(ms + EPS) * g_ref[...]


def _final_norm(h3, g, s):
    b = h3.shape[0]
    tm = FRONT
    per_step = 4 if s % (4 * tm) == 0 else 1

    def in_spec(k):
        return pl.BlockSpec((1, tm, D_MODEL), lambda bi, i: (bi, per_step * i + k + 1, 0))

    return pl.pallas_call(
        _final_kernel,
        grid=(b, s // (per_step * tm)),
        in_specs=[in_spec(k) for k in range(per_step)] + [pl.BlockSpec((1, D_MODEL), lambda bi, i: (0, 0))],
        out_specs=pl.BlockSpec((1, per_step * tm, D_MODEL), lambda bi, i: (bi, i, 0)),
        out_shape=jax.ShapeDtypeStruct((b, s, D_MODEL), F32),
        compiler_params=_cparams(("parallel", "parallel")),
        name="final_norm",
    )(*([h3] * per_step), g)


def _rope_tables(s):
    lp = FRONT + s
    t = np.arange(lp) - FRONT
    real = t >= 0
    pos = np.stack([np.where(real, t // GRID_W, 0), np.where(real, t % GRID_W, 0)], axis=-1)
    n_freq = HEAD_DIM // 4
    inv_freq = jnp.asarray(ROPE_THETA, F32) ** (-jnp.arange(n_freq, dtype=F32) / n_freq)
    lane = np.arange(LANES) % HEAD_DIM
    axis = lane // (HEAD_DIM // 2)
    freq = lane % n_freq
    ang = jnp.asarray(pos, F32)[:, axis] * inv_freq[freq][None, :]
    sign = np.where((lane % (HEAD_DIM // 2)) < n_freq, -1.0, 1.0).astype(np.float32)
    return jnp.cos(ang), jnp.sin(ang) * sign[None, :]


def _pack_w_in(w):
    o = 0

    def take(n):
        nonlocal o
        part = w[:, o:o + n]
        o += n
        return part

    def head_blocks(part, n_heads):
        part = part.reshape(D_MODEL, n_heads, HEAD_DIM)
        part = jnp.pad(part, ((0, 0), (0, 0), (0, LANES - HEAD_DIM)))
        return part.reshape(D_MODEL, n_heads * LANES)

    q_a = head_blocks(take(N_HEADS_A * HEAD_DIM), N_HEADS_A)
    k_a = head_blocks(take(N_KV_HEADS * HEAD_DIM), N_KV_HEADS)
    v_a = take(N_KV_HEADS * HEAD_DIM).reshape(D_MODEL, N_KV_HEADS, HEAD_DIM)
    v_a = jnp.pad(v_a, ((0, 0), (0, 0), (0, V_ROWS - HEAD_DIM))).reshape(D_MODEL, N_KV_HEADS * V_ROWS)
    qkv_d = take(3 * N_HEADS_D * DK)
    z_d = take(N_HEADS_D * DV)
    small = take(4 * N_HEADS_D)
    gate_a = take(D_MODEL)
    gate_d = take(D_MODEL)
    small = jnp.pad(small, ((0, 0), (0, LANES - small.shape[1])))
    w_all = jnp.concatenate([q_a, k_a, qkv_d, z_d, small, gate_a, gate_d], axis=1).astype(BF16)
    return w_all, v_a.T.astype(BF16)


def _chunk_tri(tm, reverse):
    i = np.arange(tm)
    same = (i[:, None] // CHUNK) == (i[None, :] // CHUNK)
    tri = (i[None, :] >= i[:, None]) if reverse else (i[None, :] <= i[:, None])
    return jnp.asarray((same & tri).astype(np.float32))


def _expert_tile(cap):
    n_tiles = -(-cap // 768)
    tm = -(-cap // n_tiles)
    tm = -(-tm // 16) * 16
    return tm, n_tiles


def _trunk(x, meta_tokens, layers, ffn_w, norm_final):
    b, s, d = x.shape
    lp = FRONT + s
    n = b * lp
    n_tok = b * (N_META + s)
    cap = EC_CAPACITY * n_tok // N_EXPERTS
    etm, ent = _expert_tile(cap)
    cap_pad = etm * ent
    slots = -(-max(cap_pad, cap + ROUTE_WINDOW) // LANES) * LANES
    cos, sin = _rope_tables(s)
    seg = jnp.asarray(np.kron(np.eye(N_HEADS_D), np.ones((DK, DK))), BF16)
    trif = _chunk_tri(ROW_TILE, False).astype(BF16)
    trib = _chunk_tri(ROW_TILE, True).astype(BF16)

    front = jnp.concatenate([jnp.zeros((N_NULL, d), F32), meta_tokens.astype(F32)], axis=0)
    h = jnp.concatenate([jnp.broadcast_to(front[None], (b, FRONT, d)), x], axis=1).reshape(n, d)

    for li, lw in enumerate(layers):
        q, k, vt, qkvd, z, sm, ga, gd = _inproj(h, lw["norm_mix"], lw["w_all"], lw["wvt"], cos, sin,
                                                lw["qn"], lw["kn"], lp)
        at = _attention(q.reshape(b, lp, -1), k.reshape(b, lp, -1), vt, s)
        qd, kd, vd, gcf, gcb, bf, bb = _delta_prep(qkvd.reshape(b, lp, -1), sm.reshape(b, lp, -1),
                                                   lw["conv_w"], lw["rate"], lw["dtb"], seg, trif, trib)
        of, ob = _delta_scan(qd, kd, vd, gcf, gcb, bf, bb)
        hw = N_HEADS_D * DV
        h, xn, aff = _post(h, at.reshape(n, hw), of.reshape(n, hw), ob.reshape(n, hw), z, ga, gd,
                           lw["on"], seg, lw["w_attn_proj"], lw["w_delta_proj"], lw["w_out"],
                           lw["norm_ffn"], lw["wr"], lp)
        idx, gates = _route(aff, cap, slots)
        idx = idx[:, :cap_pad]
        gates = gates[:, :cap_pad]
        xe = xn.at[idx.reshape(-1)].get(mode="promise_in_bounds").reshape(N_EXPERTS, cap_pad, d)
        ye = _experts(xe, gates[..., None], ffn_w[0], ffn_w[1], ffn_w[2], li, etm)
        h = h.at[idx.reshape(-1)].add(ye.reshape(-1, d), mode="promise_in_bounds")

    return _final_norm(h.reshape(b, lp, d), norm_final, s)


def kernel(x_prompt, x_sample, meta_tokens, norm_mix, w_in, q_norm, k_norm, conv_w, a_log, dt_bias,
           o_norm, w_attn_proj, w_delta_proj, w_out, norm_ffn, w_router, w_gate, w_up, w_down,
           norm_final):
    depth = w_in.shape[0]
    layers = []
    for l in range(depth):
        rate = jnp.exp(a_log[l].astype(F32)).reshape(1, -1)
        dtb = dt_bias[l].astype(F32).reshape(1, -1)
        pad = LANES - rate.shape[1]
        wr = jnp.pad(w_router[l].astype(F32), ((0, 0), (0, LANES - N_EXPERTS)))
        wr_hi = wr.astype(BF16)
        w_all, wvt = _pack_w_in(w_in[l])
        layers.append(dict(
            norm_mix=norm_mix[l].astype(F32).reshape(1, -1),
            w_all=w_all,
            wvt=wvt,
            qn=jnp.tile(q_norm[l].astype(F32), LANES // HEAD_DIM).reshape(1, -1),
            kn=jnp.tile(k_norm[l].astype(F32), LANES // HEAD_DIM).reshape(1, -1),
            conv_w=conv_w[l].astype(F32),
            rate=jnp.pad(rate, ((0, 0), (0, pad))),
            dtb=jnp.pad(dtb, ((0, 0), (0, pad))),
            on=jnp.tile(o_norm[l].astype(F32), N_HEADS_D).reshape(1, -1),
            w_attn_proj=w_attn_proj[l].astype(BF16),
            w_delta_proj=w_delta_proj[l].astype(BF16),
            w_out=w_out[l].astype(BF16),
            norm_ffn=norm_ffn[l].astype(F32).reshape(1, -1),
            wr=jnp.concatenate([wr_hi, (wr - wr_hi.astype(F32)).astype(BF16)], axis=1),
        ))
    ffn_w = (w_gate.astype(F32), w_up.astype(F32), w_down.astype(F32))
    nf = norm_final.astype(F32).reshape(1, -1)
    y_prompt = _trunk(x_prompt, meta_tokens, layers, ffn_w, nf)
    y_sample = _trunk(x_sample, meta_tokens, layers, ffn_w, nf)
    return (y_prompt, y_sample)
```
